```python
import math
import jax, jax.numpy as jnp
from jax import lax
import numpy as np

D_MODEL = 1024
BATCH = 2
SEQ = 16384
DEPTH = 2

HEAD_DIM = 64
N_Q_HEADS = D_MODEL // HEAD_DIM
N_KV_HEADS = 4
GQA_REP = N_Q_HEADS // N_KV_HEADS
ROPE_DIM = HEAD_DIM // 4
ROPE_THETA = 500000.0
ATTN_SCALE = HEAD_DIM ** -0.5
Q_BLOCK = 128
KV_COLS = N_KV_HEADS * HEAD_DIM
N_BRANCH = 3
CMP_LEN = 32
CMP_STRIDE = 16
CMP_HIDDEN = 256
SEL_LEN = 64
N_SELECT = 16
WIN_LEN = 512
FORCE_SCORE = 1.0e4
NSA_COLS = D_MODEL + N_BRANCH * 2 * KV_COLS + N_Q_HEADS * N_BRANCH
DIL_PATTERNS = ((128, 1), (512, 4), (2048, 16))
DIL_BLOCK = 128
DIL_COLS = D_MODEL + len(DIL_PATTERNS) * 2 * KV_COLS
N_EXPERTS = 32
N_GROUPS = 4
EXPERTS_PER_GROUP = N_EXPERTS // N_GROUPS
TOP_K = 2
D_EXPERT = 512
MOE_BLOCK = 128
DN_ALPHA = (2.0 * DEPTH) ** 0.25
DN_BETA = (8.0 * DEPTH) ** -0.25
LN_EPS = 1e-5
N_A = (DEPTH + 1) // 2
N_B = DEPTH // 2

kernel_name = 'hybrid_nsa_dilated_grouped_moe_deepnorm'


def layer_norm(x, g, b):
    xf = x.astype(jnp.float32)
    mu = xf.mean(-1, keepdims=True)
    var = jnp.square(xf - mu).mean(-1, keepdims=True)
    return ((xf - mu) * lax.rsqrt(var + LN_EPS) * g.astype(jnp.float32) + b.astype(jnp.float32)).astype(x.dtype)


def partial_rope(t, pos):
    half = ROPE_DIM // 2
    inv = ROPE_THETA ** (-jnp.arange(half, dtype=jnp.float32) * (2.0 / ROPE_DIM))
    ang = pos.astype(jnp.float32)[:, :, None, None] * inv
    cos, sin = jnp.cos(ang), jnp.sin(ang)
    tf = t.astype(jnp.float32)
    t1, t2, rest = tf[..., :half], tf[..., half:ROPE_DIM], tf[..., ROPE_DIM:]
    return jnp.concatenate([t1 * cos - t2 * sin, t1 * sin + t2 * cos, rest], -1).astype(t.dtype)


def masked_softmax(s, mask):
    s = jnp.where(mask, s.astype(jnp.float32), -jnp.inf)
    m = jnp.max(s, axis=-1, keepdims=True)
    m = jnp.where(jnp.isfinite(m), m, 0.0)
    e = jnp.where(mask, jnp.exp(s - m), 0.0)
    return e / jnp.maximum(e.sum(-1, keepdims=True), 1e-30)


def nsa_mixer(h, positions, w_in, pos_k, w1_k, w2_k, pos_v, w1_v, w2_v, w_o):
    B, S, _ = h.shape
    G, R, E, H = N_KV_HEADS, GQA_REP, HEAD_DIM, N_Q_HEADS
    proj = h @ w_in
    q = partial_rope(proj[..., :D_MODEL].reshape(B, S, H, E), positions)
    kv = proj[..., D_MODEL:D_MODEL + N_BRANCH * 2 * KV_COLS].reshape(B, S, N_BRANCH, 2, G, E)
    gate = jax.nn.sigmoid(proj[..., D_MODEL + N_BRANCH * 2 * KV_COLS:].astype(jnp.float32)).reshape(B, S, H, N_BRANCH)

    n_cmp = (S - CMP_LEN) // CMP_STRIDE + 1
    blk_idx = jnp.arange(n_cmp)[:, None] * CMP_STRIDE + jnp.arange(CMP_LEN)[None, :]

    def compress(t, pos_emb, w1, w2):
        blocks = t[:, blk_idx] + pos_emb[:, None, :]
        flat = blocks.transpose(0, 1, 3, 2, 4).reshape(B, n_cmp, G, CMP_LEN * E)
        return jax.nn.gelu(flat @ w1) @ w2

    cmp_end = blk_idx[:, -1]
    kc = partial_rope(compress(kv[:, :, 0, 0], pos_k, w1_k, w2_k), positions[:, cmp_end])
    vc = compress(kv[:, :, 0, 1], pos_v, w1_v, w2_v)

    n_sel = S // SEL_LEN
    k_top = min(N_SELECT, n_sel)
    ksb = partial_rope(kv[:, :, 1, 0], positions).reshape(B, n_sel, SEL_LEN, G, E).transpose(0, 3, 1, 2, 4)
    vsb = kv[:, :, 1, 1].reshape(B, n_sel, SEL_LEN, G, E).transpose(0, 3, 1, 2, 4)
    cs = jnp.arange(n_cmp) * CMP_STRIDE
    ss = jnp.arange(n_sel) * SEL_LEN
    overlap = ((cs[:, None] < ss[None, :] + SEL_LEN) & (cs[:, None] + CMP_LEN > ss[None, :])).astype(jnp.float32)

    pad = ((0, 0), (WIN_LEN, 0), (0, 0), (0, 0))
    kwp = jnp.pad(partial_rope(kv[:, :, 2, 0], positions), pad)
    vwp = jnp.pad(kv[:, :, 2, 1], pad)

    nqb = S // Q_BLOCK
    q_blocks = q.reshape(B, nqb, Q_BLOCK, G, R, E).transpose(1, 0, 2, 3, 4, 5)
    g_blocks = gate.reshape(B, nqb, Q_BLOCK, H, N_BRANCH).transpose(1, 0, 2, 3, 4)
    bi = jnp.arange(B)[:, None, None, None]
    gi = jnp.arange(G)[None, :, None, None]
    cmp_last = cs + CMP_LEN - 1
    sel_ids = jnp.arange(n_sel)

    def block_fn(args):
        qb_idx, qb, gb = args
        t = qb_idx * Q_BLOCK + jnp.arange(Q_BLOCK)
        s_c = jnp.einsum('bqgre,bnge->bgrqn', qb, kc) * ATTN_SCALE
        p_c = masked_softmax(s_c, cmp_last[None, :] <= t[:, None])
        o_c = jnp.einsum('bgrqn,bnge->bqgre', p_c.astype(vc.dtype), vc)
        imp = jnp.einsum('bgrqn,ns->bgqs', p_c, overlap)
        cur = t // SEL_LEN
        forced = (sel_ids[None, :] == 0) | (sel_ids[None, :] == cur[:, None]) | (sel_ids[None, :] == cur[:, None] - 1)
        imp = jnp.where(forced, FORCE_SCORE, imp)
        imp = jnp.where(sel_ids[None, :] <= cur[:, None], imp, -jnp.inf)
        _, idx = lax.top_k(imp, k_top)
        ks = ksb[bi, gi, idx]
        vs = vsb[bi, gi, idx]
        s_s = jnp.einsum('bqgre,bgqjle->bgrqjl', qb, ks) * ATTN_SCALE
        tok = idx[..., None] * SEL_LEN + jnp.arange(SEL_LEN)
        valid_s = (tok <= t[None, None, :, None, None]).reshape(B, G, 1, Q_BLOCK, k_top * SEL_LEN)
        p_s = masked_softmax(s_s.reshape(B, G, R, Q_BLOCK, k_top * SEL_LEN), valid_s)
        p_s = p_s.reshape(B, G, R, Q_BLOCK, k_top, SEL_LEN)
        o_s = jnp.einsum('bgrqjl,bgqjle->bqgre', p_s.astype(vs.dtype), vs)
        kw = lax.dynamic_slice_in_dim(kwp, qb_idx * Q_BLOCK, WIN_LEN + Q_BLOCK, axis=1)
        vw = lax.dynamic_slice_in_dim(vwp, qb_idx * Q_BLOCK, WIN_LEN + Q_BLOCK, axis=1)
        kpos = qb_idx * Q_BLOCK - WIN_LEN + jnp.arange(WIN_LEN + Q_BLOCK)
        dist = t[:, None] - kpos[None, :]
        valid_w = (dist >= 0) & (dist < WIN_LEN) & (kpos[None, :] >= 0)
        s_w = jnp.einsum('bqgre,bkge->bgrqk', qb, kw) * ATTN_SCALE
        p_w = masked_softmax(s_w, valid_w)
        o_w = jnp.einsum('bgrqk,bkge->bqgre', p_w.astype(vw.dtype), vw)
        o = (o_c.reshape(B, Q_BLOCK, H, E) * gb[..., 0:1]
             + o_s.reshape(B, Q_BLOCK, H, E) * gb[..., 1:2]
             + o_w.reshape(B, Q_BLOCK, H, E) * gb[..., 2:3])
        return o.reshape(B, Q_BLOCK, D_MODEL).astype(h.dtype)

    outs = lax.map(block_fn, (jnp.arange(nqb), q_blocks, g_blocks))
    return outs.transpose(1, 0, 2, 3).reshape(B, S, D_MODEL) @ w_o


def dilated_band_attention(q, k, v, dil, steps):
    B, S, G, R, E = q.shape
    L = S // dil
    nb = -(-L // DIL_BLOCK)
    pad = nb * DIL_BLOCK - L
    qd = q.reshape(B, L, dil, G, R, E).transpose(0, 2, 3, 4, 1, 5)
    kd = k.reshape(B, L, dil, G, E).transpose(0, 2, 3, 1, 4)
    vd = v.reshape(B, L, dil, G, E).transpose(0, 2, 3, 1, 4)
    qd = jnp.pad(qd, ((0, 0),) * 4 + ((0, pad), (0, 0)))
    kd = jnp.pad(kd, ((0, 0),) * 3 + ((0, pad), (0, 0)))
    vd = jnp.pad(vd, ((0, 0),) * 3 + ((0, pad), (0, 0)))
    qb = qd.reshape(B, dil, G, R, nb, DIL_BLOCK, E)
    kb = kd.reshape(B, dil, G, nb, DIL_BLOCK, E)
    vb = vd.reshape(B, dil, G, nb, DIL_BLOCK, E)
    prev = ((0, 0), (0, 0), (0, 0), (1, 0), (0, 0), (0, 0))
    kk = jnp.concatenate([jnp.pad(kb, prev)[:, :, :, :-1], kb], axis=-2)
    vv = jnp.concatenate([jnp.pad(vb, prev)[:, :, :, :-1], vb], axis=-2)
    qi = jnp.arange(DIL_BLOCK)
    kj = jnp.arange(2 * DIL_BLOCK)
    dist = DIL_BLOCK + qi[:, None] - kj[None, :]
    key_sub = (jnp.arange(nb)[:, None, None] - 1) * DIL_BLOCK + kj[None, None, :]
    valid = (dist >= 0) & (dist <= steps) & (key_sub >= 0)
    s = jnp.einsum('bdgrnqe,bdgnke->bdgrnqk', qb, kk).astype(jnp.float32) * ATTN_SCALE
    s = jnp.where(valid, s, -jnp.inf)
    m = jnp.max(s, axis=-1)
    e = jnp.where(valid, jnp.exp(s - m[..., None]), 0.0)
    l = e.sum(-1)
    o = jnp.einsum('bdgrnqk,bdgnke->bdgrnqe', e, vv.astype(jnp.float32)) / l[..., None]
    o = o.reshape(B, dil, G, R, nb * DIL_BLOCK, E)[:, :, :, :, :L]
    o = o.transpose(0, 4, 1, 2, 3, 5).reshape(B, S, G, R, E)
    m = m.reshape(B, dil, G, R, nb * DIL_BLOCK)[..., :L].transpose(0, 4, 1, 2, 3).reshape(B, S, G, R)
    l = l.reshape(B, dil, G, R, nb * DIL_BLOCK)[..., :L].transpose(0, 4, 1, 2, 3).reshape(B, S, G, R)
    return o, m, l


def dilated_mixer(h, positions, w_in, w_o):
    B, S, _ = h.shape
    G, R, E = N_KV_HEADS, GQA_REP, HEAD_DIM
    proj = h @ w_in
    q = partial_rope(proj[..., :D_MODEL].reshape(B, S, N_Q_HEADS, E), positions).reshape(B, S, G, R, E)
    kv = proj[..., D_MODEL:].reshape(B, S, len(DIL_PATTERNS), 2, G, E)
    outs, maxes, dens = [], [], []
    for p, (window, dil) in enumerate(DIL_PATTERNS):
        o, m, l = dilated_band_attention(q, partial_rope(kv[:, :, p, 0], positions), kv[:, :, p, 1], dil, window // dil)
        outs.append(o)
        maxes.append(m)
        dens.append(l)
    m_all = jnp.stack(maxes, 0)
    w = jnp.stack(dens, 0) * jnp.exp(m_all - m_all.max(0))
    w = w / w.sum(0)
    o = jnp.einsum('pbsgr,pbsgre->bsgre', w, jnp.stack(outs, 0))
    return o.reshape(B, S, D_MODEL).astype(h.dtype) @ w_o


def moe_ffn(h, router_w, router_b, w_gate, w_up, w_down):
    B, S, D = h.shape
    N = B * S
    xt = h.reshape(N, D)
    scores = jax.nn.sigmoid((xt @ router_w).astype(jnp.float32))
    grp = (scores + router_b.astype(jnp.float32)).reshape(N, N_GROUPS, EXPERTS_PER_GROUP)
    best = jnp.argmax(lax.top_k(grp, TOP_K)[0].sum(-1), axis=-1)
    in_grp = jnp.arange(N_GROUPS)[None, :] == best[:, None]
    masked = jnp.where(in_grp[:, :, None], grp, -jnp.inf).reshape(N, N_EXPERTS)
    _, eidx = lax.top_k(masked, TOP_K)
    wt = jnp.take_along_axis(scores, eidx, -1)
    wt = wt / wt.sum(-1, keepdims=True)
    A = N * TOP_K
    e_flat = eidx.reshape(A)
    tok_flat = jnp.repeat(jnp.arange(N, dtype=jnp.int32), TOP_K)
    w_flat = wt.reshape(A)
    order = jnp.argsort(e_flat)
    e_sorted = e_flat[order]
    counts = jnp.bincount(e_flat, length=N_EXPERTS)
    starts = jnp.cumsum(counts) - counts
    padded = (counts + MOE_BLOCK - 1) // MOE_BLOCK * MOE_BLOCK
    pad_ends = jnp.cumsum(padded)
    pad_starts = pad_ends - padded
    dest = pad_starts[e_sorted] + jnp.arange(A) - starts[e_sorted]
    n_rows = A + N_EXPERTS * MOE_BLOCK
    n_blk = n_rows // MOE_BLOCK
    row_tok = jnp.zeros((n_rows,), jnp.int32).at[dest].set(tok_flat[order])
    row_w = jnp.zeros((n_rows,), jnp.float32).at[dest].set(w_flat[order])
    blk_exp = jnp.minimum(jnp.searchsorted(pad_ends, jnp.arange(n_blk) * MOE_BLOCK, side='right'), N_EXPERTS - 1)
    xs = xt[row_tok].reshape(n_blk, MOE_BLOCK, D)

    def expert_block(args):
        xb, e = args
        return (jax.nn.silu(xb @ w_gate[e]) * (xb @ w_up[e])) @ w_down[e]

    ys = lax.map(expert_block, (xs, blk_exp)).reshape(n_rows, D)
    out = jnp.zeros((N, D), h.dtype).at[row_tok].add(ys * row_w[:, None].astype(ys.dtype))
    return out.reshape(B, S, D)


def setup_inputs(seed: int = 0) -> dict:
    key = jax.random.key(seed)
    ks = jax.random.split(key, 24)
    nrm = jax.random.normal
    D, F = D_MODEL, D_EXPERT
    cmp_in = CMP_LEN * HEAD_DIM
    return {
        'x': nrm(ks[0], (BATCH, SEQ, D), jnp.float32),
        'c': nrm(ks[1], (BATCH, D), jnp.float32),
        'positions': (jnp.arange(SEQ, dtype=jnp.int32)[None, :]
                      + jax.random.randint(ks[2], (BATCH, 1), 0, 4096, dtype=jnp.int32)),
        'ada_w': nrm(ks[3], (DEPTH, 2, D, 3 * D), jnp.float32) * (0.5 * D ** -0.5),
        'ada_b': nrm(ks[4], (DEPTH, 2, 3 * D), jnp.float32) * 0.02,
        'ln_g': 1.0 + 0.02 * nrm(ks[5], (DEPTH, 2, D), jnp.float32),
        'ln_b': 0.02 * nrm(ks[6], (DEPTH, 2, D), jnp.float32),
        'nsa_w_in': nrm(ks[7], (N_A, D, NSA_COLS), jnp.float32) * D ** -0.5,
        'nsa_cmp_pos_k': 0.02 * nrm(ks[8], (N_A, CMP_LEN, HEAD_DIM), jnp.float32),
        'nsa_cmp_w1_k': nrm(ks[9], (N_A, cmp_in, CMP_HIDDEN), jnp.float32) * cmp_in ** -0.5,
        'nsa_cmp_w2_k': nrm(ks[10], (N_A, CMP_HIDDEN, HEAD_DIM), jnp.float32) * CMP_HIDDEN ** -0.5,
        'nsa_cmp_pos_v': 0.02 * nrm(ks[11], (N_A, CMP_LEN, HEAD_DIM), jnp.float32),
        'nsa_cmp_w1_v': nrm(ks[12], (N_A, cmp_in, CMP_HIDDEN), jnp.float32) * cmp_in ** -0.5,
        'nsa_cmp_w2_v': nrm(ks[13], (N_A, CMP_HIDDEN, HEAD_DIM), jnp.float32) * CMP_HIDDEN ** -0.5,
        'nsa_w_o': nrm(ks[14], (N_A, D, D), jnp.float32) * (D ** -0.5 * DN_BETA),
        'dil_w_in': nrm(ks[15], (N_B, D, DIL_COLS), jnp.float32) * D ** -0.5,
        'dil_w_o': nrm(ks[16], (N_B, D, D), jnp.float32) * (D ** -0.5 * DN_BETA),
        'router_w': nrm(ks[17], (D, N_EXPERTS), jnp.float32) * D ** -0.5,
        'router_b': 0.01 * nrm(ks[18], (N_EXPERTS,), jnp.float32),
        'moe_w_gate': nrm(ks[19], (DEPTH, N_EXPERTS, D, F), jnp.float32) * D ** -0.5,
        'moe_w_up': nrm(ks[20], (DEPTH, N_EXPERTS, D, F), jnp.float32) * D ** -0.5,
        'moe_w_down': nrm(ks[21], (DEPTH, N_EXPERTS, F, D), jnp.float32) * (F ** -0.5 * DN_BETA),
    }


def reference(x, c, positions, ada_w, ada_b, ln_g, ln_b,
              nsa_w_in, nsa_cmp_pos_k, nsa_cmp_w1_k, nsa_cmp_w2_k,
              nsa_cmp_pos_v, nsa_cmp_w1_v, nsa_cmp_w2_v, nsa_w_o,
              dil_w_in, dil_w_o,
              router_w, router_b, moe_w_gate, moe_w_up, moe_w_down):
    cond = jax.nn.silu(c)
    for i in range(DEPTH):
        mod = (cond @ ada_w[i, 0] + ada_b[i, 0])[:, None, :]
        shift, scale, gate = jnp.split(mod, 3, axis=-1)
        hmod = x * (1.0 + scale) + shift
        if i % 2 == 0:
            j = i // 2
            y = nsa_mixer(hmod, positions, nsa_w_in[j], nsa_cmp_pos_k[j], nsa_cmp_w1_k[j], nsa_cmp_w2_k[j],
                          nsa_cmp_pos_v[j], nsa_cmp_w1_v[j], nsa_cmp_w2_v[j], nsa_w_o[j])
        else:
            j = i // 2
            y = dilated_mixer(hmod, positions, dil_w_in[j], dil_w_o[j])
        x = layer_norm(DN_ALPHA * x + gate * y, ln_g[i, 0], ln_b[i, 0])
        mod = (cond @ ada_w[i, 1] + ada_b[i, 1])[:, None, :]
        shift, scale, gate = jnp.split(mod, 3, axis=-1)
        hmod = x * (1.0 + scale) + shift
        y = moe_ffn(hmod, router_w, router_b, moe_w_gate[i], moe_w_up[i], moe_w_down[i])
        x = layer_norm(DN_ALPHA * x + gate * y, ln_g[i, 1], ln_b[i, 1])
    return x
```

```python
import functools

import numpy as np
import jax
import jax.numpy as jnp
from jax import lax
from jax.experimental import pallas as pl
from jax.experimental.pallas import tpu as pltpu

F32 = jnp.float32
BF16 = jnp.bfloat16
HIGHEST = lax.Precision.HIGHEST
NEG_INF = float("-inf")

D_MODEL = 1024
DEPTH = 2
HEAD_DIM = 64
N_Q_HEADS = D_MODEL // HEAD_DIM
N_KV_HEADS = 4
GQA_REP = N_Q_HEADS // N_KV_HEADS
ROPE_DIM = HEAD_DIM // 4
ROPE_THETA = 500000.0
ATTN_SCALE = HEAD_DIM ** -0.5
KV_COLS = N_KV_HEADS * HEAD_DIM
N_BRANCH = 3
CMP_LEN = 32
CMP_STRIDE = 16
CMP_HIDDEN = 256
SEL_LEN = 64
N_SELECT = 16
WIN_LEN = 512
FORCE_SCORE = 1.0e4
DIL_PATTERNS = ((128, 1), (512, 4), (2048, 16))
DIL_BLOCK = 128
N_EXPERTS = 32
N_GROUPS = 4
EXPERTS_PER_GROUP = N_EXPERTS // N_GROUPS
TOP_K = 2
D_EXPERT = 512
MOE_BLOCK = 128
DN_ALPHA = (2.0 * DEPTH) ** 0.25
LN_EPS = 1e-5

LANES = 128
VMEM_LIMIT_BYTES = 48 * 1024 * 1024

Q_TILE = 128
KEY_TILE = 128
ROW_TILE = 512


def _cparams(semantics):
    return pltpu.CompilerParams(dimension_semantics=semantics, vmem_limit_bytes=VMEM_LIMIT_BYTES)


def _dot(a, b):
    return jnp.dot(a, b, preferred_element_type=F32)


def _dot_nt(a, b):
    return lax.dot_general(a, b, (((1,), (1,)), ((), ())), preferred_element_type=F32)


def _ada_kernel(c_ref, w_ref, b_ref, o_ref):
    c = c_ref[...]
    cond = c * jax.nn.sigmoid(c)
    o_ref[0] = jnp.dot(cond, w_ref[0], preferred_element_type=F32, precision=HIGHEST) + b_ref[0]


def _ada_mods(c, ada_w, ada_b):
    B, D = c.shape
    n_sub = ada_w.shape[0] * ada_w.shape[1]
    w = ada_w.reshape(n_sub, D, 3 * D)
    b = ada_b.reshape(n_sub, 1, 3 * D)
    c8 = jnp.zeros((8, D), F32).at[:B].set(c)
    tn = 768
    out = pl.pallas_call(
        _ada_kernel,
        out_shape=jax.ShapeDtypeStruct((n_sub, 8, 3 * D), F32),
        grid=(n_sub, 3 * D // tn),
        in_specs=[
            pl.BlockSpec((8, D), lambda s, j: (0, 0)),
            pl.BlockSpec((1, D, tn), lambda s, j: (s, 0, j)),
            pl.BlockSpec((1, 1, tn), lambda s, j: (s, 0, j)),
        ],
        out_specs=pl.BlockSpec((1, 8, tn), lambda s, j: (s, 0, j)),
        compiler_params=_cparams(("parallel", "parallel")),
        name="ada_mods",
    )(c8, w, b)
    return out[:, :B]


def _rope_tab_kernel(pos_ref, inv_ref, sg1_ref, sg2_ref, c_ref, s1_ref, s2_ref):
    ang = pos_ref[0] * inv_ref[...]
    sin = jnp.sin(ang)
    c_ref[0] = jnp.cos(ang)
    s1_ref[0] = sin * sg1_ref[...]
    s2_ref[0] = sin * sg2_ref[...]


def _rope_tables(positions):
    B, S = positions.shape
    half = ROPE_DIM // 2
    inv = ROPE_THETA ** (-jnp.arange(half, dtype=F32) * (2.0 / ROPE_DIM))
    li = np.arange(LANES) % HEAD_DIM
    in_rope = li < ROPE_DIM
    inv_row = jnp.where(jnp.asarray(in_rope), inv[li % half], 0.0).reshape(1, LANES)
    sg1 = jnp.asarray(np.where(li < half, -1.0, 0.0), F32).reshape(1, LANES)
    sg2 = jnp.asarray(np.where((li >= half) & in_rope, 1.0, 0.0), F32).reshape(1, LANES)
    pos = positions.astype(F32).reshape(B, S, 1)
    tm = min(S, 2048)
    row = pl.BlockSpec((1, LANES), lambda b, i: (0, 0))
    tab = pl.BlockSpec((1, tm, LANES), lambda b, i: (b, i, 0))
    return pl.pallas_call(
        _rope_tab_kernel,
        out_shape=[jax.ShapeDtypeStruct((B, S, LANES), F32)] * 3,
        grid=(B, S // tm),
        in_specs=[pl.BlockSpec((1, tm, 1), lambda b, i: (b, i, 0)), row, row, row],
        out_specs=[tab, tab, tab],
        compiler_params=_cparams(("parallel", "parallel")),
        name="rope_tables",
    )(pos, inv_row, sg1, sg2)


def _rope128(t, c, s1, s2):
    return t * c + pltpu.roll(t, LANES - ROPE_DIM // 2, 1) * s1 + pltpu.roll(t, ROPE_DIM // 2, 1) * s2


def _rope_cols(a, c, s1, s2):
    n = a.shape[1] // LANES
    return jnp.concatenate(
        [_rope128(a[:, k * LANES:(k + 1) * LANES], c, s1, s2) for k in range(n)], axis=1)


def _nsa_weight_cols():
    d = D_MODEL
    def kv(branch, which):
        base = d + (branch * 2 + which) * KV_COLS
        return list(range(base, base + KV_COLS))
    cols = list(range(d))
    cols += kv(1, 0) + kv(2, 0) + kv(1, 1) + kv(2, 1) + kv(0, 0) + kv(0, 1)
    gate0 = d + N_BRANCH * 2 * KV_COLS
    gcols = [-1] * LANES
    for g in range(N_KV_HEADS):
        for br in range(N_BRANCH):
            for r in range(GQA_REP):
                gcols[g * 16 + br * GQA_REP + r] = gate0 + (g * GQA_REP + r) * N_BRANCH + br
    return np.asarray(cols + gcols)


def _permute_cols(w, cols):
    picked = w[:, np.maximum(cols, 0)]
    return jnp.where(jnp.asarray(cols >= 0)[None, :], picked, 0.0).astype(BF16)


def _nsa_inproj_kernel(x_ref, sh_ref, sc_ref, w_ref, c_ref, s1_ref, s2_ref,
                       qT_ref, ksel_ref, kwin_ref, vTsel_ref, vTwin_ref, kcmp_ref, vcmp_ref, gT_ref):
    tm = x_ref.shape[1]
    h = (x_ref[0] * (1.0 + sc_ref[0]) + sh_ref[0]).astype(BF16)
    c, s1, s2 = c_ref[0], s1_ref[0], s2_ref[0]
    w = KV_COLS

    def proj(j, n=w):
        return _dot(h, w_ref[:, j * w:j * w + n])

    for j in range(4):
        a = _rope_cols(proj(j), c, s1, s2) * ATTN_SCALE
        qT_ref[0, j * w:(j + 1) * w, :] = a.T.astype(BF16)
    for j, ref in ((4, ksel_ref), (5, kwin_ref)):
        a = _rope_cols(proj(j), c, s1, s2)
        for g in range(N_KV_HEADS):
            ref[0, g] = a[:, g * HEAD_DIM:(g + 1) * HEAD_DIM].astype(BF16)
    for j, ref in ((6, vTsel_ref), (7, vTwin_ref)):
        aT = proj(j).T.astype(BF16)
        for k in range(tm // KEY_TILE):
            ref[0, k] = aT[:, k * KEY_TILE:(k + 1) * KEY_TILE]
    for j, ref in ((8, kcmp_ref), (9, vcmp_ref)):
        a = proj(j)
        for g in range(N_KV_HEADS):
            ref[0, g] = a[:, g * HEAD_DIM:(g + 1) * HEAD_DIM].astype(BF16)
    gates = jax.nn.sigmoid(proj(10, LANES))
    gT_ref[0] = gates.T[:4 * 16]


def _nsa_inproj(x, shift, scale, w_in, tabs):
    B, S, D = x.shape
    tm = ROW_TILE
    wp = _permute_cols(w_in, _nsa_weight_cols())
    ncol = wp.shape[1]
    vec = pl.BlockSpec((1, 1, D), lambda b, i: (b, 0, 0))
    tab = pl.BlockSpec((1, tm, LANES), lambda b, i: (b, i, 0))
    nat = pl.BlockSpec((1, N_KV_HEADS, tm, HEAD_DIM), lambda b, i: (b, 0, i, 0))
    vt = pl.BlockSpec((1, tm // KEY_TILE, KV_COLS, KEY_TILE), lambda b, i: (b, i, 0, 0))
    nat_shape = jax.ShapeDtypeStruct((B, N_KV_HEADS, S, HEAD_DIM), BF16)
    vt_shape = jax.ShapeDtypeStruct((B, S // KEY_TILE, KV_COLS, KEY_TILE), BF16)
    return pl.pallas_call(
        _nsa_inproj_kernel,
        out_shape=[
            jax.ShapeDtypeStruct((B, D, S), BF16),
            nat_shape, nat_shape,
            vt_shape, vt_shape,
            nat_shape, nat_shape,
            jax.ShapeDtypeStruct((B, 4 * 16, S), F32),
        ],
        grid=(B, S // tm),
        in_specs=[
            pl.BlockSpec((1, tm, D), lambda b, i: (b, i, 0)), vec, vec,
            pl.BlockSpec((D, ncol), lambda b, i: (0, 0)), tab, tab, tab,
        ],
        out_specs=[
            pl.BlockSpec((1, D, tm), lambda b, i: (b, 0, i)),
            nat, nat, vt, vt, nat, nat,
            pl.BlockSpec((1, 4 * 16, tm), lambda b, i: (b, 0, i)),
        ],
        compiler_params=_cparams(("parallel", "parallel")),
        name="nsa_inproj",
    )(x, shift, scale, wp, *tabs)


def _compress_kernel(xk_ref, xv_ref, w1k_ref, w1v_ref, pk_ref, pv_ref, w2k_ref, w2vT_ref,
                     c_ref, s1_ref, s2_ref, kc_ref, vcT_ref):
    n = xk_ref.shape[2]
    half = w1k_ref.shape[0] // 2

    def hidden(x_ref, w1_ref, p_ref):
        x = x_ref[0, 0]
        first = _dot(x, w1_ref[:half])
        second = _dot(x, w1_ref[half:])
        bias = _dot(p_ref[...], w1_ref[...])[0:1]
        hid = first + pltpu.roll(second, n - 1, 0) + bias
        return jax.nn.gelu(hid).astype(BF16)

    kc = _dot(hidden(xk_ref, w1k_ref, pk_ref), w2k_ref[...])
    kc = _rope128(kc, c_ref[0], s1_ref[0], s2_ref[0])
    row = lax.broadcasted_iota(jnp.int32, kc.shape, 0)
    kc = jnp.where(row < n - 1, kc, 0.0)
    kc_ref[0, 0] = kc[:, :HEAD_DIM].astype(BF16)

    vcT = _dot_nt(w2vT_ref[...], hidden(xv_ref, w1v_ref, pv_ref))
    col = lax.broadcasted_iota(jnp.int32, vcT.shape, 1)
    vcT = jnp.where(col < n - 1, vcT, 0.0).astype(BF16)
    for k in range(n // KEY_TILE):
        vcT_ref[0, 0, k] = vcT[:, k * KEY_TILE:(k + 1) * KEY_TILE]


def _compress(kcmp, vcmp, pos_k, w1_k, w2_k, pos_v, w1_v, w2_v, tabs):
    B, G, S, E = kcmp.shape
    n = S // CMP_STRIDE
    wide = CMP_STRIDE * E
    xk = kcmp.reshape(B, G, n, wide)
    xv = vcmp.reshape(B, G, n, wide)
    def flat8(p):
        return jnp.zeros((8, CMP_LEN * E), BF16).at[0].set(p.reshape(-1).astype(BF16))
    w2k = jnp.zeros((CMP_HIDDEN, LANES), BF16).at[:, :E].set(w2_k.astype(BF16))
    w2vT = w2_v.T.astype(BF16)
    last = CMP_LEN - 1
    ctabs = [jnp.zeros((B, n, LANES), F32).at[:, :n - 1].set(t[:, last::CMP_STRIDE][:, :n - 1]) for t in tabs]
    xspec = pl.BlockSpec((1, 1, n, wide), lambda b, g: (b, g, 0, 0))
    w1spec = pl.BlockSpec((CMP_LEN * E, CMP_HIDDEN), lambda b, g: (0, 0))
    pspec = pl.BlockSpec((8, CMP_LEN * E), lambda b, g: (0, 0))
    tspec = pl.BlockSpec((1, n, LANES), lambda b, g: (b, 0, 0))
    return pl.pallas_call(
        _compress_kernel,
        out_shape=[
            jax.ShapeDtypeStruct((B, G, n, E), BF16),
            jax.ShapeDtypeStruct((B, G, n // KEY_TILE, E, KEY_TILE), BF16),
        ],
        grid=(B, G),
        in_specs=[xspec, xspec, w1spec, w1spec, pspec, pspec,
                  pl.BlockSpec((CMP_HIDDEN, LANES), lambda b, g: (0, 0)),
                  pl.BlockSpec((E, CMP_HIDDEN), lambda b, g: (0, 0)),
                  tspec, tspec, tspec],
        out_specs=[
            pl.BlockSpec((1, 1, n, E), lambda b, g: (b, g, 0, 0)),
            pl.BlockSpec((1, 1, n // KEY_TILE, E, KEY_TILE), lambda b, g: (b, g, 0, 0, 0)),
        ],
        compiler_params=_cparams(("parallel", "parallel")),
        name="nsa_compress",
    )(xk, xv, w1_k.astype(BF16), w1_v.astype(BF16), flat8(pos_k), flat8(pos_v), w2k, w2vT, *ctabs)


def _flash_update(s, vT, m, l, acc):
    m_new = jnp.maximum(m, jnp.max(s, axis=0, keepdims=True))
    m_safe = jnp.where(m_new == NEG_INF, 0.0, m_new)
    p = jnp.exp(s - m_safe)
    alpha = jnp.exp(m - m_safe)
    l_new = alpha * l + jnp.sum(p, axis=0, keepdims=True)
    acc_new = alpha * acc + _dot(vT, p.astype(BF16))
    return m_new, l_new, acc_new


def _nsa_attn_kernel(qT_ref, kc_ref, vcT_ref, ov_ref, ksel_ref, vTsel_ref, kwin_ref, vTwin_ref, gT_ref,
                     o_ref, s_buf, imp_buf, sel_buf):
    i = pl.program_id(2)
    tq = Q_TILE
    m_lanes = GQA_REP * tq
    e = HEAD_DIM
    t0 = i * tq
    n_chunks = kc_ref.shape[2]
    n_sel = sel_buf.shape[0]

    qT = qT_ref[0]
    qTm = jnp.concatenate([qT[r * e:(r + 1) * e] for r in range(GQA_REP)], axis=1)
    lane = lax.broadcasted_iota(jnp.int32, (1, m_lanes), 1)
    tok = t0 + (lane & (tq - 1))
    row_k = lax.broadcasted_iota(jnp.int32, (KEY_TILE, 1), 0)

    def new_state():
        return (jnp.full((1, m_lanes), NEG_INF, F32), jnp.zeros((1, m_lanes), F32),
                jnp.zeros((e, m_lanes), F32))

    def normalise(l, acc):
        return acc / jnp.maximum(l, 1e-30)

    n_vis = jnp.minimum(i // (KEY_TILE // 8) + 1, n_chunks)

    def cmp_scores(c, m):
        s = _dot(kc_ref[0, 0, c], qTm)
        last_tok = (c * KEY_TILE + row_k) * CMP_STRIDE + (CMP_LEN - 1)
        s = jnp.where(last_tok <= tok, s, NEG_INF)
        s_buf[c] = s
        return jnp.maximum(m, jnp.max(s, axis=0, keepdims=True))

    m_c = lax.fori_loop(0, n_vis, cmp_scores, jnp.full((1, m_lanes), NEG_INF, F32))
    m_c = jnp.where(m_c == NEG_INF, 0.0, m_c)

    imp_buf[...] = jnp.zeros(imp_buf.shape, F32)

    def cmp_accum(c, carry):
        l, acc = carry
        p = jnp.exp(s_buf[c] - m_c)
        pb = p.astype(BF16)
        imp_buf[...] += _dot(ov_ref[c], pb)
        return l + jnp.sum(p, axis=0, keepdims=True), acc + _dot(vcT_ref[0, 0, c], pb)

    l_c, acc_c = lax.fori_loop(
        0, n_vis, cmp_accum, (jnp.zeros((1, m_lanes), F32), jnp.zeros((e, m_lanes), F32)))
    inv_l = 1.0 / jnp.maximum(l_c, 1e-30)
    o_cmp = acc_c * inv_l
    imp_n = imp_buf[...] * inv_l
    imp = imp_n[:, 0:tq]
    for r in range(1, GQA_REP):
        imp = imp + imp_n[:, r * tq:(r + 1) * tq]

    sidx = lax.broadcasted_iota(jnp.int32, (n_sel, tq), 0)
    cur = (t0 + lax.broadcasted_iota(jnp.int32, (1, tq), 1)) // SEL_LEN
    forced = (sidx == 0) | (sidx == cur) | (sidx == cur - 1)
    vals = jnp.where(forced, FORCE_SCORE, imp)
    vals = jnp.where(sidx <= cur, vals, NEG_INF)

    def pick(_, avail):
        mv = jnp.where(avail > 0.5, vals, NEG_INF)
        top = jnp.max(mv, axis=0, keepdims=True)
        cand = (avail > 0.5) & (mv == top)
        first = jnp.min(jnp.where(cand, sidx, n_sel), axis=0, keepdims=True)
        return jnp.where(sidx == first, 0.0, avail)

    avail = lax.fori_loop(0, min(N_SELECT, n_sel), pick, jnp.ones((n_sel, tq), F32))
    sel_buf[...] = 1.0 - avail

    half_tile = KEY_TILE // 2

    def sel_mask(j):
        rows = [jnp.broadcast_to(sel_buf[pl.ds(2 * j + h, 1), :], (half_tile, tq)) for h in range(2)]
        mk = jnp.concatenate(rows, axis=0)
        return jnp.concatenate([mk] * GQA_REP, axis=1) > 0.5

    def sel_step(j, state):
        s = _dot(ksel_ref[0, 0, j], qTm)
        s = jnp.where(sel_mask(j), s, NEG_INF)
        return _flash_update(s, vTsel_ref[0, j, :, :], *state)

    state = lax.fori_loop(0, i, sel_step, new_state())
    s = _dot(ksel_ref[0, 0, i], qTm)
    s = jnp.where(sel_mask(i) & (t0 + row_k <= tok), s, NEG_INF)
    m_s, l_s, acc_s = _flash_update(s, vTsel_ref[0, i, :, :], *state)
    o_sel = normalise(l_s, acc_s)

    state = new_state()
    n_back = WIN_LEN // KEY_TILE
    for d in range(n_back + 1):
        jt = i - n_back + d
        jc = jnp.maximum(jt, 0)
        kpos = jt * KEY_TILE + row_k
        dist = tok - kpos
        valid = (dist >= 0) & (dist < WIN_LEN) & (kpos >= 0)
        s = jnp.where(valid, _dot(kwin_ref[0, 0, jc], qTm), NEG_INF)
        state = _flash_update(s, vTwin_ref[0, jc, :, :], *state)
    o_win = normalise(state[1], state[2])

    def gate(branch):
        g = gT_ref[0]
        return jnp.concatenate([g[branch * GQA_REP + r:branch * GQA_REP + r + 1, :] for r in range(GQA_REP)], axis=1)

    oT = o_cmp * gate(0) + o_sel * gate(1) + o_win * gate(2)
    o_rows = jnp.concatenate([oT[:, r * tq:(r + 1) * tq] for r in range(GQA_REP)], axis=0)
    o_ref[0] = o_rows.T.astype(BF16)


def _overlap_tiles(n_sel, n_cmp_pad):
    cs = np.arange(n_cmp_pad)[None, :] * CMP_STRIDE
    ss = np.arange(n_sel)[:, None] * SEL_LEN
    ov = ((cs < ss + SEL_LEN) & (cs + CMP_LEN > ss)).astype(np.float32)
    ov = ov.reshape(n_sel, n_cmp_pad // KEY_TILE, KEY_TILE).transpose(1, 0, 2)
    return jnp.asarray(ov, BF16)


def _nsa_attention(qT, kc, vcT, ksel, vTsel, kwin, vTwin, gT):
    B, D, S = qT.shape
    G, E = N_KV_HEADS, HEAD_DIM
    n_sel = S // SEL_LEN
    n_tiles = S // KEY_TILE
    n_chunks = kc.shape[2] // KEY_TILE
    kc5 = kc.reshape(B, G, n_chunks, KEY_TILE, E)
    ksel5 = ksel.reshape(B, G, n_tiles, KEY_TILE, E)
    kwin5 = kwin.reshape(B, G, n_tiles, KEY_TILE, E)
    ov = _overlap_tiles(n_sel, n_chunks * KEY_TILE)
    kspec = pl.BlockSpec((1, 1, n_tiles, KEY_TILE, E), lambda b, g, i: (b, g, 0, 0, 0))
    vspec = pl.BlockSpec((1, n_tiles, E, KEY_TILE), lambda b, g, i: (b, 0, g, 0))
    m_lanes = GQA_REP * Q_TILE
    return pl.pallas_call(
        _nsa_attn_kernel,
        out_shape=jax.ShapeDtypeStruct((B, S, D), BF16),
        grid=(B, G, S // Q_TILE),
        in_specs=[
            pl.BlockSpec((1, GQA_REP * E, Q_TILE), lambda b, g, i: (b, g, i)),
            pl.BlockSpec((1, 1, n_chunks, KEY_TILE, E), lambda b, g, i: (b, g, 0, 0, 0)),
            pl.BlockSpec((1, 1, n_chunks, E, KEY_TILE), lambda b, g, i: (b, g, 0, 0, 0)),
            pl.BlockSpec((n_chunks, n_sel, KEY_TILE), lambda b, g, i: (0, 0, 0)),
            kspec, vspec, kspec, vspec,
            pl.BlockSpec((1, 16, Q_TILE), lambda b, g, i: (b, g, i)),
        ],
        out_specs=pl.BlockSpec((1, Q_TILE, GQA_REP * E), lambda b, g, i: (b, i, g)),
        scratch_shapes=[
            pltpu.VMEM((n_chunks, KEY_TILE, m_lanes), F32),
            pltpu.VMEM((n_sel, m_lanes), F32),
            pltpu.VMEM((n_sel, Q_TILE), F32),
        ],
        compiler_params=_cparams(("parallel", "parallel", "arbitrary")),
        name="nsa_attention",
    )(qT, kc5, vcT, ov, ksel5, vTsel, kwin5, vTwin, gT)


def _layer_norm(z, g, b):
    mu = jnp.mean(z, axis=-1, keepdims=True)
    d = z - mu
    var = jnp.mean(d * d, axis=-1, keepdims=True)
    return d * lax.rsqrt(var + LN_EPS) * g + b


def _proj_ln_kernel(o_ref, x_ref, w_ref, gate_ref, g_ref, b_ref, sh_ref, sc_ref, rw_ref,
                    x1_ref, h_ref, lgT_ref):
    y = _dot(o_ref[0], w_ref[...])
    xn = _layer_norm(DN_ALPHA * x_ref[0] + gate_ref[0] * y, g_ref[...], b_ref[...])
    x1_ref[0] = xn
    h = xn * (1.0 + sc_ref[0]) + sh_ref[0]
    h_ref[0] = h
    logits = jnp.dot(h, rw_ref[...], preferred_element_type=F32, precision=HIGHEST)
    lgT_ref[0] = logits.T[:N_EXPERTS]


def _proj_ln(o, x, w_o, gate, ln_g, ln_b, shift2, scale2, router_w):
    B, S, D = x.shape
    tm = ROW_TILE
    rw = jnp.zeros((D, LANES), F32).at[:, :N_EXPERTS].set(router_w)
    vec = pl.BlockSpec((1, 1, D), lambda b, i: (b, 0, 0))
    par = pl.BlockSpec((1, D), lambda b, i: (0, 0))
    row = pl.BlockSpec((1, tm, D), lambda b, i: (b, i, 0))
    return pl.pallas_call(
        _proj_ln_kernel,
        out_shape=[
            jax.ShapeDtypeStruct((B, S, D), F32),
            jax.ShapeDtypeStruct((B, S, D), F32),
            jax.ShapeDtypeStruct((B, N_EXPERTS, S), F32),
        ],
        grid=(B, S // tm),
        in_specs=[row, row, pl.BlockSpec((D, D), lambda b, i: (0, 0)), vec, par, par, vec, vec,
                  pl.BlockSpec((D, LANES), lambda b, i: (0, 0))],
        out_specs=[row, row, pl.BlockSpec((1, N_EXPERTS, tm), lambda b, i: (b, 0, i))],
        compiler_params=_cparams(("parallel", "parallel")),
        name="proj_ln",
    )(o, x, w_o.astype(BF16), gate, ln_g.reshape(1, D), ln_b.reshape(1, D), shift2, scale2, rw)


def _first_max(v, idx, big):
    top = jnp.max(v, axis=0, keepdims=True)
    first = jnp.min(jnp.where(v == top, idx, big), axis=0, keepdims=True)
    return top, first


def _route_kernel(lg_ref, rb_ref, e_ref, w_ref):
    scores = jax.nn.sigmoid(lg_ref[0])
    biased = scores + rb_ref[...]
    eidx = lax.broadcasted_iota(jnp.int32, scores.shape, 0)
    npg = EXPERTS_PER_GROUP
    best_v, best_g = None, None
    for g in range(N_GROUPS):
        v = biased[g * npg:(g + 1) * npg]
        ii = g * npg + lax.broadcasted_iota(jnp.int32, v.shape, 0)
        top1, i1 = _first_max(v, ii, N_EXPERTS)
        top2 = jnp.max(jnp.where(ii == i1, NEG_INF, v), axis=0, keepdims=True)
        gs = top1 + top2
        if g == 0:
            best_v, best_g = gs, jnp.zeros_like(i1)
        else:
            better = gs > best_v
            best_g = jnp.where(better, g, best_g)
            best_v = jnp.where(better, gs, best_v)
    masked = jnp.where(eidx // npg == best_g, biased, NEG_INF)
    _, e1 = _first_max(masked, eidx, N_EXPERTS)
    _, e2 = _first_max(jnp.where(eidx == e1, NEG_INF, masked), eidx, N_EXPERTS)
    sc1 = jnp.sum(jnp.where(eidx == e1, scores, 0.0), axis=0, keepdims=True)
    sc2 = jnp.sum(jnp.where(eidx == e2, scores, 0.0), axis=0, keepdims=True)
    tot = sc1 + sc2
    e_ref[0] = jnp.concatenate([e1, e2], axis=0)
    w_ref[0] = jnp.concatenate([sc1 / tot, sc2 / tot], axis=0)


def _route(lgT, router_b):
    B, E, S = lgT.shape
    tn = min(S, 2048)
    return pl.pallas_call(
        _route_kernel,
        out_shape=[jax.ShapeDtypeStruct((B, TOP_K, S), jnp.int32), jax.ShapeDtypeStruct((B, TOP_K, S), F32)],
        grid=(B, S // tn),
        in_specs=[pl.BlockSpec((1, E, tn), lambda b, i: (b, 0, i)), pl.BlockSpec((E, 1), lambda b, i: (0, 0))],
        out_specs=[pl.BlockSpec((1, TOP_K, tn), lambda b, i: (b, 0, i))] * 2,
        compiler_params=_cparams(("parallel", "parallel")),
        name="moe_route",
    )(lgT, router_b.reshape(E, 1))


def _dispatch_plan(eidx, wts):
    B, K, S = eidx.shape
    n_tok = B * S
    n_asg = n_tok * K
    e_flat = eidx.transpose(0, 2, 1).reshape(n_asg)
    w_flat = wts.transpose(0, 2, 1).reshape(n_asg)
    onehot = (e_flat[:, None] == jnp.arange(N_EXPERTS, dtype=jnp.int32)[None, :]).astype(jnp.int32)
    csum = jnp.cumsum(onehot, axis=0)
    rank = jnp.take_along_axis(csum, e_flat[:, None], axis=1)[:, 0] - 1
    counts = csum[-1]
    padded = (counts + MOE_BLOCK - 1) // MOE_BLOCK * MOE_BLOCK
    pad_ends = jnp.cumsum(padded)
    dest = (pad_ends - padded)[e_flat] + rank
    n_rows = n_asg + N_EXPERTS * MOE_BLOCK
    n_blk = n_rows // MOE_BLOCK
    asg = jnp.arange(n_asg, dtype=jnp.int32)
    row_asg = jnp.full((n_rows,), -1, jnp.int32).at[dest].set(asg)
    is_pad = row_asg < 0
    pad_rank = jnp.cumsum(is_pad.astype(jnp.int32)) - 1
    row_dst = jnp.where(is_pad, n_asg + pad_rank, row_asg)
    row_src = jnp.where(is_pad, 0, row_asg // K)
    row_w = jnp.zeros((n_rows,), F32).at[dest].set(w_flat)
    blk_exp = jnp.minimum(
        jnp.searchsorted(pad_ends, jnp.arange(n_blk, dtype=jnp.int32) * MOE_BLOCK, side="right"),
        N_EXPERTS - 1).astype(jnp.int32)
    return (blk_exp, row_src.reshape(n_blk, 1, MOE_BLOCK), row_dst.reshape(n_blk, 1, MOE_BLOCK),
            row_w.reshape(n_rows, 1))


def _moe_kernel(blk_exp_ref, src_ref, dst_ref, rw_ref, x_hbm, wg_ref, wu_ref, wd_ref, y_hbm,
                xbuf, ybuf, sems):
    del blk_exp_ref
    rows = xbuf.shape[0]

    def in_copy(r, src_row):
        return pltpu.make_async_copy(x_hbm.at[pl.ds(src_row, 1), :], xbuf.at[pl.ds(r, 1), :], sems.at[0])

    def out_copy(r, dst_row):
        return pltpu.make_async_copy(ybuf.at[pl.ds(r, 1), :], y_hbm.at[pl.ds(dst_row, 1), :], sems.at[1])

    def start_in(r, c):
        in_copy(r, src_ref[0, 0, r]).start()
        return c

    def wait_in(r, c):
        in_copy(r, 0).wait()
        return c

    lax.fori_loop(0, rows, start_in, 0)
    lax.fori_loop(0, rows, wait_in, 0)

    x = xbuf[...].astype(BF16)
    gate = _dot(x, wg_ref[0].astype(BF16))
    up = _dot(x, wu_ref[0].astype(BF16))
    hid = (gate * jax.nn.sigmoid(gate) * up).astype(BF16)
    ybuf[...] = _dot(hid, wd_ref[0].astype(BF16)) * rw_ref[...]

    def start_out(r, c):
        out_copy(r, dst_ref[0, 0, r]).start()
        return c

    def wait_out(r, c):
        out_copy(r, 0).wait()
        return c

    lax.fori_loop(0, rows, start_out, 0)
    lax.fori_loop(0, rows, wait_out, 0)


def _moe_experts(h2d, plan, w_gate, w_up, w_down):
    blk_exp, row_src, row_dst, row_w = plan
    n_blk = row_src.shape[0]
    n_rows = n_blk * MOE_BLOCK
    D, F = w_gate.shape[1], w_gate.shape[2]
    idx = pl.BlockSpec((1, 1, MOE_BLOCK), lambda i, be: (i, 0, 0), memory_space=pltpu.SMEM)
    grid_spec = pltpu.PrefetchScalarGridSpec(
        num_scalar_prefetch=1,
        grid=(n_blk,),
        in_specs=[
            idx, idx,
            pl.BlockSpec((MOE_BLOCK, 1), lambda i, be: (i, 0)),
            pl.BlockSpec(memory_space=pl.ANY),
            pl.BlockSpec((1, D, F), lambda i, be: (be[i], 0, 0)),
            pl.BlockSpec((1, D, F), lambda i, be: (be[i], 0, 0)),
            pl.BlockSpec((1, F, D), lambda i, be: (be[i], 0, 0)),
        ],
        out_specs=pl.BlockSpec(memory_space=pl.ANY),
        scratch_shapes=[
            pltpu.VMEM((MOE_BLOCK, D), F32),
            pltpu.VMEM((MOE_BLOCK, D), F32),
            pltpu.SemaphoreType.DMA((2,)),
        ],
    )
    return pl.pallas_call(
        _moe_kernel,
        out_shape=jax.ShapeDtypeStruct((n_rows, D), F32),
        grid_spec=grid_spec,
        compiler_params=_cparams(("arbitrary",)),
        name="moe_experts",
    )(blk_exp, row_src, row_dst, row_w, h2d, w_gate, w_up, w_down)


def _combine_ln_kernel(y_ref, x_ref, gate_ref, g_ref, b_ref, o_ref):
    d = x_ref.shape[2]
    y = y_ref[:, :d] + y_ref[:, d:]
    o_ref[0] = _layer_norm(DN_ALPHA * x_ref[0] + gate_ref[0] * y, g_ref[...], b_ref[...])


def _combine_ln(ybuf, x, gate, ln_g, ln_b):
    B, S, D = x.shape
    tm = ROW_TILE
    y2 = ybuf.reshape(ybuf.shape[0] // TOP_K, TOP_K * D)
    per_b = S // tm
    vec = pl.BlockSpec((1, 1, D), lambda b, i: (b, 0, 0))
    par = pl.BlockSpec((1, D), lambda b, i: (0, 0))
    row = pl.BlockSpec((1, tm, D), lambda b, i: (b, i, 0))
    return pl.pallas_call(
        _combine_ln_kernel,
        out_shape=jax.ShapeDtypeStruct((B, S, D), F32),
        grid=(B, per_b),
        in_specs=[pl.BlockSpec((tm, TOP_K * D), lambda b, i: (b * per_b + i, 0)), row, vec, par, par],
        out_specs=row,
        compiler_params=_cparams(("parallel", "parallel")),
        name="moe_combine_ln",
    )(y2, x, gate, ln_g.reshape(1, D), ln_b.reshape(1, D))


def _moe_sublayer(x1, h2, lgT, router_b, w_gate, w_up, w_down, gate, ln_g, ln_b):
    B, S, D = x1.shape
    eidx, wts = _route(lgT, router_b)
    plan = _dispatch_plan(eidx, wts)
    ybuf = _moe_experts(h2.reshape(B * S, D), plan, w_gate, w_up, w_down)
    return _combine_ln(ybuf, x1, gate, ln_g, ln_b)


def _dil_weight_cols():
    d = D_MODEL
    cols = list(range(d))
    for which in range(2):
        for p in range(len(DIL_PATTERNS)):
            base = d + (p * 2 + which) * KV_COLS
            for g in range(N_KV_HEADS):
                head = list(range(base + g * HEAD_DIM, base + (g + 1) * HEAD_DIM))
                cols += head + head
    return np.asarray(cols)


def _dil_inproj_kernel(x_ref, sh_ref, sc_ref, w_ref, c_ref, s1_ref, s2_ref, q_ref, *kv_refs):
    h = (x_ref[0] * (1.0 + sc_ref[0]) + sh_ref[0]).astype(BF16)
    c, s1, s2 = c_ref[0], s1_ref[0], s2_ref[0]
    d = q_ref.shape[2]
    w = 2 * KV_COLS
    n_pat = len(DIL_PATTERNS)
    for j in range(d // w):
        a = _rope_cols(_dot(h, w_ref[:, j * w:(j + 1) * w]), c, s1, s2) * ATTN_SCALE
        q_ref[0, :, j * w:(j + 1) * w] = a.astype(BF16)
    for p in range(n_pat):
        a = _rope_cols(_dot(h, w_ref[:, d + p * w:d + (p + 1) * w]), c, s1, s2)
        kv_refs[p][0] = a.astype(BF16)
    for p in range(n_pat):
        a = _dot(h, w_ref[:, d + (n_pat + p) * w:d + (n_pat + p + 1) * w])
        kv_refs[n_pat + p][0] = a.astype(BF16)


def _dil_inproj(x, shift, scale, w_in, tabs):
    B, S, D = x.shape
    tm = ROW_TILE
    wp = _permute_cols(w_in, _dil_weight_cols())
    ncol = wp.shape[1]
    n_pat = len(DIL_PATTERNS)
    vec = pl.BlockSpec((1, 1, D), lambda b, i: (b, 0, 0))
    tab = pl.BlockSpec((1, tm, LANES), lambda b, i: (b, i, 0))
    kvspec = pl.BlockSpec((1, tm, 2 * KV_COLS), lambda b, i: (b, i, 0))
    outs = pl.pallas_call(
        _dil_inproj_kernel,
        out_shape=[jax.ShapeDtypeStruct((B, S, D), BF16)]
        + [jax.ShapeDtypeStruct((B, S, 2 * KV_COLS), BF16)] * (2 * n_pat),
        grid=(B, S // tm),
        in_specs=[pl.BlockSpec((1, tm, D), lambda b, i: (b, i, 0)), vec, vec,
                  pl.BlockSpec((D, ncol), lambda b, i: (0, 0)), tab, tab, tab],
        out_specs=[pl.BlockSpec((1, tm, D), lambda b, i: (b, i, 0))] + [kvspec] * (2 * n_pat),
        compiler_params=_cparams(("parallel", "parallel")),
        name="dil_inproj",
    )(x, shift, scale, wp, *tabs)
    return outs[0], outs[1:1 + n_pat], outs[1 + n_pat:]


def _dil_attn_kernel(steps, first, last, *refs):
    if first:
        q_ref, kc_ref, kp_ref, vc_ref, vp_ref = refs[:5]
        acc_in = ml_in = None
        outs = refs[5:]
    else:
        q_ref, kc_ref, kp_ref, vc_ref, vp_ref, acc_in, ml_in = refs[:7]
        outs = refs[7:]
    nb = pl.program_id(2)
    blk = DIL_BLOCK
    qi = lax.broadcasted_iota(jnp.int32, (blk, 2 * blk), 0)
    kj = lax.broadcasted_iota(jnp.int32, (blk, 2 * blk), 1)
    dist = blk + qi - kj
    valid = (dist >= 0) & (dist <= steps) & ((nb - 1) * blk + kj >= 0)
    lane = lax.broadcasted_iota(jnp.int32, (1, LANES), 1)
    lo = lane < HEAD_DIM
    lo_b = lo.astype(BF16)
    hi_b = 1.0 - lo_b
    lane_ml = lax.broadcasted_iota(jnp.int32, (blk, LANES), 1)
    ml_old = None if first else ml_in[0]
    ml_new = jnp.zeros((blk, LANES), F32)

    for g in range(N_KV_HEADS):
        seg = slice(g * LANES, (g + 1) * LANES)
        kcat = jnp.concatenate([kp_ref[0][:, seg], kc_ref[0][:, seg]], axis=0)
        vcat = jnp.concatenate([vp_ref[0][:, seg], vc_ref[0][:, seg]], axis=0)
        k_half = (kcat * lo_b, kcat * hi_b)
        v_half = (vcat * lo_b, vcat * hi_b)
        for pair in range(GQA_REP // 2):
            chunk = g * (GQA_REP // 2) + pair
            cs = slice(chunk * LANES, (chunk + 1) * LANES)
            q2 = q_ref[0][:, cs]
            pv_sum = None
            alpha_side, l_side = [], []
            for side in range(2):
                head = 2 * chunk + side
                s = jnp.where(valid, _dot_nt(q2, k_half[side]), NEG_INF)
                m_new = jnp.max(s, axis=1, keepdims=True)
                if not first:
                    m_old = ml_old[:, head:head + 1]
                    m_new = jnp.maximum(m_old, m_new)
                    alpha_side.append(jnp.exp(m_old - m_new))
                p = jnp.exp(s - m_new)
                l_new = jnp.sum(p, axis=1, keepdims=True)
                if not first:
                    l_new = l_new + alpha_side[side] * ml_old[:, N_Q_HEADS + head:N_Q_HEADS + head + 1]
                l_side.append(l_new)
                pv = _dot(p.astype(BF16), v_half[side])
                pv_sum = pv if pv_sum is None else pv_sum + pv
                ml_new = jnp.where(lane_ml == head, m_new, ml_new)
                ml_new = jnp.where(lane_ml == N_Q_HEADS + head, l_new, ml_new)
            acc = pv_sum
            if not first:
                acc = acc_in[0][:, cs] * jnp.where(lo, alpha_side[0], alpha_side[1]) + pv_sum
            if last:
                outs[0][0, :, cs] = (acc / jnp.where(lo, l_side[0], l_side[1])).astype(BF16)
            else:
                outs[0][0, :, cs] = acc
    if not last:
        outs[1][0] = ml_new


def _dil_attention(q, kds, vds):
    B, S, D = q.shape
    n_pat = len(DIL_PATTERNS)
    acc = ml = None
    for p, (window, dil) in enumerate(DIL_PATTERNS):
        first, last = p == 0, p == n_pat - 1
        L = S // dil
        nblk = L // DIL_BLOCK
        kw = 2 * KV_COLS

        def view(a):
            return a.reshape(B, L, dil * a.shape[2])

        cur = lambda b, c, n: (b, n, c)
        prev = lambda b, c, n: (b, jnp.maximum(n - 1, 0), c)
        qspec = pl.BlockSpec((1, DIL_BLOCK, D), cur)
        in_specs = [qspec, pl.BlockSpec((1, DIL_BLOCK, kw), cur), pl.BlockSpec((1, DIL_BLOCK, kw), prev),
                    pl.BlockSpec((1, DIL_BLOCK, kw), cur), pl.BlockSpec((1, DIL_BLOCK, kw), prev)]
        args = [view(q), view(kds[p]), view(kds[p]), view(vds[p]), view(vds[p])]
        if not first:
            in_specs += [qspec, pl.BlockSpec((1, DIL_BLOCK, LANES), cur)]
            args += [view(acc), view(ml)]
        if last:
            out_shape = [jax.ShapeDtypeStruct((B, L, dil * D), BF16)]
            out_specs = [qspec]
        else:
            out_shape = [jax.ShapeDtypeStruct((B, L, dil * D), F32),
                         jax.ShapeDtypeStruct((B, L, dil * LANES), F32)]
            out_specs = [qspec, pl.BlockSpec((1, DIL_BLOCK, LANES), cur)]
        res = pl.pallas_call(
            functools.partial(_dil_attn_kernel, window // dil, first, last),
            out_shape=out_shape,
            grid=(B, dil, nblk),
            in_specs=in_specs,
            out_specs=out_specs,
            compiler_params=_cparams(("parallel", "parallel", "arbitrary")),
            name=f"dil_attention_{p}",
        )(*args)
        if last:
            return res[0].reshape(B, S, D)
        acc, ml = res[0].reshape(B, S, D), res[1].reshape(B, S, LANES)


def kernel(x, c, positions, ada_w, ada_b, ln_g, ln_b, nsa_w_in, nsa_cmp_pos_k, nsa_cmp_w1_k, nsa_cmp_w2_k, nsa_cmp_pos_v, nsa_cmp_w1_v, nsa_cmp_w2_v, nsa_w_o, dil_w_in, dil_w_o, router_w, router_b, moe_w_gate, moe_w_up, moe_w_down):
    B, S, D = x.shape
    mods = _ada_mods(c, ada_w, ada_b)
    def mod(i, sub):
        m = mods[i * 2 + sub]
        return [m[:, k * D:(k + 1) * D].reshape(B, 1, D) for k in range(3)]
    tabs = _rope_tables(positions)

    for i in range(DEPTH):
        shift, scale, gate = mod(i, 0)
        shift2, scale2, gate2 = mod(i, 1)
        j = i // 2
        if i % 2 == 0:
            qT, ksel, kwin, vTsel, vTwin, kcmp, vcmp, gT = _nsa_inproj(x, shift, scale, nsa_w_in[j], tabs)
            kc, vcT = _compress(kcmp, vcmp, nsa_cmp_pos_k[j], nsa_cmp_w1_k[j], nsa_cmp_w2_k[j],
                                nsa_cmp_pos_v[j], nsa_cmp_w1_v[j], nsa_cmp_w2_v[j], tabs)
            o = _nsa_attention(qT, kc, vcT, ksel, vTsel, kwin, vTwin, gT)
            w_o = nsa_w_o[j]
        else:
            q, kds, vds = _dil_inproj(x, shift, scale, dil_w_in[j], tabs)
            o = _dil_attention(q, kds, vds)
            w_o = dil_w_o[j]
        x1, h2, lgT = _proj_ln(o, x, w_o, gate, ln_g[i, 0], ln_b[i, 0], shift2, scale2, router_w)
        x = _moe_sublayer(x1, h2, lgT, router_b, moe_w_gate[i], moe_w_up[i], moe_w_down[i],
                          gate2, ln_g[i, 1], ln_b[i, 1])
    return x
```

```python
import functools

import numpy as np
import jax
import jax.numpy as jnp
from jax import lax
from jax.experimental import pallas as pl
from jax.experimental.pallas import tpu as pltpu

F32 = jnp.float32
BF16 = jnp.bfloat16
HIGHEST = lax.Precision.HIGHEST
NEG_INF = float("-inf")

D_MODEL = 1024
DEPTH = 2
HEAD_DIM = 64
N_Q_HEADS = D_MODEL // HEAD_DIM
N_KV_HEADS = 4
GQA_REP = N_Q_HEADS // N_KV_HEADS
ROPE_DIM = HEAD_DIM // 4
ROPE_THETA = 500000.0
ATTN_SCALE = HEAD_DIM ** -0.5
LOG2_E = 1.4426950408889634
Q_SCALE = ATTN_SCALE * LOG2_E
KV_COLS = N_KV_HEADS * HEAD_DIM
N_BRANCH = 3
CMP_LEN = 32
CMP_STRIDE = 16
CMP_HIDDEN = 256
SEL_LEN = 64
N_SELECT = 16
WIN_LEN = 512
FORCE_SCORE = 1.0e4
DIL_PATTERNS = ((128, 1), (512, 4), (2048, 16))
DIL_BLOCK = 128
N_EXPERTS = 32
N_GROUPS = 4
EXPERTS_PER_GROUP = N_EXPERTS // N_GROUPS
TOP_K = 2
D_EXPERT = 512
MOE_BLOCK = 128
DN_ALPHA = (2.0 * DEPTH) ** 0.25
LN_EPS = 1e-5

LANES = 128
VMEM_LIMIT_BYTES = 48 * 1024 * 1024

Q_TILE = 128
KEY_TILE = 128
SEL_TILES = 4
ROW_TILE = 512


def _cparams(semantics):
    return pltpu.CompilerParams(dimension_semantics=semantics, vmem_limit_bytes=VMEM_LIMIT_BYTES)


def _dot(a, b):
    return jnp.dot(a, b, preferred_element_type=F32)


def _dot_nt(a, b):
    return lax.dot_general(a, b, (((1,), (1,)), ((), ())), preferred_element_type=F32)


def _ada_kernel(c_ref, w_ref, b_ref, o_ref):
    c = c_ref[...]
    cond = c * jax.nn.sigmoid(c)
    o_ref[0] = jnp.dot(cond, w_ref[0], preferred_element_type=F32, precision=HIGHEST) + b_ref[0]


def _ada_mods(c, ada_w, ada_b):
    B, D = c.shape
    n_sub = ada_w.shape[0] * ada_w.shape[1]
    w = ada_w.reshape(n_sub, D, 3 * D)
    b = ada_b.reshape(n_sub, 1, 3 * D)
    c8 = jnp.zeros((8, D), F32).at[:B].set(c)
    tn = 768
    out = pl.pallas_call(
        _ada_kernel,
        out_shape=jax.ShapeDtypeStruct((n_sub, 8, 3 * D), F32),
        grid=(n_sub, 3 * D // tn),
        in_specs=[
            pl.BlockSpec((8, D), lambda s, j: (0, 0)),
            pl.BlockSpec((1, D, tn), lambda s, j: (s, 0, j)),
            pl.BlockSpec((1, 1, tn), lambda s, j: (s, 0, j)),
        ],
        out_specs=pl.BlockSpec((1, 8, tn), lambda s, j: (s, 0, j)),
        compiler_params=_cparams(("parallel", "parallel")),
        name="ada_mods",
    )(c8, w, b)
    return out[:, :B]


def _rope_tab_kernel(pos_ref, inv_ref, sg1_ref, sg2_ref, c_ref, s1_ref, s2_ref):
    ang = pos_ref[0] * inv_ref[...]
    sin = jnp.sin(ang)
    c_ref[0] = jnp.cos(ang)
    s1_ref[0] = sin * sg1_ref[...]
    s2_ref[0] = sin * sg2_ref[...]


def _rope_tables(positions):
    B, S = positions.shape
    half = ROPE_DIM // 2
    inv = ROPE_THETA ** (-jnp.arange(half, dtype=F32) * (2.0 / ROPE_DIM))
    li = np.arange(LANES) % HEAD_DIM
    in_rope = li < ROPE_DIM
    inv_row = jnp.where(jnp.asarray(in_rope), inv[li % half], 0.0).reshape(1, LANES)
    sg1 = jnp.asarray(np.where(li < half, -1.0, 0.0), F32).reshape(1, LANES)
    sg2 = jnp.asarray(np.where((li >= half) & in_rope, 1.0, 0.0), F32).reshape(1, LANES)
    pos = positions.astype(F32).reshape(B, S, 1)
    tm = min(S, 2048)
    row = pl.BlockSpec((1, LANES), lambda b, i: (0, 0))
    tab = pl.BlockSpec((1, tm, LANES), lambda b, i: (b, i, 0))
    return pl.pallas_call(
        _rope_tab_kernel,
        out_shape=[jax.ShapeDtypeStruct((B, S, LANES), F32)] * 3,
        grid=(B, S // tm),
        in_specs=[pl.BlockSpec((1, tm, 1), lambda b, i: (b, i, 0)), row, row, row],
        out_specs=[tab, tab, tab],
        compiler_params=_cparams(("parallel", "parallel")),
        name="rope_tables",
    )(pos, inv_row, sg1, sg2)


def _rope128(t, c, s1, s2):
    return t * c + pltpu.roll(t, LANES - ROPE_DIM // 2, 1) * s1 + pltpu.roll(t, ROPE_DIM // 2, 1) * s2


def _rope_cols(a, c, s1, s2):
    n = a.shape[1] // LANES
    return jnp.concatenate(
        [_rope128(a[:, k * LANES:(k + 1) * LANES], c, s1, s2) for k in range(n)], axis=1)


def _nsa_weight_cols():
    d = D_MODEL
    def kv(branch, which):
        base = d + (branch * 2 + which) * KV_COLS
        return list(range(base, base + KV_COLS))
    cols = list(range(d))
    cols += kv(1, 0) + kv(2, 0) + kv(1, 1) + kv(2, 1) + kv(0, 0) + kv(0, 1)
    gate0 = d + N_BRANCH * 2 * KV_COLS
    gcols = [-1] * LANES
    for g in range(N_KV_HEADS):
        for br in range(N_BRANCH):
            for r in range(GQA_REP):
                gcols[g * 16 + br * GQA_REP + r] = gate0 + (g * GQA_REP + r) * N_BRANCH + br
    return np.asarray(cols + gcols)


def _permute_cols(w, cols):
    picked = w[:, np.maximum(cols, 0)]
    return jnp.where(jnp.asarray(cols >= 0)[None, :], picked, 0.0).astype(BF16)


def _nsa_inproj_kernel(x_ref, sh_ref, sc_ref, w_ref, c_ref, s1_ref, s2_ref,
                       qT_ref, ksel_ref, kwin_ref, vTsel_ref, vTwin_ref, kcmp_ref, vcmp_ref, gT_ref):
    tm = x_ref.shape[1]
    h = (x_ref[0] * (1.0 + sc_ref[0]) + sh_ref[0]).astype(BF16)
    c, s1, s2 = c_ref[0], s1_ref[0], s2_ref[0]
    w = KV_COLS

    def proj(j, n=w):
        return _dot(h, w_ref[:, j * w:j * w + n])

    for j in range(4):
        a = _rope_cols(proj(j), c, s1, s2) * Q_SCALE
        qT_ref[0, j * w:(j + 1) * w, :] = a.T.astype(BF16)
    for j, ref in ((4, ksel_ref), (5, kwin_ref)):
        a = _rope_cols(proj(j), c, s1, s2)
        for g in range(N_KV_HEADS):
            ref[0, g] = a[:, g * HEAD_DIM:(g + 1) * HEAD_DIM].astype(BF16)
    for j, ref in ((6, vTsel_ref), (7, vTwin_ref)):
        aT = proj(j).T.astype(BF16)
        for k in range(tm // KEY_TILE):
            ref[0, k] = aT[:, k * KEY_TILE:(k + 1) * KEY_TILE]
    for j, ref in ((8, kcmp_ref), (9, vcmp_ref)):
        a = proj(j)
        for g in range(N_KV_HEADS):
            ref[0, g] = a[:, g * HEAD_DIM:(g + 1) * HEAD_DIM].astype(BF16)
    gates = jax.nn.sigmoid(proj(10, LANES))
    gT_ref[0] = gates.T[:4 * 16]


def _nsa_inproj(x, shift, scale, w_in, tabs):
    B, S, D = x.shape
    tm = ROW_TILE
    wp = _permute_cols(w_in, _nsa_weight_cols())
    ncol = wp.shape[1]
    vec = pl.BlockSpec((1, 1, D), lambda b, i: (b, 0, 0))
    tab = pl.BlockSpec((1, tm, LANES), lambda b, i: (b, i, 0))
    nat = pl.BlockSpec((1, N_KV_HEADS, tm, HEAD_DIM), lambda b, i: (b, 0, i, 0))
    vt = pl.BlockSpec((1, tm // KEY_TILE, KV_COLS, KEY_TILE), lambda b, i: (b, i, 0, 0))
    nat_shape = jax.ShapeDtypeStruct((B, N_KV_HEADS, S, HEAD_DIM), BF16)
    vt_shape = jax.ShapeDtypeStruct((B, S // KEY_TILE, KV_COLS, KEY_TILE), BF16)
    return pl.pallas_call(
        _nsa_inproj_kernel,
        out_shape=[
            jax.ShapeDtypeStruct((B, D, S), BF16),
            nat_shape, nat_shape,
            vt_shape, vt_shape,
            nat_shape, nat_shape,
            jax.ShapeDtypeStruct((B, 4 * 16, S), F32),
        ],
        grid=(B, S // tm),
        in_specs=[
            pl.BlockSpec((1, tm, D), lambda b, i: (b, i, 0)), vec, vec,
            pl.BlockSpec((D, ncol), lambda b, i: (0, 0)), tab, tab, tab,
        ],
        out_specs=[
            pl.BlockSpec((1, D, tm), lambda b, i: (b, 0, i)),
            nat, nat, vt, vt, nat, nat,
            pl.BlockSpec((1, 4 * 16, tm), lambda b, i: (b, 0, i)),
        ],
        compiler_params=_cparams(("parallel", "parallel")),
        name="nsa_inproj",
    )(x, shift, scale, wp, *tabs)


def _compress_kernel(xk_ref, xv_ref, w1k_ref, w1v_ref, pk_ref, pv_ref, w2k_ref, w2vT_ref,
                     c_ref, s1_ref, s2_ref, kc_ref, vcT_ref):
    n = xk_ref.shape[2]
    half = w1k_ref.shape[0] // 2

    def hidden(x_ref, w1_ref, p_ref):
        x = x_ref[0, 0]
        first = _dot(x, w1_ref[:half])
        second = _dot(x, w1_ref[half:])
        bias = _dot(p_ref[...], w1_ref[...])[0:1]
        hid = first + pltpu.roll(second, n - 1, 0) + bias
        return jax.nn.gelu(hid).astype(BF16)

    kc = _dot(hidden(xk_ref, w1k_ref, pk_ref), w2k_ref[...])
    kc = _rope128(kc, c_ref[0], s1_ref[0], s2_ref[0])
    row = lax.broadcasted_iota(jnp.int32, kc.shape, 0)
    kc = jnp.where(row < n - 1, kc, 0.0)
    kc_ref[0, 0] = kc[:, :HEAD_DIM].astype(BF16)

    vcT = _dot_nt(w2vT_ref[...], hidden(xv_ref, w1v_ref, pv_ref))
    col = lax.broadcasted_iota(jnp.int32, vcT.shape, 1)
    vcT = jnp.where(col < n - 1, vcT, 0.0).astype(BF16)
    for k in range(n // KEY_TILE):
        vcT_ref[0, 0, k] = vcT[:, k * KEY_TILE:(k + 1) * KEY_TILE]


def _compress(kcmp, vcmp, pos_k, w1_k, w2_k, pos_v, w1_v, w2_v, tabs):
    B, G, S, E = kcmp.shape
    n = S // CMP_STRIDE
    wide = CMP_STRIDE * E
    xk = kcmp.reshape(B, G, n, wide)
    xv = vcmp.reshape(B, G, n, wide)
    def flat8(p):
        return jnp.zeros((8, CMP_LEN * E), BF16).at[0].set(p.reshape(-1).astype(BF16))
    w2k = jnp.zeros((CMP_HIDDEN, LANES), BF16).at[:, :E].set(w2_k.astype(BF16))
    w2vT = w2_v.T.astype(BF16)
    last = CMP_LEN - 1
    ctabs = [jnp.zeros((B, n, LANES), F32).at[:, :n - 1].set(t[:, last::CMP_STRIDE][:, :n - 1]) for t in tabs]
    xspec = pl.BlockSpec((1, 1, n, wide), lambda b, g: (b, g, 0, 0))
    w1spec = pl.BlockSpec((CMP_LEN * E, CMP_HIDDEN), lambda b, g: (0, 0))
    pspec = pl.BlockSpec((8, CMP_LEN * E), lambda b, g: (0, 0))
    tspec = pl.BlockSpec((1, n, LANES), lambda b, g: (b, 0, 0))
    return pl.pallas_call(
        _compress_kernel,
        out_shape=[
            jax.ShapeDtypeStruct((B, G, n, E), BF16),
            jax.ShapeDtypeStruct((B, G, n // KEY_TILE, E, KEY_TILE), BF16),
        ],
        grid=(B, G),
        in_specs=[xspec, xspec, w1spec, w1spec, pspec, pspec,
                  pl.BlockSpec((CMP_HIDDEN, LANES), lambda b, g: (0, 0)),
                  pl.BlockSpec((E, CMP_HIDDEN), lambda b, g: (0, 0)),
                  tspec, tspec, tspec],
        out_specs=[
            pl.BlockSpec((1, 1, n, E), lambda b, g: (b, g, 0, 0)),
            pl.BlockSpec((1, 1, n // KEY_TILE, E, KEY_TILE), lambda b, g: (b, g, 0, 0, 0)),
        ],
        compiler_params=_cparams(("parallel", "parallel")),
        name="nsa_compress",
    )(xk, xv, w1_k.astype(BF16), w1_v.astype(BF16), flat8(pos_k), flat8(pos_v), w2k, w2vT, *ctabs)


def _flash_update(s, vT_tiles, m, l, acc):
    m_new = jnp.maximum(m, jnp.max(s, axis=0, keepdims=True))
    m_safe = jnp.where(m_new == NEG_INF, 0.0, m_new)
    p = jnp.exp2(s - m_safe)
    alpha = jnp.exp2(m - m_safe)
    l_new = alpha * l + jnp.sum(p, axis=0, keepdims=True)
    pb = p.astype(BF16)
    pv = None
    for k, vT in enumerate(vT_tiles):
        part = _dot(vT, pb[k * KEY_TILE:(k + 1) * KEY_TILE])
        pv = part if pv is None else pv + part
    return m_new, l_new, alpha * acc + pv


def _nsa_attn_kernel(qT_ref, kc_ref, vcT_ref, ov_ref, ksel_ref, vTsel_ref, kwin_ref, vTwin_ref, gT_ref,
                     o_ref, s_buf, imp_buf, sel_buf, sq0_buf, sq1_buf):
    i = pl.program_id(2)
    tq = Q_TILE
    m_lanes = GQA_REP * tq
    e = HEAD_DIM
    t0 = i * tq
    n_chunks = kc_ref.shape[2]
    n_sel = sel_buf.shape[0]

    qT = qT_ref[0]
    qTm = jnp.concatenate([qT[r * e:(r + 1) * e] for r in range(GQA_REP)], axis=1)
    lane = lax.broadcasted_iota(jnp.int32, (1, m_lanes), 1)
    tok = t0 + (lane & (tq - 1))
    row_k = lax.broadcasted_iota(jnp.int32, (KEY_TILE, 1), 0)

    def new_state():
        return (jnp.full((1, m_lanes), NEG_INF, F32), jnp.zeros((1, m_lanes), F32),
                jnp.zeros((e, m_lanes), F32))

    def normalise(l, acc):
        return acc / jnp.maximum(l, 1e-30)

    n_vis = jnp.minimum(i // (KEY_TILE // 8) + 1, n_chunks)

    def cmp_scores(c, m):
        s = _dot(kc_ref[0, 0, c], qTm)
        last_tok = (c * KEY_TILE + row_k) * CMP_STRIDE + (CMP_LEN - 1)
        s = jnp.where(last_tok <= tok, s, NEG_INF)
        s_buf[c] = s
        return jnp.maximum(m, jnp.max(s, axis=0, keepdims=True))

    m_c = lax.fori_loop(0, n_vis, cmp_scores, jnp.full((1, m_lanes), NEG_INF, F32))
    m_c = jnp.where(m_c == NEG_INF, 0.0, m_c)

    imp_buf[...] = jnp.zeros(imp_buf.shape, F32)

    def cmp_accum(c, carry):
        l, acc = carry
        p = jnp.exp2(s_buf[c] - m_c)
        pb = p.astype(BF16)
        imp_buf[...] += _dot(ov_ref[c], pb)
        return l + jnp.sum(p, axis=0, keepdims=True), acc + _dot(vcT_ref[0, 0, c], pb)

    l_c, acc_c = lax.fori_loop(
        0, n_vis, cmp_accum, (jnp.zeros((1, m_lanes), F32), jnp.zeros((e, m_lanes), F32)))
    inv_l = 1.0 / jnp.maximum(l_c, 1e-30)
    o_cmp = acc_c * inv_l
    imp_n = imp_buf[...] * inv_l
    imp = imp_n[:, 0:tq]
    for r in range(1, GQA_REP):
        imp = imp + imp_n[:, r * tq:(r + 1) * tq]

    sidx = lax.broadcasted_iota(jnp.int32, (n_sel, tq), 0)
    cur = (t0 + lax.broadcasted_iota(jnp.int32, (1, tq), 1)) // SEL_LEN
    forced = (sidx == 0) | (sidx == cur) | (sidx == cur - 1)
    vals = jnp.where(forced, FORCE_SCORE, imp)
    vals = jnp.where(sidx <= cur, vals, NEG_INF)

    def pick(_, rest):
        top = jnp.max(rest, axis=0, keepdims=True)
        first = jnp.min(jnp.where(rest == top, sidx, n_sel), axis=0, keepdims=True)
        return jnp.where(sidx == first, NEG_INF, rest)

    rest = lax.fori_loop(0, min(N_SELECT, n_sel), pick, vals)
    sel_buf[...] = jnp.where(rest < vals, 0.0, NEG_INF)

    blocks_per_tile = KEY_TILE // SEL_LEN

    def sel_scores(c):
        base = c * SEL_TILES
        ks = jnp.concatenate([ksel_ref[0, 0, base + k] for k in range(SEL_TILES)], axis=0)
        return _dot(ks, qTm)

    def sel_chunk(sq_ref, c, state, causal):
        m, l, acc = state
        base = c * SEL_TILES
        n_blocks = SEL_TILES * blocks_per_tile
        s = sq_ref[...]
        if causal:
            key = base * KEY_TILE + lax.broadcasted_iota(jnp.int32, (SEL_TILES * KEY_TILE, 1), 0)
            s = jnp.where(key <= tok, s, NEG_INF)
        blocks = [s[h * SEL_LEN:(h + 1) * SEL_LEN] for h in range(n_blocks)]
        bias = [jnp.concatenate([sel_buf[pl.ds(base * blocks_per_tile + h, 1), :]] * GQA_REP, axis=1)
                for h in range(n_blocks)]
        part = None
        for h in range(n_blocks):
            blk = jnp.max(blocks[h].reshape(SEL_LEN // 8, 8, m_lanes), axis=0) + bias[h]
            part = blk if part is None else jnp.maximum(part, blk)
        m_new = jnp.maximum(m, jnp.max(part, axis=0, keepdims=True))
        m_safe = jnp.where(m_new == NEG_INF, 0.0, m_new)
        p = jnp.concatenate([jnp.exp2(blocks[h] + (bias[h] - m_safe)) for h in range(n_blocks)], axis=0)
        alpha = jnp.exp2(m - m_safe)
        l_new = alpha * l + jnp.sum(p, axis=0, keepdims=True)
        pb = p.astype(BF16)
        pv = None
        for k in range(SEL_TILES):
            part = _dot(vTsel_ref[0, base + k, :, :], pb[k * KEY_TILE:(k + 1) * KEY_TILE])
            pv = part if pv is None else pv + part
        return m_new, l_new, alpha * acc + pv

    def sel_pair(cp, state):
        c0 = 2 * cp
        sq1_buf[...] = sel_scores(c0 + 1)
        state = sel_chunk(sq0_buf, c0, state, False)
        sq0_buf[...] = sel_scores(c0 + 2)
        return sel_chunk(sq1_buf, c0 + 1, state, False)

    n_pairs = (i // SEL_TILES) // 2
    sq0_buf[...] = sel_scores(0)
    state = lax.fori_loop(0, n_pairs, sel_pair, new_state())
    c_tail = 2 * n_pairs
    sq1_buf[...] = sel_scores(c_tail + 1)
    state = sel_chunk(sq0_buf, c_tail, state, True)
    _, l_s, acc_s = sel_chunk(sq1_buf, c_tail + 1, state, True)
    o_sel = normalise(l_s, acc_s)

    n_win = WIN_LEN // KEY_TILE + 1
    j0 = jnp.maximum(i - (n_win - 1), 0)
    kw = jnp.concatenate([kwin_ref[0, 0, j0 + k] for k in range(n_win)], axis=0)
    dist = tok - (j0 * KEY_TILE + lax.broadcasted_iota(jnp.int32, (n_win * KEY_TILE, 1), 0))
    s = jnp.where((dist >= 0) & (dist < WIN_LEN), _dot(kw, qTm), NEG_INF)
    _, l_w, acc_w = _flash_update(s, [vTwin_ref[0, j0 + k, :, :] for k in range(n_win)], *new_state())
    o_win = normalise(l_w, acc_w)

    def gate(branch):
        g = gT_ref[0]
        return jnp.concatenate([g[branch * GQA_REP + r:branch * GQA_REP + r + 1, :] for r in range(GQA_REP)], axis=1)

    oT = o_cmp * gate(0) + o_sel * gate(1) + o_win * gate(2)
    o_rows = jnp.concatenate([oT[:, r * tq:(r + 1) * tq] for r in range(GQA_REP)], axis=0)
    o_ref[0] = o_rows.T.astype(BF16)


def _overlap_tiles(n_sel, n_cmp_pad):
    cs = np.arange(n_cmp_pad)[None, :] * CMP_STRIDE
    ss = np.arange(n_sel)[:, None] * SEL_LEN
    ov = ((cs < ss + SEL_LEN) & (cs + CMP_LEN > ss)).astype(np.float32)
    ov = ov.reshape(n_sel, n_cmp_pad // KEY_TILE, KEY_TILE).transpose(1, 0, 2)
    return jnp.asarray(ov, BF16)


def _nsa_attention(qT, kc, vcT, ksel, vTsel, kwin, vTwin, gT):
    B, D, S = qT.shape
    G, E = N_KV_HEADS, HEAD_DIM
    n_sel = S // SEL_LEN
    n_tiles = S // KEY_TILE
    n_chunks = kc.shape[2] // KEY_TILE
    kc5 = kc.reshape(B, G, n_chunks, KEY_TILE, E)
    ksel5 = ksel.reshape(B, G, n_tiles, KEY_TILE, E)
    kwin5 = kwin.reshape(B, G, n_tiles, KEY_TILE, E)
    ov = _overlap_tiles(n_sel, n_chunks * KEY_TILE)
    kspec = pl.BlockSpec((1, 1, n_tiles, KEY_TILE, E), lambda b, g, i: (b, g, 0, 0, 0))
    vspec = pl.BlockSpec((1, n_tiles, E, KEY_TILE), lambda b, g, i: (b, 0, g, 0))
    m_lanes = GQA_REP * Q_TILE
    return pl.pallas_call(
        _nsa_attn_kernel,
        out_shape=jax.ShapeDtypeStruct((B, S, D), BF16),
        grid=(B, G, S // Q_TILE),
        in_specs=[
            pl.BlockSpec((1, GQA_REP * E, Q_TILE), lambda b, g, i: (b, g, i)),
            pl.BlockSpec((1, 1, n_chunks, KEY_TILE, E), lambda b, g, i: (b, g, 0, 0, 0)),
            pl.BlockSpec((1, 1, n_chunks, E, KEY_TILE), lambda b, g, i: (b, g, 0, 0, 0)),
            pl.BlockSpec((n_chunks, n_sel, KEY_TILE), lambda b, g, i: (0, 0, 0)),
            kspec, vspec, kspec, vspec,
            pl.BlockSpec((1, 16, Q_TILE), lambda b, g, i: (b, g, i)),
        ],
        out_specs=pl.BlockSpec((1, Q_TILE, GQA_REP * E), lambda b, g, i: (b, i, g)),
        scratch_shapes=[
            pltpu.VMEM((n_chunks, KEY_TILE, m_lanes), F32),
            pltpu.VMEM((n_sel, m_lanes), F32),
            pltpu.VMEM((n_sel, Q_TILE), F32),
            pltpu.VMEM((SEL_TILES * KEY_TILE, m_lanes), F32),
            pltpu.VMEM((SEL_TILES * KEY_TILE, m_lanes), F32),
        ],
        compiler_params=_cparams(("parallel", "parallel", "arbitrary")),
        name="nsa_attention",
    )(qT, kc5, vcT, ov, ksel5, vTsel, kwin5, vTwin, gT)


def _layer_norm(z, g, b):
    mu = jnp.mean(z, axis=-1, keepdims=True)
    d = z - mu
    var = jnp.mean(d * d, axis=-1, keepdims=True)
    return d * lax.rsqrt(var + LN_EPS) * g + b


def _proj_ln_kernel(o_ref, x_ref, w_ref, gate_ref, g_ref, b_ref, sh_ref, sc_ref, rw_ref,
                    x1_ref, h_ref, lgT_ref):
    y = _dot(o_ref[0], w_ref[...])
    xn = _layer_norm(DN_ALPHA * x_ref[0] + gate_ref[0] * y, g_ref[...], b_ref[...])
    x1_ref[0] = xn
    h = xn * (1.0 + sc_ref[0]) + sh_ref[0]
    h_ref[0] = h
    logits = jnp.dot(h, rw_ref[...], preferred_element_type=F32, precision=HIGHEST)
    lgT_ref[0] = logits.T[:N_EXPERTS]


def _proj_ln(o, x, w_o, gate, ln_g, ln_b, shift2, scale2, router_w):
    B, S, D = x.shape
    tm = ROW_TILE
    rw = jnp.zeros((D, LANES), F32).at[:, :N_EXPERTS].set(router_w)
    vec = pl.BlockSpec((1, 1, D), lambda b, i: (b, 0, 0))
    par = pl.BlockSpec((1, D), lambda b, i: (0, 0))
    row = pl.BlockSpec((1, tm, D), lambda b, i: (b, i, 0))
    return pl.pallas_call(
        _proj_ln_kernel,
        out_shape=[
            jax.ShapeDtypeStruct((B, S, D), F32),
            jax.ShapeDtypeStruct((B, S, D), F32),
            jax.ShapeDtypeStruct((B, N_EXPERTS, S), F32),
        ],
        grid=(B, S // tm),
        in_specs=[row, row, pl.BlockSpec((D, D), lambda b, i: (0, 0)), vec, par, par, vec, vec,
                  pl.BlockSpec((D, LANES), lambda b, i: (0, 0))],
        out_specs=[row, row, pl.BlockSpec((1, N_EXPERTS, tm), lambda b, i: (b, 0, i))],
        compiler_params=_cparams(("parallel", "parallel")),
        name="proj_ln",
    )(o, x, w_o.astype(BF16), gate, ln_g.reshape(1, D), ln_b.reshape(1, D), shift2, scale2, rw)


def _first_max(v, idx, big):
    top = jnp.max(v, axis=0, keepdims=True)
    first = jnp.min(jnp.where(v == top, idx, big), axis=0, keepdims=True)
    return top, first


def _route_kernel(lg_ref, rb_ref, e_ref, w_ref):
    scores = jax.nn.sigmoid(lg_ref[0])
    biased = scores + rb_ref[...]
    eidx = lax.broadcasted_iota(jnp.int32, scores.shape, 0)
    npg = EXPERTS_PER_GROUP
    best_v, best_g = None, None
    for g in range(N_GROUPS):
        v = biased[g * npg:(g + 1) * npg]
        ii = g * npg + lax.broadcasted_iota(jnp.int32, v.shape, 0)
        top1, i1 = _first_max(v, ii, N_EXPERTS)
        top2 = jnp.max(jnp.where(ii == i1, NEG_INF, v), axis=0, keepdims=True)
        gs = top1 + top2
        if g == 0:
            best_v, best_g = gs, jnp.zeros_like(i1)
        else:
            better = gs > best_v
            best_g = jnp.where(better, g, best_g)
            best_v = jnp.where(better, gs, best_v)
    masked = jnp.where(eidx // npg == best_g, biased, NEG_INF)
    _, e1 = _first_max(masked, eidx, N_EXPERTS)
    _, e2 = _first_max(jnp.where(eidx == e1, NEG_INF, masked), eidx, N_EXPERTS)
    sc1 = jnp.sum(jnp.where(eidx == e1, scores, 0.0), axis=0, keepdims=True)
    sc2 = jnp.sum(jnp.where(eidx == e2, scores, 0.0), axis=0, keepdims=True)
    tot = sc1 + sc2
    e_ref[0] = jnp.concatenate([e1, e2], axis=0)
    w_ref[0] = jnp.concatenate([sc1 / tot, sc2 / tot], axis=0)


def _route(lgT, router_b):
    B, E, S = lgT.shape
    tn = min(S, 2048)
    return pl.pallas_call(
        _route_kernel,
        out_shape=[jax.ShapeDtypeStruct((B, TOP_K, S), jnp.int32), jax.ShapeDtypeStruct((B, TOP_K, S), F32)],
        grid=(B, S // tn),
        in_specs=[pl.BlockSpec((1, E, tn), lambda b, i: (b, 0, i)), pl.BlockSpec((E, 1), lambda b, i: (0, 0))],
        out_specs=[pl.BlockSpec((1, TOP_K, tn), lambda b, i: (b, 0, i))] * 2,
        compiler_params=_cparams(("parallel", "parallel")),
        name="moe_route",
    )(lgT, router_b.reshape(E, 1))


def _dispatch_plan(eidx, wts):
    B, K, S = eidx.shape
    n_tok = B * S
    n_asg = n_tok * K
    e_flat = eidx.transpose(0, 2, 1).reshape(n_asg)
    w_flat = wts.transpose(0, 2, 1).reshape(n_asg)
    onehot = (e_flat[:, None] == jnp.arange(N_EXPERTS, dtype=jnp.int32)[None, :]).astype(jnp.int32)
    csum = jnp.cumsum(onehot, axis=0)
    rank = jnp.take_along_axis(csum, e_flat[:, None], axis=1)[:, 0] - 1
    counts = csum[-1]
    padded = (counts + MOE_BLOCK - 1) // MOE_BLOCK * MOE_BLOCK
    pad_ends = jnp.cumsum(padded)
    dest = (pad_ends - padded)[e_flat] + rank
    n_rows = n_asg + N_EXPERTS * MOE_BLOCK
    n_blk = n_rows // MOE_BLOCK
    asg = jnp.arange(n_asg, dtype=jnp.int32)
    row_asg = jnp.full((n_rows,), -1, jnp.int32).at[dest].set(asg)
    is_pad = row_asg < 0
    pad_rank = jnp.cumsum(is_pad.astype(jnp.int32)) - 1
    row_dst = jnp.where(is_pad, n_asg + pad_rank, row_asg)
    row_src = jnp.where(is_pad, 0, row_asg // K)
    row_w = jnp.zeros((n_rows,), F32).at[dest].set(w_flat)
    blk_start = jnp.arange(n_blk, dtype=jnp.int32) * MOE_BLOCK
    blk_exp = jnp.minimum(jnp.sum((pad_ends[None, :] <= blk_start[:, None]).astype(jnp.int32), axis=1),
                          N_EXPERTS - 1)
    return (blk_exp, row_src.reshape(n_blk, 1, MOE_BLOCK), row_dst.reshape(n_blk, 1, MOE_BLOCK),
            row_w.reshape(n_rows, 1))


def _moe_kernel(blk_exp_ref, src_ref, dst_ref, rw_ref, x_hbm, wg_ref, wu_ref, wd_ref, y_hbm,
                xbuf, ybuf, sems):
    del blk_exp_ref
    rows = xbuf.shape[0]

    def in_copy(r, src_row):
        return pltpu.make_async_copy(x_hbm.at[pl.ds(src_row, 1), :], xbuf.at[pl.ds(r, 1), :], sems.at[0])

    def out_copy(r, dst_row):
        return pltpu.make_async_copy(ybuf.at[pl.ds(r, 1), :], y_hbm.at[pl.ds(dst_row, 1), :], sems.at[1])

    def start_in(r, c):
        in_copy(r, src_ref[0, 0, r]).start()
        return c

    lax.fori_loop(0, rows, start_in, 0, unroll=8)
    pltpu.make_async_copy(x_hbm.at[pl.ds(0, rows), :], xbuf, sems.at[0]).wait()

    x = xbuf[...].astype(BF16)
    gate = _dot(x, wg_ref[0].astype(BF16))
    up = _dot(x, wu_ref[0].astype(BF16))
    hid = (gate * jax.nn.sigmoid(gate) * up).astype(BF16)
    ybuf[...] = _dot(hid, wd_ref[0].astype(BF16)) * rw_ref[...]

    def start_out(r, c):
        out_copy(r, dst_ref[0, 0, r]).start()
        return c

    lax.fori_loop(0, rows, start_out, 0, unroll=8)
    pltpu.make_async_copy(ybuf, y_hbm.at[pl.ds(0, rows), :], sems.at[1]).wait()


def _moe_experts(h2d, plan, w_gate, w_up, w_down):
    blk_exp, row_src, row_dst, row_w = plan
    n_blk = row_src.shape[0]
    n_rows = n_blk * MOE_BLOCK
    D, F = w_gate.shape[1], w_gate.shape[2]
    idx = pl.BlockSpec((1, 1, MOE_BLOCK), lambda i, be: (i, 0, 0), memory_space=pltpu.SMEM)
    grid_spec = pltpu.PrefetchScalarGridSpec(
        num_scalar_prefetch=1,
        grid=(n_blk,),
        in_specs=[
            idx, idx,
            pl.BlockSpec((MOE_BLOCK, 1), lambda i, be: (i, 0)),
            pl.BlockSpec(memory_space=pl.ANY),
            pl.BlockSpec((1, D, F), lambda i, be: (be[i], 0, 0)),
            pl.BlockSpec((1, D, F), lambda i, be: (be[i], 0, 0)),
            pl.BlockSpec((1, F, D), lambda i, be: (be[i], 0, 0)),
        ],
        out_specs=pl.BlockSpec(memory_space=pl.ANY),
        scratch_shapes=[
            pltpu.VMEM((MOE_BLOCK, D), F32),
            pltpu.VMEM((MOE_BLOCK, D), F32),
            pltpu.SemaphoreType.DMA((2,)),
        ],
    )
    return pl.pallas_call(
        _moe_kernel,
        out_shape=jax.ShapeDtypeStruct((n_rows, D), F32),
        grid_spec=grid_spec,
        compiler_params=_cparams(("arbitrary",)),
        name="moe_experts",
    )(blk_exp, row_src, row_dst, row_w, h2d, w_gate, w_up, w_down)


def _combine_ln_kernel(y_ref, x_ref, gate_ref, g_ref, b_ref, o_ref):
    d = x_ref.shape[2]
    y = y_ref[:, :d] + y_ref[:, d:]
    o_ref[0] = _layer_norm(DN_ALPHA * x_ref[0] + gate_ref[0] * y, g_ref[...], b_ref[...])


def _combine_ln(ybuf, x, gate, ln_g, ln_b):
    B, S, D = x.shape
    tm = ROW_TILE
    y2 = ybuf.reshape(ybuf.shape[0] // TOP_K, TOP_K * D)
    per_b = S // tm
    vec = pl.BlockSpec((1, 1, D), lambda b, i: (b, 0, 0))
    par = pl.BlockSpec((1, D), lambda b, i: (0, 0))
    row = pl.BlockSpec((1, tm, D), lambda b, i: (b, i, 0))
    return pl.pallas_call(
        _combine_ln_kernel,
        out_shape=jax.ShapeDtypeStruct((B, S, D), F32),
        grid=(B, per_b),
        in_specs=[pl.BlockSpec((tm, TOP_K * D), lambda b, i: (b * per_b + i, 0)), row, vec, par, par],
        out_specs=row,
        compiler_params=_cparams(("parallel", "parallel")),
        name="moe_combine_ln",
    )(y2, x, gate, ln_g.reshape(1, D), ln_b.reshape(1, D))


def _moe_sublayer(x1, h2, lgT, router_b, w_gate, w_up, w_down, gate, ln_g, ln_b):
    B, S, D = x1.shape
    eidx, wts = _route(lgT, router_b)
    plan = _dispatch_plan(eidx, wts)
    ybuf = _moe_experts(h2.reshape(B * S, D), plan, w_gate, w_up, w_down)
    return _combine_ln(ybuf, x1, gate, ln_g, ln_b)


def _dil_weight_cols():
    d = D_MODEL
    cols = list(range(d))
    for which in range(2):
        for p in range(len(DIL_PATTERNS)):
            base = d + (p * 2 + which) * KV_COLS
            for g in range(N_KV_HEADS):
                head = list(range(base + g * HEAD_DIM, base + (g + 1) * HEAD_DIM))
                cols += head + head
    return np.asarray(cols)


def _dil_inproj_kernel(x_ref, sh_ref, sc_ref, w_ref, c_ref, s1_ref, s2_ref, q_ref, *kv_refs):
    h = (x_ref[0] * (1.0 + sc_ref[0]) + sh_ref[0]).astype(BF16)
    c, s1, s2 = c_ref[0], s1_ref[0], s2_ref[0]
    d = q_ref.shape[2]
    w = 2 * KV_COLS
    n_pat = len(DIL_PATTERNS)
    for j in range(d // w):
        a = _rope_cols(_dot(h, w_ref[:, j * w:(j + 1) * w]), c, s1, s2) * Q_SCALE
        q_ref[0, :, j * w:(j + 1) * w] = a.astype(BF16)
    for p in range(n_pat):
        a = _rope_cols(_dot(h, w_ref[:, d + p * w:d + (p + 1) * w]), c, s1, s2)
        kv_refs[p][0] = a.astype(BF16)
    for p in range(n_pat):
        a = _dot(h, w_ref[:, d + (n_pat + p) * w:d + (n_pat + p + 1) * w])
        kv_refs[n_pat + p][0] = a.astype(BF16)


def _dil_inproj(x, shift, scale, w_in, tabs):
    B, S, D = x.shape
    tm = ROW_TILE
    wp = _permute_cols(w_in, _dil_weight_cols())
    ncol = wp.shape[1]
    n_pat = len(DIL_PATTERNS)
    vec = pl.BlockSpec((1, 1, D), lambda b, i: (b, 0, 0))
    tab = pl.BlockSpec((1, tm, LANES), lambda b, i: (b, i, 0))
    kvspec = pl.BlockSpec((1, tm, 2 * KV_COLS), lambda b, i: (b, i, 0))
    outs = pl.pallas_call(
        _dil_inproj_kernel,
        out_shape=[jax.ShapeDtypeStruct((B, S, D), BF16)]
        + [jax.ShapeDtypeStruct((B, S, 2 * KV_COLS), BF16)] * (2 * n_pat),
        grid=(B, S // tm),
        in_specs=[pl.BlockSpec((1, tm, D), lambda b, i: (b, i, 0)), vec, vec,
                  pl.BlockSpec((D, ncol), lambda b, i: (0, 0)), tab, tab, tab],
        out_specs=[pl.BlockSpec((1, tm, D), lambda b, i: (b, i, 0))] + [kvspec] * (2 * n_pat),
        compiler_params=_cparams(("parallel", "parallel")),
        name="dil_inproj",
    )(x, shift, scale, wp, *tabs)
    return outs[0], outs[1:1 + n_pat], outs[1 + n_pat:]


def _dil_attn_kernel(steps, first, last, *refs):
    if first:
        q_ref, kc_ref, kp_ref, vc_ref, vp_ref = refs[:5]
        acc_in = ml_in = None
        outs = refs[5:]
    else:
        q_ref, kc_ref, kp_ref, vc_ref, vp_ref, acc_in, ml_in = refs[:7]
        outs = refs[7:]
    nb = pl.program_id(2)
    blk = DIL_BLOCK
    qi = lax.broadcasted_iota(jnp.int32, (blk, 2 * blk), 0)
    kj = lax.broadcasted_iota(jnp.int32, (blk, 2 * blk), 1)
    dist = blk + qi - kj
    valid = (dist >= 0) & (dist <= steps) & ((nb - 1) * blk + kj >= 0)
    lane = lax.broadcasted_iota(jnp.int32, (1, LANES), 1)
    lo = lane < HEAD_DIM
    lo_b = lo.astype(BF16)
    hi_b = 1.0 - lo_b
    lane_ml = lax.broadcasted_iota(jnp.int32, (blk, LANES), 1)
    ml_old = None if first else ml_in[0]
    ml_new = jnp.zeros((blk, LANES), F32)

    for g in range(N_KV_HEADS):
        seg = slice(g * LANES, (g + 1) * LANES)
        kcat = jnp.concatenate([kp_ref[0][:, seg], kc_ref[0][:, seg]], axis=0)
        vcat = jnp.concatenate([vp_ref[0][:, seg], vc_ref[0][:, seg]], axis=0)
        k_half = (kcat * lo_b, kcat * hi_b)
        v_half = (vcat * lo_b, vcat * hi_b)
        for pair in range(GQA_REP // 2):
            chunk = g * (GQA_REP // 2) + pair
            cs = slice(chunk * LANES, (chunk + 1) * LANES)
            q2 = q_ref[0][:, cs]
            pv_sum = None
            alpha_side, l_side = [], []
            for side in range(2):
                head = 2 * chunk + side
                s = jnp.where(valid, _dot_nt(q2, k_half[side]), NEG_INF)
                m_new = jnp.max(s, axis=1, keepdims=True)
                if not first:
                    m_old = ml_old[:, head:head + 1]
                    m_new = jnp.maximum(m_old, m_new)
                    alpha_side.append(jnp.exp2(m_old - m_new))
                p = jnp.exp2(s - m_new)
                l_new = jnp.sum(p, axis=1, keepdims=True)
                if not first:
                    l_new = l_new + alpha_side[side] * ml_old[:, N_Q_HEADS + head:N_Q_HEADS + head + 1]
                l_side.append(l_new)
                pv = _dot(p.astype(BF16), v_half[side])
                pv_sum = pv if pv_sum is None else pv_sum + pv
                ml_new = jnp.where(lane_ml == head, m_new, ml_new)
                ml_new = jnp.where(lane_ml == N_Q_HEADS + head, l_new, ml_new)
            acc = pv_sum
            if not first:
                acc = acc_in[0][:, cs] * jnp.where(lo, alpha_side[0], alpha_side[1]) + pv_sum
            if last:
                outs[0][0, :, cs] = (acc / jnp.where(lo, l_side[0], l_side[1])).astype(BF16)
            else:
                outs[0][0, :, cs] = acc
    if not last:
        outs[1][0] = ml_new


def _dil_attention(q, kds, vds):
    B, S, D = q.shape
    n_pat = len(DIL_PATTERNS)
    acc = ml = None
    for p, (window, dil) in enumerate(DIL_PATTERNS):
        first, last = p == 0, p == n_pat - 1
        L = S // dil
        nblk = L // DIL_BLOCK
        kw = 2 * KV_COLS

        def view(a):
            return a.reshape(B, L, dil * a.shape[2])

        cur = lambda b, c, n: (b, n, c)
        prev = lambda b, c, n: (b, jnp.maximum(n - 1, 0), c)
        qspec = pl.BlockSpec((1, DIL_BLOCK, D), cur)
        in_specs = [qspec, pl.BlockSpec((1, DIL_BLOCK, kw), cur), pl.BlockSpec((1, DIL_BLOCK, kw), prev),
                    pl.BlockSpec((1, DIL_BLOCK, kw), cur), pl.BlockSpec((1, DIL_BLOCK, kw), prev)]
        args = [view(q), view(kds[p]), view(kds[p]), view(vds[p]), view(vds[p])]
        if not first:
            in_specs += [qspec, pl.BlockSpec((1, DIL_BLOCK, LANES), cur)]
            args += [view(acc), view(ml)]
        if last:
            out_shape = [jax.ShapeDtypeStruct((B, L, dil * D), BF16)]
            out_specs = [qspec]
        else:
            out_shape = [jax.ShapeDtypeStruct((B, L, dil * D), F32),
                         jax.ShapeDtypeStruct((B, L, dil * LANES), F32)]
            out_specs = [qspec, pl.BlockSpec((1, DIL_BLOCK, LANES), cur)]
        res = pl.pallas_call(
            functools.partial(_dil_attn_kernel, window // dil, first, last),
            out_shape=out_shape,
            grid=(B, dil, nblk),
            in_specs=in_specs,
            out_specs=out_specs,
            compiler_params=_cparams(("parallel", "parallel", "arbitrary")),
            name=f"dil_attention_{p}",
        )(*args)
        if last:
            return res[0].reshape(B, S, D)
        acc, ml = res[0].reshape(B, S, D), res[1].reshape(B, S, LANES)


def kernel(x, c, positions, ada_w, ada_b, ln_g, ln_b, nsa_w_in, nsa_cmp_pos_k, nsa_cmp_w1_k, nsa_cmp_w2_k, nsa_cmp_pos_v, nsa_cmp_w1_v, nsa_cmp_w2_v, nsa_w_o, dil_w_in, dil_w_o, router_w, router_b, moe_w_gate, moe_w_up, moe_w_down):
    B, S, D = x.shape
    mods = _ada_mods(c, ada_w, ada_b)
    def mod(i, sub):
        m = mods[i * 2 + sub]
        return [m[:, k * D:(k + 1) * D].reshape(B, 1, D) for k in range(3)]
    tabs = _rope_tables(positions)

    for i in range(DEPTH):
        shift, scale, gate = mod(i, 0)
        shift2, scale2, gate2 = mod(i, 1)
        j = i // 2
        if i % 2 == 0:
            qT, ksel, kwin, vTsel, vTwin, kcmp, vcmp, gT = _nsa_inproj(x, shift, scale, nsa_w_in[j], tabs)
            kc, vcT = _compress(kcmp, vcmp, nsa_cmp_pos_k[j], nsa_cmp_w1_k[j], nsa_cmp_w2_k[j],
                                nsa_cmp_pos_v[j], nsa_cmp_w1_v[j], nsa_cmp_w2_v[j], tabs)
            o = _nsa_attention(qT, kc, vcT, ksel, vTsel, kwin, vTwin, gT)
            w_o = nsa_w_o[j]
        else:
            q, kds, vds = _dil_inproj(x, shift, scale, dil_w_in[j], tabs)
            o = _dil_attention(q, kds, vds)
            w_o = dil_w_o[j]
        x1, h2, lgT = _proj_ln(o, x, w_o, gate, ln_g[i, 0], ln_b[i, 0], shift2, scale2, router_w)
        x = _moe_sublayer(x1, h2, lgT, router_b, moe_w_gate[i], moe_w_up[i], moe_w_down[i],
                          gate2, ln_g[i, 1], ln_b[i, 1])
    return x
```

```python
import functools

import numpy as np
import jax
import jax.numpy as jnp
from jax import lax
from jax.experimental import pallas as pl
from jax.experimental.pallas import tpu as pltpu

F32 = jnp.float32
BF16 = jnp.bfloat16
HIGHEST = lax.Precision.HIGHEST
NEG_INF = float("-inf")

D_MODEL = 1024
DEPTH = 2
HEAD_DIM = 64
N_Q_HEADS = D_MODEL // HEAD_DIM
N_KV_HEADS = 4
GQA_REP = N_Q_HEADS // N_KV_HEADS
ROPE_DIM = HEAD_DIM // 4
ROPE_THETA = 500000.0
ATTN_SCALE = HEAD_DIM ** -0.5
LOG2_E = 1.4426950408889634
Q_SCALE = ATTN_SCALE * LOG2_E
KV_COLS = N_KV_HEADS * HEAD_DIM
N_BRANCH = 3
CMP_LEN = 32
CMP_STRIDE = 16
CMP_HIDDEN = 256
SEL_LEN = 64
N_SELECT = 16
WIN_LEN = 512
FORCE_SCORE = 1.0e4
DIL_PATTERNS = ((128, 1), (512, 4), (2048, 16))
DIL_BLOCK = 128
N_EXPERTS = 32
N_GROUPS = 4
EXPERTS_PER_GROUP = N_EXPERTS // N_GROUPS
TOP_K = 2
D_EXPERT = 512
MOE_BLOCK = 128
DN_ALPHA = (2.0 * DEPTH) ** 0.25
LN_EPS = 1e-5

LANES = 128
VMEM_LIMIT_BYTES = 48 * 1024 * 1024

Q_TILE = 128
KEY_TILE = 128
SEL_TILES = 4
ROW_TILE = 512
MOE_IO_TILE = 256


def _cparams(semantics):
    return pltpu.CompilerParams(dimension_semantics=semantics, vmem_limit_bytes=VMEM_LIMIT_BYTES)


def _dot(a, b):
    return jnp.dot(a, b, preferred_element_type=F32)


def _dot_nt(a, b):
    return lax.dot_general(a, b, (((1,), (1,)), ((), ())), preferred_element_type=F32)


def _ada_kernel(c_ref, w_ref, b_ref, o_ref):
    c = c_ref[...]
    cond = c * jax.nn.sigmoid(c)
    o_ref[0] = jnp.dot(cond, w_ref[0], preferred_element_type=F32, precision=HIGHEST) + b_ref[0]


def _ada_mods(c, ada_w, ada_b):
    B, D = c.shape
    n_sub = ada_w.shape[0] * ada_w.shape[1]
    w = ada_w.reshape(n_sub, D, 3 * D)
    b = ada_b.reshape(n_sub, 1, 3 * D)
    c8 = jnp.zeros((8, D), F32).at[:B].set(c)
    tn = 768
    out = pl.pallas_call(
        _ada_kernel,
        out_shape=jax.ShapeDtypeStruct((n_sub, 8, 3 * D), F32),
        grid=(n_sub, 3 * D // tn),
        in_specs=[
            pl.BlockSpec((8, D), lambda s, j: (0, 0)),
            pl.BlockSpec((1, D, tn), lambda s, j: (s, 0, j)),
            pl.BlockSpec((1, 1, tn), lambda s, j: (s, 0, j)),
        ],
        out_specs=pl.BlockSpec((1, 8, tn), lambda s, j: (s, 0, j)),
        compiler_params=_cparams(("parallel", "parallel")),
        name="ada_mods",
    )(c8, w, b)
    return out[:, :B]


def _rope_tab_kernel(pos_ref, inv_ref, sg1_ref, sg2_ref, c_ref, s1_ref, s2_ref):
    ang = pos_ref[0] * inv_ref[...]
    sin = jnp.sin(ang)
    c_ref[0] = jnp.cos(ang)
    s1_ref[0] = sin * sg1_ref[...]
    s2_ref[0] = sin * sg2_ref[...]


def _rope_tables(positions):
    B, S = positions.shape
    half = ROPE_DIM // 2
    inv = ROPE_THETA ** (-jnp.arange(half, dtype=F32) * (2.0 / ROPE_DIM))
    li = np.arange(LANES) % HEAD_DIM
    in_rope = li < ROPE_DIM
    inv_row = jnp.where(jnp.asarray(in_rope), inv[li % half], 0.0).reshape(1, LANES)
    sg1 = jnp.asarray(np.where(li < half, -1.0, 0.0), F32).reshape(1, LANES)
    sg2 = jnp.asarray(np.where((li >= half) & in_rope, 1.0, 0.0), F32).reshape(1, LANES)
    pos = positions.astype(F32).reshape(B, S, 1)
    tm = min(S, 2048)
    row = pl.BlockSpec((1, LANES), lambda b, i: (0, 0))
    tab = pl.BlockSpec((1, tm, LANES), lambda b, i: (b, i, 0))
    return pl.pallas_call(
        _rope_tab_kernel,
        out_shape=[jax.ShapeDtypeStruct((B, S, LANES), F32)] * 3,
        grid=(B, S // tm),
        in_specs=[pl.BlockSpec((1, tm, 1), lambda b, i: (b, i, 0)), row, row, row],
        out_specs=[tab, tab, tab],
        compiler_params=_cparams(("parallel", "parallel")),
        name="rope_tables",
    )(pos, inv_row, sg1, sg2)


def _rope128(t, c, s1, s2):
    return t * c + pltpu.roll(t, LANES - ROPE_DIM // 2, 1) * s1 + pltpu.roll(t, ROPE_DIM // 2, 1) * s2


def _rope_cols(a, c, s1, s2):
    n = a.shape[1] // LANES
    return jnp.concatenate(
        [_rope128(a[:, k * LANES:(k + 1) * LANES], c, s1, s2) for k in range(n)], axis=1)


def _nsa_weight_cols():
    d = D_MODEL
    def kv(branch, which):
        base = d + (branch * 2 + which) * KV_COLS
        return list(range(base, base + KV_COLS))
    cols = list(range(d))
    cols += kv(1, 0) + kv(2, 0) + kv(1, 1) + kv(2, 1) + kv(0, 0) + kv(0, 1)
    gate0 = d + N_BRANCH * 2 * KV_COLS
    gcols = [-1] * LANES
    for g in range(N_KV_HEADS):
        for br in range(N_BRANCH):
            for r in range(GQA_REP):
                gcols[g * 16 + br * GQA_REP + r] = gate0 + (g * GQA_REP + r) * N_BRANCH + br
    return np.asarray(cols + gcols)


def _permute_cols(w, cols):
    picked = w[:, np.maximum(cols, 0)]
    return jnp.where(jnp.asarray(cols >= 0)[None, :], picked, 0.0).astype(BF16)


def _nsa_inproj_kernel(x_ref, sh_ref, sc_ref, w_ref, c_ref, s1_ref, s2_ref,
                       qT_ref, ksel_ref, kwin_ref, vTsel_ref, vTwin_ref, kcmp_ref, vcmp_ref, gT_ref):
    tm = x_ref.shape[1]
    h = (x_ref[0] * (1.0 + sc_ref[0]) + sh_ref[0]).astype(BF16)
    c, s1, s2 = c_ref[0], s1_ref[0], s2_ref[0]
    w = KV_COLS

    def proj(j, n=w):
        return _dot(h, w_ref[:, j * w:j * w + n])

    for j in range(4):
        a = _rope_cols(proj(j), c, s1, s2) * Q_SCALE
        qT_ref[0, j * w:(j + 1) * w, :] = a.T.astype(BF16)
    for j, ref in ((4, ksel_ref), (5, kwin_ref)):
        a = _rope_cols(proj(j), c, s1, s2)
        for g in range(N_KV_HEADS):
            ref[0, g] = a[:, g * HEAD_DIM:(g + 1) * HEAD_DIM].astype(BF16)
    for j, ref in ((6, vTsel_ref), (7, vTwin_ref)):
        aT = proj(j).T.astype(BF16)
        for k in range(tm // KEY_TILE):
            ref[0, k] = aT[:, k * KEY_TILE:(k + 1) * KEY_TILE]
    for j, ref in ((8, kcmp_ref), (9, vcmp_ref)):
        a = proj(j)
        for g in range(N_KV_HEADS):
            ref[0, g] = a[:, g * HEAD_DIM:(g + 1) * HEAD_DIM].astype(BF16)
    gates = jax.nn.sigmoid(proj(10, LANES))
    gT_ref[0] = gates.T[:4 * 16]


def _nsa_inproj(x, shift, scale, w_in, tabs):
    B, S, D = x.shape
    tm = ROW_TILE
    wp = _permute_cols(w_in, _nsa_weight_cols())
    ncol = wp.shape[1]
    vec = pl.BlockSpec((1, 1, D), lambda b, i: (b, 0, 0))
    tab = pl.BlockSpec((1, tm, LANES), lambda b, i: (b, i, 0))
    nat = pl.BlockSpec((1, N_KV_HEADS, tm, HEAD_DIM), lambda b, i: (b, 0, i, 0))
    vt = pl.BlockSpec((1, tm // KEY_TILE, KV_COLS, KEY_TILE), lambda b, i: (b, i, 0, 0))
    nat_shape = jax.ShapeDtypeStruct((B, N_KV_HEADS, S, HEAD_DIM), BF16)
    vt_shape = jax.ShapeDtypeStruct((B, S // KEY_TILE, KV_COLS, KEY_TILE), BF16)
    return pl.pallas_call(
        _nsa_inproj_kernel,
        out_shape=[
            jax.ShapeDtypeStruct((B, D, S), BF16),
            nat_shape, nat_shape,
            vt_shape, vt_shape,
            nat_shape, nat_shape,
            jax.ShapeDtypeStruct((B, 4 * 16, S), F32),
        ],
        grid=(B, S // tm),
        in_specs=[
            pl.BlockSpec((1, tm, D), lambda b, i: (b, i, 0)), vec, vec,
            pl.BlockSpec((D, ncol), lambda b, i: (0, 0)), tab, tab, tab,
        ],
        out_specs=[
            pl.BlockSpec((1, D, tm), lambda b, i: (b, 0, i)),
            nat, nat, vt, vt, nat, nat,
            pl.BlockSpec((1, 4 * 16, tm), lambda b, i: (b, 0, i)),
        ],
        compiler_params=_cparams(("parallel", "parallel")),
        name="nsa_inproj",
    )(x, shift, scale, wp, *tabs)


def _compress_kernel(xk_ref, xv_ref, w1k_ref, w1v_ref, pk_ref, pv_ref, w2k_ref, w2vT_ref,
                     c_ref, s1_ref, s2_ref, kc_ref, vcT_ref):
    n = xk_ref.shape[2]
    half = w1k_ref.shape[0] // 2

    def hidden(x_ref, w1_ref, p_ref):
        x = x_ref[0, 0]
        first = _dot(x, w1_ref[:half])
        second = _dot(x, w1_ref[half:])
        bias = _dot(p_ref[...], w1_ref[...])[0:1]
        hid = first + pltpu.roll(second, n - 1, 0) + bias
        return jax.nn.gelu(hid).astype(BF16)

    kc = _dot(hidden(xk_ref, w1k_ref, pk_ref), w2k_ref[...])
    kc = _rope128(kc, c_ref[0], s1_ref[0], s2_ref[0])
    row = lax.broadcasted_iota(jnp.int32, kc.shape, 0)
    kc = jnp.where(row < n - 1, kc, 0.0)
    kc_ref[0, 0] = kc[:, :HEAD_DIM].astype(BF16)

    vcT = _dot_nt(w2vT_ref[...], hidden(xv_ref, w1v_ref, pv_ref))
    col = lax.broadcasted_iota(jnp.int32, vcT.shape, 1)
    vcT = jnp.where(col < n - 1, vcT, 0.0).astype(BF16)
    for k in range(n // KEY_TILE):
        vcT_ref[0, 0, k] = vcT[:, k * KEY_TILE:(k + 1) * KEY_TILE]


def _compress(kcmp, vcmp, pos_k, w1_k, w2_k, pos_v, w1_v, w2_v, tabs):
    B, G, S, E = kcmp.shape
    n = S // CMP_STRIDE
    wide = CMP_STRIDE * E
    xk = kcmp.reshape(B, G, n, wide)
    xv = vcmp.reshape(B, G, n, wide)
    def flat8(p):
        return jnp.zeros((8, CMP_LEN * E), BF16).at[0].set(p.reshape(-1).astype(BF16))
    w2k = jnp.zeros((CMP_HIDDEN, LANES), BF16).at[:, :E].set(w2_k.astype(BF16))
    w2vT = w2_v.T.astype(BF16)
    last = CMP_LEN - 1
    ctabs = [jnp.zeros((B, n, LANES), F32).at[:, :n - 1].set(t[:, last::CMP_STRIDE][:, :n - 1]) for t in tabs]
    xspec = pl.BlockSpec((1, 1, n, wide), lambda b, g: (b, g, 0, 0))
    w1spec = pl.BlockSpec((CMP_LEN * E, CMP_HIDDEN), lambda b, g: (0, 0))
    pspec = pl.BlockSpec((8, CMP_LEN * E), lambda b, g: (0, 0))
    tspec = pl.BlockSpec((1, n, LANES), lambda b, g: (b, 0, 0))
    return pl.pallas_call(
        _compress_kernel,
        out_shape=[
            jax.ShapeDtypeStruct((B, G, n, E), BF16),
            jax.ShapeDtypeStruct((B, G, n // KEY_TILE, E, KEY_TILE), BF16),
        ],
        grid=(B, G),
        in_specs=[xspec, xspec, w1spec, w1spec, pspec, pspec,
                  pl.BlockSpec((CMP_HIDDEN, LANES), lambda b, g: (0, 0)),
                  pl.BlockSpec((E, CMP_HIDDEN), lambda b, g: (0, 0)),
                  tspec, tspec, tspec],
        out_specs=[
            pl.BlockSpec((1, 1, n, E), lambda b, g: (b, g, 0, 0)),
            pl.BlockSpec((1, 1, n // KEY_TILE, E, KEY_TILE), lambda b, g: (b, g, 0, 0, 0)),
        ],
        compiler_params=_cparams(("parallel", "parallel")),
        name="nsa_compress",
    )(xk, xv, w1_k.astype(BF16), w1_v.astype(BF16), flat8(pos_k), flat8(pos_v), w2k, w2vT, *ctabs)


def _flash_update(s, vT_tiles, m, l, acc):
    m_new = jnp.maximum(m, jnp.max(s, axis=0, keepdims=True))
    m_safe = jnp.where(m_new == NEG_INF, 0.0, m_new)
    p = jnp.exp2(s - m_safe)
    alpha = jnp.exp2(m - m_safe)
    l_new = alpha * l + jnp.sum(p, axis=0, keepdims=True)
    pb = p.astype(BF16)
    pv = None
    for k, vT in enumerate(vT_tiles):
        part = _dot(vT, pb[k * KEY_TILE:(k + 1) * KEY_TILE])
        pv = part if pv is None else pv + part
    return m_new, l_new, alpha * acc + pv


def _nsa_attn_kernel(qT_ref, kc_ref, vcT_ref, ov_ref, ksel_ref, vTsel_ref, kwin_ref, vTwin_ref, gT_ref,
                     o_ref, s_buf, imp_buf, sel_buf, sq0_buf, sq1_buf):
    i = pl.program_id(2)
    tq = Q_TILE
    m_lanes = GQA_REP * tq
    e = HEAD_DIM
    t0 = i * tq
    n_chunks = kc_ref.shape[2]
    n_sel = sel_buf.shape[0]

    qT = qT_ref[0]
    qTm = jnp.concatenate([qT[r * e:(r + 1) * e] for r in range(GQA_REP)], axis=1)
    lane = lax.broadcasted_iota(jnp.int32, (1, m_lanes), 1)
    tok = t0 + (lane & (tq - 1))
    row_k = lax.broadcasted_iota(jnp.int32, (KEY_TILE, 1), 0)

    def new_state():
        return (jnp.full((1, m_lanes), NEG_INF, F32), jnp.zeros((1, m_lanes), F32),
                jnp.zeros((e, m_lanes), F32))

    def normalise(l, acc):
        return acc / jnp.maximum(l, 1e-30)

    cmp_tiles = s_buf.shape[1] // KEY_TILE
    cmp_rows = cmp_tiles * KEY_TILE
    n_vis = jnp.minimum((i // (KEY_TILE // 8)) // cmp_tiles + 1, n_chunks // cmp_tiles)
    row_c = lax.broadcasted_iota(jnp.int32, (cmp_rows, 1), 0)

    def cmp_scores(c, m):
        kc = jnp.concatenate([kc_ref[0, 0, c * cmp_tiles + k] for k in range(cmp_tiles)], axis=0)
        last_tok = (c * cmp_rows + row_c) * CMP_STRIDE + (CMP_LEN - 1)
        s = jnp.where(last_tok <= tok, _dot(kc, qTm), NEG_INF)
        s_buf[c] = s
        return jnp.maximum(m, jnp.max(s, axis=0, keepdims=True))

    m_c = lax.fori_loop(0, n_vis, cmp_scores, jnp.full((1, m_lanes), NEG_INF, F32))
    m_c = jnp.where(m_c == NEG_INF, 0.0, m_c)

    imp_buf[...] = jnp.zeros(imp_buf.shape, F32)

    def cmp_accum(c, carry):
        l, acc = carry
        p = jnp.exp2(s_buf[c] - m_c)
        pb = p.astype(BF16)
        imp = imp_buf[...]
        for k in range(cmp_tiles):
            pk = pb[k * KEY_TILE:(k + 1) * KEY_TILE]
            imp = imp + _dot(ov_ref[c * cmp_tiles + k], pk)
            acc = acc + _dot(vcT_ref[0, 0, c * cmp_tiles + k], pk)
        imp_buf[...] = imp
        return l + jnp.sum(p, axis=0, keepdims=True), acc

    l_c, acc_c = lax.fori_loop(
        0, n_vis, cmp_accum, (jnp.zeros((1, m_lanes), F32), jnp.zeros((e, m_lanes), F32)))
    inv_l = 1.0 / jnp.maximum(l_c, 1e-30)
    o_cmp = acc_c * inv_l
    imp_n = imp_buf[...] * inv_l
    imp = imp_n[:, 0:tq]
    for r in range(1, GQA_REP):
        imp = imp + imp_n[:, r * tq:(r + 1) * tq]

    sidx = lax.broadcasted_iota(jnp.int32, (n_sel, tq), 0)
    cur = (t0 + lax.broadcasted_iota(jnp.int32, (1, tq), 1)) // SEL_LEN
    forced = (sidx == 0) | (sidx == cur) | (sidx == cur - 1)
    vals = jnp.where(forced, FORCE_SCORE, imp)
    vals = jnp.where(sidx <= cur, vals, NEG_INF)

    def pick(_, rest):
        top = jnp.max(rest, axis=0, keepdims=True)
        first = jnp.min(jnp.where(rest == top, sidx, n_sel), axis=0, keepdims=True)
        return jnp.where(sidx == first, NEG_INF, rest)

    rest = lax.fori_loop(0, min(N_SELECT, n_sel), pick, vals)
    sel_buf[...] = jnp.where(rest < vals, 0.0, NEG_INF)

    blocks_per_tile = KEY_TILE // SEL_LEN

    def sel_scores(c):
        base = c * SEL_TILES
        ks = jnp.concatenate([ksel_ref[0, 0, base + k] for k in range(SEL_TILES)], axis=0)
        return _dot(ks, qTm)

    def sel_chunk(sq_ref, c, state, causal):
        m, l, acc = state
        base = c * SEL_TILES
        n_blocks = SEL_TILES * blocks_per_tile
        s = sq_ref[...]
        if causal:
            key = base * KEY_TILE + lax.broadcasted_iota(jnp.int32, (SEL_TILES * KEY_TILE, 1), 0)
            s = jnp.where(key <= tok, s, NEG_INF)
        blocks = [s[h * SEL_LEN:(h + 1) * SEL_LEN] for h in range(n_blocks)]
        bias = [jnp.concatenate([sel_buf[pl.ds(base * blocks_per_tile + h, 1), :]] * GQA_REP, axis=1)
                for h in range(n_blocks)]
        part = None
        for h in range(n_blocks):
            blk = jnp.max(blocks[h].reshape(SEL_LEN // 8, 8, m_lanes), axis=0) + bias[h]
            part = blk if part is None else jnp.maximum(part, blk)
        m_new = jnp.maximum(m, jnp.max(part, axis=0, keepdims=True))
        m_safe = jnp.where(m_new == NEG_INF, 0.0, m_new)
        p = jnp.concatenate([jnp.exp2(blocks[h] + (bias[h] - m_safe)) for h in range(n_blocks)], axis=0)
        alpha = jnp.exp2(m - m_safe)
        l_new = alpha * l + jnp.sum(p, axis=0, keepdims=True)
        pb = p.astype(BF16)
        pv = None
        for k in range(SEL_TILES):
            part = _dot(vTsel_ref[0, base + k, :, :], pb[k * KEY_TILE:(k + 1) * KEY_TILE])
            pv = part if pv is None else pv + part
        return m_new, l_new, alpha * acc + pv

    def sel_pair(cp, state):
        c0 = 2 * cp
        sq1_buf[...] = sel_scores(c0 + 1)
        state = sel_chunk(sq0_buf, c0, state, False)
        sq0_buf[...] = sel_scores(c0 + 2)
        return sel_chunk(sq1_buf, c0 + 1, state, False)

    n_pairs = (i // SEL_TILES) // 2
    sq0_buf[...] = sel_scores(0)
    state = lax.fori_loop(0, n_pairs, sel_pair, new_state())
    c_tail = 2 * n_pairs
    sq1_buf[...] = sel_scores(c_tail + 1)
    state = sel_chunk(sq0_buf, c_tail, state, True)
    _, l_s, acc_s = sel_chunk(sq1_buf, c_tail + 1, state, True)
    o_sel = normalise(l_s, acc_s)

    n_win = WIN_LEN // KEY_TILE + 1
    j0 = jnp.maximum(i - (n_win - 1), 0)
    kw = jnp.concatenate([kwin_ref[0, 0, j0 + k] for k in range(n_win)], axis=0)
    dist = tok - (j0 * KEY_TILE + lax.broadcasted_iota(jnp.int32, (n_win * KEY_TILE, 1), 0))
    s = jnp.where((dist >= 0) & (dist < WIN_LEN), _dot(kw, qTm), NEG_INF)
    _, l_w, acc_w = _flash_update(s, [vTwin_ref[0, j0 + k, :, :] for k in range(n_win)], *new_state())
    o_win = normalise(l_w, acc_w)

    def gate(branch):
        g = gT_ref[0]
        return jnp.concatenate([g[branch * GQA_REP + r:branch * GQA_REP + r + 1, :] for r in range(GQA_REP)], axis=1)

    oT = o_cmp * gate(0) + o_sel * gate(1) + o_win * gate(2)
    o_rows = jnp.concatenate([oT[:, r * tq:(r + 1) * tq] for r in range(GQA_REP)], axis=0)
    o_ref[0] = o_rows.T.astype(BF16)


def _overlap_tiles(n_sel, n_cmp_pad):
    cs = np.arange(n_cmp_pad)[None, :] * CMP_STRIDE
    ss = np.arange(n_sel)[:, None] * SEL_LEN
    ov = ((cs < ss + SEL_LEN) & (cs + CMP_LEN > ss)).astype(np.float32)
    ov = ov.reshape(n_sel, n_cmp_pad // KEY_TILE, KEY_TILE).transpose(1, 0, 2)
    return jnp.asarray(ov, BF16)


def _nsa_attention(qT, kc, vcT, ksel, vTsel, kwin, vTwin, gT):
    B, D, S = qT.shape
    G, E = N_KV_HEADS, HEAD_DIM
    n_sel = S // SEL_LEN
    n_tiles = S // KEY_TILE
    n_chunks = kc.shape[2] // KEY_TILE
    kc5 = kc.reshape(B, G, n_chunks, KEY_TILE, E)
    ksel5 = ksel.reshape(B, G, n_tiles, KEY_TILE, E)
    kwin5 = kwin.reshape(B, G, n_tiles, KEY_TILE, E)
    ov = _overlap_tiles(n_sel, n_chunks * KEY_TILE)
    cmp_tiles = 2 if n_chunks % 2 == 0 else 1
    kspec = pl.BlockSpec((1, 1, n_tiles, KEY_TILE, E), lambda b, g, i: (b, g, 0, 0, 0))
    vspec = pl.BlockSpec((1, n_tiles, E, KEY_TILE), lambda b, g, i: (b, 0, g, 0))
    m_lanes = GQA_REP * Q_TILE
    return pl.pallas_call(
        _nsa_attn_kernel,
        out_shape=jax.ShapeDtypeStruct((B, S, D), BF16),
        grid=(B, G, S // Q_TILE),
        in_specs=[
            pl.BlockSpec((1, GQA_REP * E, Q_TILE), lambda b, g, i: (b, g, i)),
            pl.BlockSpec((1, 1, n_chunks, KEY_TILE, E), lambda b, g, i: (b, g, 0, 0, 0)),
            pl.BlockSpec((1, 1, n_chunks, E, KEY_TILE), lambda b, g, i: (b, g, 0, 0, 0)),
            pl.BlockSpec((n_chunks, n_sel, KEY_TILE), lambda b, g, i: (0, 0, 0)),
            kspec, vspec, kspec, vspec,
            pl.BlockSpec((1, 16, Q_TILE), lambda b, g, i: (b, g, i)),
        ],
        out_specs=pl.BlockSpec((1, Q_TILE, GQA_REP * E), lambda b, g, i: (b, i, g)),
        scratch_shapes=[
            pltpu.VMEM((n_chunks // cmp_tiles, cmp_tiles * KEY_TILE, m_lanes), F32),
            pltpu.VMEM((n_sel, m_lanes), F32),
            pltpu.VMEM((n_sel, Q_TILE), F32),
            pltpu.VMEM((SEL_TILES * KEY_TILE, m_lanes), F32),
            pltpu.VMEM((SEL_TILES * KEY_TILE, m_lanes), F32),
        ],
        compiler_params=_cparams(("parallel", "parallel", "arbitrary")),
        name="nsa_attention",
    )(qT, kc5, vcT, ov, ksel5, vTsel, kwin5, vTwin, gT)


def _layer_norm(z, g, b):
    mu = jnp.mean(z, axis=-1, keepdims=True)
    d = z - mu
    var = jnp.mean(d * d, axis=-1, keepdims=True)
    return d * lax.rsqrt(var + LN_EPS) * g + b


def _proj_ln_kernel(o_ref, x_ref, w_ref, gate_ref, g_ref, b_ref, sh_ref, sc_ref, rw_ref,
                    x1_ref, h_ref, lgT_ref):
    y = _dot(o_ref[0], w_ref[...])
    xn = _layer_norm(DN_ALPHA * x_ref[0] + gate_ref[0] * y, g_ref[...], b_ref[...])
    x1_ref[0] = xn
    h = xn * (1.0 + sc_ref[0]) + sh_ref[0]
    h_ref[0] = h
    logits = jnp.dot(h, rw_ref[...], preferred_element_type=F32, precision=HIGHEST)
    lgT_ref[0] = logits.T[:N_EXPERTS]


def _proj_ln(o, x, w_o, gate, ln_g, ln_b, shift2, scale2, router_w):
    B, S, D = x.shape
    tm = ROW_TILE
    rw = jnp.zeros((D, LANES), F32).at[:, :N_EXPERTS].set(router_w)
    vec = pl.BlockSpec((1, 1, D), lambda b, i: (b, 0, 0))
    par = pl.BlockSpec((1, D), lambda b, i: (0, 0))
    row = pl.BlockSpec((1, tm, D), lambda b, i: (b, i, 0))
    return pl.pallas_call(
        _proj_ln_kernel,
        out_shape=[
            jax.ShapeDtypeStruct((B, S, D), F32),
            jax.ShapeDtypeStruct((B, S, D), F32),
            jax.ShapeDtypeStruct((B, N_EXPERTS, S), F32),
        ],
        grid=(B, S // tm),
        in_specs=[row, row, pl.BlockSpec((D, D), lambda b, i: (0, 0)), vec, par, par, vec, vec,
                  pl.BlockSpec((D, LANES), lambda b, i: (0, 0))],
        out_specs=[row, row, pl.BlockSpec((1, N_EXPERTS, tm), lambda b, i: (b, 0, i))],
        compiler_params=_cparams(("parallel", "parallel")),
        name="proj_ln",
    )(o, x, w_o.astype(BF16), gate, ln_g.reshape(1, D), ln_b.reshape(1, D), shift2, scale2, rw)


def _first_max(v, idx, big):
    top = jnp.max(v, axis=0, keepdims=True)
    first = jnp.min(jnp.where(v == top, idx, big), axis=0, keepdims=True)
    return top, first


def _route_kernel(lg_ref, rb_ref, e_ref, w_ref):
    scores = jax.nn.sigmoid(lg_ref[0])
    biased = scores + rb_ref[...]
    eidx = lax.broadcasted_iota(jnp.int32, scores.shape, 0)
    npg = EXPERTS_PER_GROUP
    best_v, best_g = None, None
    for g in range(N_GROUPS):
        v = biased[g * npg:(g + 1) * npg]
        ii = g * npg + lax.broadcasted_iota(jnp.int32, v.shape, 0)
        top1, i1 = _first_max(v, ii, N_EXPERTS)
        top2 = jnp.max(jnp.where(ii == i1, NEG_INF, v), axis=0, keepdims=True)
        gs = top1 + top2
        if g == 0:
            best_v, best_g = gs, jnp.zeros_like(i1)
        else:
            better = gs > best_v
            best_g = jnp.where(better, g, best_g)
            best_v = jnp.where(better, gs, best_v)
    masked = jnp.where(eidx // npg == best_g, biased, NEG_INF)
    _, e1 = _first_max(masked, eidx, N_EXPERTS)
    _, e2 = _first_max(jnp.where(eidx == e1, NEG_INF, masked), eidx, N_EXPERTS)
    sc1 = jnp.sum(jnp.where(eidx == e1, scores, 0.0), axis=0, keepdims=True)
    sc2 = jnp.sum(jnp.where(eidx == e2, scores, 0.0), axis=0, keepdims=True)
    tot = sc1 + sc2
    e_ref[0] = jnp.concatenate([e1, e2], axis=0)
    w_ref[0] = jnp.concatenate([sc1 / tot, sc2 / tot], axis=0)


def _route(lgT, router_b):
    B, E, S = lgT.shape
    tn = min(S, 2048)
    return pl.pallas_call(
        _route_kernel,
        out_shape=[jax.ShapeDtypeStruct((B, TOP_K, S), jnp.int32), jax.ShapeDtypeStruct((B, TOP_K, S), F32)],
        grid=(B, S // tn),
        in_specs=[pl.BlockSpec((1, E, tn), lambda b, i: (b, 0, i)), pl.BlockSpec((E, 1), lambda b, i: (0, 0))],
        out_specs=[pl.BlockSpec((1, TOP_K, tn), lambda b, i: (b, 0, i))] * 2,
        compiler_params=_cparams(("parallel", "parallel")),
        name="moe_route",
    )(lgT, router_b.reshape(E, 1))


def _dispatch_plan(eidx):
    B, K, S = eidx.shape
    n_asg = B * S * K
    e_flat = eidx.transpose(0, 2, 1).reshape(n_asg)
    chunk = LANES
    onehot = (e_flat[:, None] == jnp.arange(N_EXPERTS, dtype=jnp.int32)[None, :]).astype(F32)
    oh = onehot.reshape(n_asg // chunk, chunk, N_EXPERTS)
    tri = jnp.tril(jnp.ones((chunk, chunk), F32))
    within = jnp.einsum("ij,cjk->cik", tri, oh)
    chunk_tot = within[:, -1, :]
    chunk_end = jnp.cumsum(chunk_tot, axis=0)
    incl = within + (chunk_end - chunk_tot)[:, None, :]
    rank = (jnp.sum(incl * oh, axis=-1) - 1.0).reshape(n_asg).astype(jnp.int32)
    counts = chunk_end[-1].astype(jnp.int32)
    padded = (counts + MOE_BLOCK - 1) // MOE_BLOCK * MOE_BLOCK
    pad_ends = jnp.cumsum(padded)
    dest = (pad_ends - padded)[e_flat] + rank
    n_blk = n_asg // MOE_BLOCK + N_EXPERTS
    blk_start = jnp.arange(n_blk, dtype=jnp.int32) * MOE_BLOCK
    blk_exp = jnp.minimum(jnp.sum((pad_ends[None, :] <= blk_start[:, None]).astype(jnp.int32), axis=1),
                          N_EXPERTS - 1)
    return blk_exp, dest


def _moe_dispatch_kernel(n_steps, dest_ref, x_ref, xs_init_ref, xs_hbm, stage, sems):
    del xs_init_ref
    i = pl.program_id(0)
    tm = x_ref.shape[0]
    slot = i & 1

    def drain(s):
        for _ in range(TOP_K):
            pltpu.make_async_copy(stage.at[s], xs_hbm.at[pl.ds(0, tm), :], sems.at[s]).wait()

    @pl.when(i >= 2)
    def _():
        drain(slot)

    stage[slot] = x_ref[...]

    def issue(r, c):
        for k in range(TOP_K):
            pltpu.make_async_copy(stage.at[slot, pl.ds(r, 1), :],
                                  xs_hbm.at[pl.ds(dest_ref[0, 0, TOP_K * r + k], 1), :],
                                  sems.at[slot]).start()
        return c

    lax.fori_loop(0, tm, issue, 0, unroll=8)

    @pl.when(i == n_steps - 1)
    def _():
        drain(slot)
        if n_steps >= 2:
            drain(1 - slot)


def _moe_dispatch(h2d, dest, n_rows):
    n_tok, D = h2d.shape
    tm = MOE_IO_TILE
    n_steps = n_tok // tm
    return pl.pallas_call(
        functools.partial(_moe_dispatch_kernel, n_steps),
        out_shape=jax.ShapeDtypeStruct((n_rows, D), F32),
        grid=(n_steps,),
        in_specs=[
            pl.BlockSpec((1, 1, TOP_K * tm), lambda i: (i, 0, 0), memory_space=pltpu.SMEM),
            pl.BlockSpec((tm, D), lambda i: (i, 0)),
            pl.BlockSpec(memory_space=pl.ANY),
        ],
        out_specs=pl.BlockSpec(memory_space=pl.ANY),
        scratch_shapes=[pltpu.VMEM((2, tm, D), F32), pltpu.SemaphoreType.DMA((2,))],
        input_output_aliases={2: 0},
        compiler_params=_cparams(("arbitrary",)),
        name="moe_dispatch",
    )(dest.reshape(n_steps, 1, TOP_K * tm), h2d, jnp.zeros((n_rows, D), F32))


def _moe_expert_kernel(blk_exp_ref, x_ref, wg_ref, wu_ref, wd_ref, y_ref):
    del blk_exp_ref
    x = x_ref[...].astype(BF16)
    gate = _dot(x, wg_ref[0])
    up = _dot(x, wu_ref[0])
    hid = (gate * jax.nn.sigmoid(gate) * up).astype(BF16)
    y_ref[...] = _dot(hid, wd_ref[0])


def _moe_experts(xs, blk_exp, w_gate, w_up, w_down):
    n_rows, D = xs.shape
    F = w_gate.shape[2]
    rows = pl.BlockSpec((MOE_BLOCK, D), lambda i, be: (i, 0))
    grid_spec = pltpu.PrefetchScalarGridSpec(
        num_scalar_prefetch=1,
        grid=(n_rows // MOE_BLOCK,),
        in_specs=[
            rows,
            pl.BlockSpec((1, D, F), lambda i, be: (be[i], 0, 0)),
            pl.BlockSpec((1, D, F), lambda i, be: (be[i], 0, 0)),
            pl.BlockSpec((1, F, D), lambda i, be: (be[i], 0, 0)),
        ],
        out_specs=rows,
    )
    return pl.pallas_call(
        _moe_expert_kernel,
        out_shape=jax.ShapeDtypeStruct((n_rows, D), F32),
        grid_spec=grid_spec,
        compiler_params=_cparams(("arbitrary",)),
        name="moe_experts",
    )(blk_exp, xs, w_gate.astype(BF16), w_up.astype(BF16), w_down.astype(BF16))


def _moe_combine_kernel(n_steps, dcur_ref, dnext_ref, w_ref, x_ref, gate_ref, g_ref, b_ref, ys_hbm,
                        o_ref, gbuf, sems):
    i = pl.program_id(0)
    tm = x_ref.shape[0]
    slot = i & 1

    def issue(d_ref, s):
        def body(r, c):
            for k in range(TOP_K):
                pltpu.make_async_copy(ys_hbm.at[pl.ds(d_ref[0, 0, TOP_K * r + k], 1), :],
                                      gbuf.at[s, k, pl.ds(r, 1), :], sems.at[s]).start()
            return c
        lax.fori_loop(0, tm, body, 0, unroll=8)

    @pl.when(i == 0)
    def _():
        issue(dcur_ref, slot)

    @pl.when(i + 1 < n_steps)
    def _():
        issue(dnext_ref, 1 - slot)

    for k in range(TOP_K):
        pltpu.make_async_copy(ys_hbm.at[pl.ds(0, tm), :], gbuf.at[slot, k], sems.at[slot]).wait()
    w = w_ref[...]
    y = gbuf[slot, 0] * w[:, 0:1] + gbuf[slot, 1] * w[:, 1:2]
    o_ref[...] = _layer_norm(DN_ALPHA * x_ref[...] + gate_ref[0] * y, g_ref[...], b_ref[...])


def _moe_combine_ln(ys, dest, wts, x, gate, ln_g, ln_b):
    B, S, D = x.shape
    n_tok = B * S
    tm = MOE_IO_TILE
    n_steps = n_tok // tm
    per_b = S // tm
    d3 = dest.reshape(n_steps, 1, TOP_K * tm)
    w2 = wts.transpose(0, 2, 1).reshape(n_tok, TOP_K)
    idx = lambda f: pl.BlockSpec((1, 1, TOP_K * tm), f, memory_space=pltpu.SMEM)
    par = pl.BlockSpec((1, D), lambda i: (0, 0))
    row = pl.BlockSpec((tm, D), lambda i: (i, 0))
    out = pl.pallas_call(
        functools.partial(_moe_combine_kernel, n_steps),
        out_shape=jax.ShapeDtypeStruct((n_tok, D), F32),
        grid=(n_steps,),
        in_specs=[
            idx(lambda i: (i, 0, 0)),
            idx(lambda i: (jnp.minimum(i + 1, n_steps - 1), 0, 0)),
            pl.BlockSpec((tm, TOP_K), lambda i: (i, 0)),
            row,
            pl.BlockSpec((1, 1, D), lambda i: (i // per_b, 0, 0)),
            par, par,
            pl.BlockSpec(memory_space=pl.ANY),
        ],
        out_specs=row,
        scratch_shapes=[pltpu.VMEM((2, TOP_K, tm, D), F32), pltpu.SemaphoreType.DMA((2,))],
        compiler_params=_cparams(("arbitrary",)),
        name="moe_combine_ln",
    )(d3, d3, w2, x.reshape(n_tok, D), gate, ln_g.reshape(1, D), ln_b.reshape(1, D), ys)
    return out.reshape(B, S, D)


def _moe_sublayer(x1, h2, lgT, router_b, w_gate, w_up, w_down, gate, ln_g, ln_b):
    B, S, D = x1.shape
    eidx, wts = _route(lgT, router_b)
    blk_exp, dest = _dispatch_plan(eidx)
    n_rows = blk_exp.shape[0] * MOE_BLOCK
    xs = _moe_dispatch(h2.reshape(B * S, D), dest, n_rows)
    ys = _moe_experts(xs, blk_exp, w_gate, w_up, w_down)
    return _moe_combine_ln(ys, dest, wts, x1, gate, ln_g, ln_b)


def _dil_weight_cols():
    d = D_MODEL
    cols = list(range(d))
    for which in range(2):
        for p in range(len(DIL_PATTERNS)):
            base = d + (p * 2 + which) * KV_COLS
            for g in range(N_KV_HEADS):
                head = list(range(base + g * HEAD_DIM, base + (g + 1) * HEAD_DIM))
                cols += head + head
    return np.asarray(cols)


def _dil_inproj_kernel(x_ref, sh_ref, sc_ref, w_ref, c_ref, s1_ref, s2_ref, q_ref, *kv_refs):
    h = (x_ref[0] * (1.0 + sc_ref[0]) + sh_ref[0]).astype(BF16)
    c, s1, s2 = c_ref[0], s1_ref[0], s2_ref[0]
    d = q_ref.shape[2]
    w = 2 * KV_COLS
    n_pat = len(DIL_PATTERNS)
    for j in range(d // w):
        a = _rope_cols(_dot(h, w_ref[:, j * w:(j + 1) * w]), c, s1, s2) * Q_SCALE
        q_ref[0, :, j * w:(j + 1) * w] = a.astype(BF16)
    for p in range(n_pat):
        a = _rope_cols(_dot(h, w_ref[:, d + p * w:d + (p + 1) * w]), c, s1, s2)
        kv_refs[p][0] = a.astype(BF16)
    for p in range(n_pat):
        a = _dot(h, w_ref[:, d + (n_pat + p) * w:d + (n_pat + p + 1) * w])
        kv_refs[n_pat + p][0] = a.astype(BF16)


def _dil_inproj(x, shift, scale, w_in, tabs):
    B, S, D = x.shape
    tm = ROW_TILE
    wp = _permute_cols(w_in, _dil_weight_cols())
    ncol = wp.shape[1]
    n_pat = len(DIL_PATTERNS)
    vec = pl.BlockSpec((1, 1, D), lambda b, i: (b, 0, 0))
    tab = pl.BlockSpec((1, tm, LANES), lambda b, i: (b, i, 0))
    kvspec = pl.BlockSpec((1, tm, 2 * KV_COLS), lambda b, i: (b, i, 0))
    outs = pl.pallas_call(
        _dil_inproj_kernel,
        out_shape=[jax.ShapeDtypeStruct((B, S, D), BF16)]
        + [jax.ShapeDtypeStruct((B, S, 2 * KV_COLS), BF16)] * (2 * n_pat),
        grid=(B, S // tm),
        in_specs=[pl.BlockSpec((1, tm, D), lambda b, i: (b, i, 0)), vec, vec,
                  pl.BlockSpec((D, ncol), lambda b, i: (0, 0)), tab, tab, tab],
        out_specs=[pl.BlockSpec((1, tm, D), lambda b, i: (b, i, 0))] + [kvspec] * (2 * n_pat),
        compiler_params=_cparams(("parallel", "parallel")),
        name="dil_inproj",
    )(x, shift, scale, wp, *tabs)
    return outs[0], outs[1:1 + n_pat], outs[1 + n_pat:]


def _dil_attn_kernel(steps, first, last, *refs):
    if first:
        q_ref, kc_ref, kp_ref, vc_ref, vp_ref = refs[:5]
        acc_in = ml_in = None
        outs = refs[5:]
    else:
        q_ref, kc_ref, kp_ref, vc_ref, vp_ref, acc_in, ml_in = refs[:7]
        outs = refs[7:]
    nb = pl.program_id(2)
    blk = DIL_BLOCK
    qi = lax.broadcasted_iota(jnp.int32, (blk, 2 * blk), 0)
    kj = lax.broadcasted_iota(jnp.int32, (blk, 2 * blk), 1)
    dist = blk + qi - kj
    valid = (dist >= 0) & (dist <= steps) & ((nb - 1) * blk + kj >= 0)
    lane = lax.broadcasted_iota(jnp.int32, (1, LANES), 1)
    lo = lane < HEAD_DIM
    lo_b = lo.astype(BF16)
    hi_b = 1.0 - lo_b
    lane_ml = lax.broadcasted_iota(jnp.int32, (blk, LANES), 1)
    ml_old = None if first else ml_in[0]
    ml_new = jnp.zeros((blk, LANES), F32)

    for g in range(N_KV_HEADS):
        seg = slice(g * LANES, (g + 1) * LANES)
        kcat = jnp.concatenate([kp_ref[0][:, seg], kc_ref[0][:, seg]], axis=0)
        vcat = jnp.concatenate([vp_ref[0][:, seg], vc_ref[0][:, seg]], axis=0)
        k_half = (kcat * lo_b, kcat * hi_b)
        v_half = (vcat * lo_b, vcat * hi_b)
        for pair in range(GQA_REP // 2):
            chunk = g * (GQA_REP // 2) + pair
            cs = slice(chunk * LANES, (chunk + 1) * LANES)
            q2 = q_ref[0][:, cs]
            pv_sum = None
            alpha_side, l_side = [], []
            for side in range(2):
                head = 2 * chunk + side
                s = jnp.where(valid, _dot_nt(q2, k_half[side]), NEG_INF)
                m_new = jnp.max(s, axis=1, keepdims=True)
                if not first:
                    m_old = ml_old[:, head:head + 1]
                    m_new = jnp.maximum(m_old, m_new)
                    alpha_side.append(jnp.exp2(m_old - m_new))
                p = jnp.exp2(s - m_new)
                l_new = jnp.sum(p, axis=1, keepdims=True)
                if not first:
                    l_new = l_new + alpha_side[side] * ml_old[:, N_Q_HEADS + head:N_Q_HEADS + head + 1]
                l_side.append(l_new)
                pv = _dot(p.astype(BF16), v_half[side])
                pv_sum = pv if pv_sum is None else pv_sum + pv
                ml_new = jnp.where(lane_ml == head, m_new, ml_new)
                ml_new = jnp.where(lane_ml == N_Q_HEADS + head, l_new, ml_new)
            acc = pv_sum
            if not first:
                acc = acc_in[0][:, cs] * jnp.where(lo, alpha_side[0], alpha_side[1]) + pv_sum
            if last:
                outs[0][0, :, cs] = (acc / jnp.where(lo, l_side[0], l_side[1])).astype(BF16)
            else:
                outs[0][0, :, cs] = acc
    if not last:
        outs[1][0] = ml_new


def _dil_attention(q, kds, vds):
    B, S, D = q.shape
    n_pat = len(DIL_PATTERNS)
    acc = ml = None
    for p, (window, dil) in enumerate(DIL_PATTERNS):
        first, last = p == 0, p == n_pat - 1
        L = S // dil
        nblk = L // DIL_BLOCK
        kw = 2 * KV_COLS

        def view(a):
            return a.reshape(B, L, dil * a.shape[2])

        cur = lambda b, c, n: (b, n, c)
        prev = lambda b, c, n: (b, jnp.maximum(n - 1, 0), c)
        qspec = pl.BlockSpec((1, DIL_BLOCK, D), cur)
        in_specs = [qspec, pl.BlockSpec((1, DIL_BLOCK, kw), cur), pl.BlockSpec((1, DIL_BLOCK, kw), prev),
                    pl.BlockSpec((1, DIL_BLOCK, kw), cur), pl.BlockSpec((1, DIL_BLOCK, kw), prev)]
        args = [view(q), view(kds[p]), view(kds[p]), view(vds[p]), view(vds[p])]
        if not first:
            in_specs += [qspec, pl.BlockSpec((1, DIL_BLOCK, LANES), cur)]
            args += [view(acc), view(ml)]
        if last:
            out_shape = [jax.ShapeDtypeStruct((B, L, dil * D), BF16)]
            out_specs = [qspec]
        else:
            out_shape = [jax.ShapeDtypeStruct((B, L, dil * D), F32),
                         jax.ShapeDtypeStruct((B, L, dil * LANES), F32)]
            out_specs = [qspec, pl.BlockSpec((1, DIL_BLOCK, LANES), cur)]
        res = pl.pallas_call(
            functools.partial(_dil_attn_kernel, window // dil, first, last),
            out_shape=out_shape,
            grid=(B, dil, nblk),
            in_specs=in_specs,
            out_specs=out_specs,
            compiler_params=_cparams(("parallel", "parallel", "arbitrary")),
            name=f"dil_attention_{p}",
        )(*args)
        if last:
            return res[0].reshape(B, S, D)
        acc, ml = res[0].reshape(B, S, D), res[1].reshape(B, S, LANES)


def kernel(x, c, positions, ada_w, ada_b, ln_g, ln_b, nsa_w_in, nsa_cmp_pos_k, nsa_cmp_w1_k, nsa_cmp_w2_k, nsa_cmp_pos_v, nsa_cmp_w1_v, nsa_cmp_w2_v, nsa_w_o, dil_w_in, dil_w_o, router_w, router_b, moe_w_gate, moe_w_up, moe_w_down):
    B, S, D = x.shape
    mods = _ada_mods(c, ada_w, ada_b)
    def mod(i, sub):
        m = mods[i * 2 + sub]
        return [m[:, k * D:(k + 1) * D].reshape(B, 1, D) for k in range(3)]
    tabs = _rope_tables(positions)

    for i in range(DEPTH):
        shift, scale, gate = mod(i, 0)
        shift2, scale2, gate2 = mod(i, 1)
        j = i // 2
        if i % 2 == 0:
            qT, ksel, kwin, vTsel, vTwin, kcmp, vcmp, gT = _nsa_inproj(x, shift, scale, nsa_w_in[j], tabs)
            kc, vcT = _compress(kcmp, vcmp, nsa_cmp_pos_k[j], nsa_cmp_w1_k[j], nsa_cmp_w2_k[j],
                                nsa_cmp_pos_v[j], nsa_cmp_w1_v[j], nsa_cmp_w2_v[j], tabs)
            o = _nsa_attention(qT, kc, vcT, ksel, vTsel, kwin, vTwin, gT)
            w_o = nsa_w_o[j]
        else:
            q, kds, vds = _dil_inproj(x, shift, scale, dil_w_in[j], tabs)
            o = _dil_attention(q, kds, vds)
            w_o = dil_w_o[j]
        x1, h2, lgT = _proj_ln(o, x, w_o, gate, ln_g[i, 0], ln_b[i, 0], shift2, scale2, router_w)
        x = _moe_sublayer(x1, h2, lgT, router_b, moe_w_gate[i], moe_w_up[i], moe_w_down[i],
                          gate2, ln_g[i, 1], ln_b[i, 1])
    return x
```

```python
import functools

import numpy as np
import jax
import jax.numpy as jnp
from jax import lax
from jax.experimental import pallas as pl
from jax.experimental.pallas import tpu as pltpu

F32 = jnp.float32
BF16 = jnp.bfloat16
HIGHEST = lax.Precision.HIGHEST
NEG_INF = float("-inf")

D_MODEL = 1024
DEPTH = 2
HEAD_DIM = 64
N_Q_HEADS = D_MODEL // HEAD_DIM
N_KV_HEADS = 4
GQA_REP = N_Q_HEADS // N_KV_HEADS
ROPE_DIM = HEAD_DIM // 4
ROPE_THETA = 500000.0
ATTN_SCALE = HEAD_DIM ** -0.5
LOG2_E = 1.4426950408889634
Q_SCALE = ATTN_SCALE * LOG2_E
KV_COLS = N_KV_HEADS * HEAD_DIM
N_BRANCH = 3
CMP_LEN = 32
CMP_STRIDE = 16
CMP_HIDDEN = 256
SEL_LEN = 64
N_SELECT = 16
WIN_LEN = 512
FORCE_SCORE = 1.0e4
DIL_PATTERNS = ((128, 1), (512, 4), (2048, 16))
DIL_BLOCK = 128
N_EXPERTS = 32
N_GROUPS = 4
EXPERTS_PER_GROUP = N_EXPERTS // N_GROUPS
TOP_K = 2
D_EXPERT = 512
MOE_BLOCK = 128
DN_ALPHA = (2.0 * DEPTH) ** 0.25
LN_EPS = 1e-5

LANES = 128
VMEM_LIMIT_BYTES = 48 * 1024 * 1024

Q_TILE = 128
KEY_TILE = 128
SEL_TILES = 4
ROW_TILE = 512
MOE_IO_TILE = 256


def _cparams(semantics):
    return pltpu.CompilerParams(dimension_semantics=semantics, vmem_limit_bytes=VMEM_LIMIT_BYTES)


def _dot(a, b):
    return jnp.dot(a, b, preferred_element_type=F32)


def _dot_nt(a, b):
    return lax.dot_general(a, b, (((1,), (1,)), ((), ())), preferred_element_type=F32)


def _ada_kernel(c_ref, w_ref, b_ref, o_ref):
    c = c_ref[...]
    cond = c * jax.nn.sigmoid(c)
    o_ref[0] = jnp.dot(cond, w_ref[0], preferred_element_type=F32, precision=HIGHEST) + b_ref[0]


def _ada_mods(c, ada_w, ada_b):
    B, D = c.shape
    n_sub = ada_w.shape[0] * ada_w.shape[1]
    w = ada_w.reshape(n_sub, D, 3 * D)
    b = ada_b.reshape(n_sub, 1, 3 * D)
    c8 = jnp.zeros((8, D), F32).at[:B].set(c)
    tn = 768
    out = pl.pallas_call(
        _ada_kernel,
        out_shape=jax.ShapeDtypeStruct((n_sub, 8, 3 * D), F32),
        grid=(n_sub, 3 * D // tn),
        in_specs=[
            pl.BlockSpec((8, D), lambda s, j: (0, 0)),
            pl.BlockSpec((1, D, tn), lambda s, j: (s, 0, j)),
            pl.BlockSpec((1, 1, tn), lambda s, j: (s, 0, j)),
        ],
        out_specs=pl.BlockSpec((1, 8, tn), lambda s, j: (s, 0, j)),
        compiler_params=_cparams(("parallel", "parallel")),
        name="ada_mods",
    )(c8, w, b)
    return out[:, :B]


def _rope_tab_kernel(pos_ref, inv_ref, sg1_ref, sg2_ref, c_ref, s1_ref, s2_ref):
    ang = pos_ref[0] * inv_ref[...]
    sin = jnp.sin(ang)
    c_ref[0] = jnp.cos(ang)
    s1_ref[0] = sin * sg1_ref[...]
    s2_ref[0] = sin * sg2_ref[...]


def _rope_tables(positions):
    B, S = positions.shape
    half = ROPE_DIM // 2
    inv = ROPE_THETA ** (-jnp.arange(half, dtype=F32) * (2.0 / ROPE_DIM))
    li = np.arange(LANES) % HEAD_DIM
    in_rope = li < ROPE_DIM
    inv_row = jnp.where(jnp.asarray(in_rope), inv[li % half], 0.0).reshape(1, LANES)
    sg1 = jnp.asarray(np.where(li < half, -1.0, 0.0), F32).reshape(1, LANES)
    sg2 = jnp.asarray(np.where((li >= half) & in_rope, 1.0, 0.0), F32).reshape(1, LANES)
    pos = positions.astype(F32).reshape(B, S, 1)
    tm = min(S, 2048)
    row = pl.BlockSpec((1, LANES), lambda b, i: (0, 0))
    tab = pl.BlockSpec((1, tm, LANES), lambda b, i: (b, i, 0))
    return pl.pallas_call(
        _rope_tab_kernel,
        out_shape=[jax.ShapeDtypeStruct((B, S, LANES), F32)] * 3,
        grid=(B, S // tm),
        in_specs=[pl.BlockSpec((1, tm, 1), lambda b, i: (b, i, 0)), row, row, row],
        out_specs=[tab, tab, tab],
        compiler_params=_cparams(("parallel", "parallel")),
        name="rope_tables",
    )(pos, inv_row, sg1, sg2)


def _rope128(t, c, s1, s2):
    return t * c + pltpu.roll(t, LANES - ROPE_DIM // 2, 1) * s1 + pltpu.roll(t, ROPE_DIM // 2, 1) * s2


def _rope_cols(a, c, s1, s2):
    n = a.shape[1] // LANES
    return jnp.concatenate(
        [_rope128(a[:, k * LANES:(k + 1) * LANES], c, s1, s2) for k in range(n)], axis=1)


def _nsa_weight_cols():
    d = D_MODEL
    def kv(branch, which):
        base = d + (branch * 2 + which) * KV_COLS
        return list(range(base, base + KV_COLS))
    cols = list(range(d))
    cols += kv(1, 0) + kv(2, 0) + kv(1, 1) + kv(2, 1) + kv(0, 0) + kv(0, 1)
    gate0 = d + N_BRANCH * 2 * KV_COLS
    gcols = [-1] * LANES
    for g in range(N_KV_HEADS):
        for br in range(N_BRANCH):
            for r in range(GQA_REP):
                gcols[g * 16 + br * GQA_REP + r] = gate0 + (g * GQA_REP + r) * N_BRANCH + br
    return np.asarray(cols + gcols)


def _permute_cols(w, cols):
    picked = w[:, np.maximum(cols, 0)]
    return jnp.where(jnp.asarray(cols >= 0)[None, :], picked, 0.0).astype(BF16)


def _nsa_inproj_kernel(x_ref, sh_ref, sc_ref, w_ref, c_ref, s1_ref, s2_ref,
                       qT_ref, ksel_ref, kwin_ref, vTsel_ref, vTwin_ref, kcmp_ref, vcmp_ref, gT_ref):
    tm = x_ref.shape[1]
    h = (x_ref[0] * (1.0 + sc_ref[0]) + sh_ref[0]).astype(BF16)
    c, s1, s2 = c_ref[0], s1_ref[0], s2_ref[0]
    w = KV_COLS

    def proj(j, n=w):
        return _dot(h, w_ref[:, j * w:j * w + n])

    for j in range(4):
        a = _rope_cols(proj(j), c, s1, s2) * Q_SCALE
        qT_ref[0, j * w:(j + 1) * w, :] = a.T.astype(BF16)
    for j, ref in ((4, ksel_ref), (5, kwin_ref)):
        a = _rope_cols(proj(j), c, s1, s2)
        for g in range(N_KV_HEADS):
            ref[0, g] = a[:, g * HEAD_DIM:(g + 1) * HEAD_DIM].astype(BF16)
    for j, ref in ((6, vTsel_ref), (7, vTwin_ref)):
        aT = proj(j).T.astype(BF16)
        for k in range(tm // KEY_TILE):
            ref[0, k] = aT[:, k * KEY_TILE:(k + 1) * KEY_TILE]
    for j, ref in ((8, kcmp_ref), (9, vcmp_ref)):
        a = proj(j)
        for g in range(N_KV_HEADS):
            ref[0, g] = a[:, g * HEAD_DIM:(g + 1) * HEAD_DIM].astype(BF16)
    gates = jax.nn.sigmoid(proj(10, LANES))
    gT_ref[0] = gates.T[:4 * 16]


def _nsa_inproj(x, shift, scale, w_in, tabs):
    B, S, D = x.shape
    tm = ROW_TILE
    wp = _permute_cols(w_in, _nsa_weight_cols())
    ncol = wp.shape[1]
    vec = pl.BlockSpec((1, 1, D), lambda b, i: (b, 0, 0))
    tab = pl.BlockSpec((1, tm, LANES), lambda b, i: (b, i, 0))
    nat = pl.BlockSpec((1, N_KV_HEADS, tm, HEAD_DIM), lambda b, i: (b, 0, i, 0))
    vt = pl.BlockSpec((1, tm // KEY_TILE, KV_COLS, KEY_TILE), lambda b, i: (b, i, 0, 0))
    nat_shape = jax.ShapeDtypeStruct((B, N_KV_HEADS, S, HEAD_DIM), BF16)
    vt_shape = jax.ShapeDtypeStruct((B, S // KEY_TILE, KV_COLS, KEY_TILE), BF16)
    return pl.pallas_call(
        _nsa_inproj_kernel,
        out_shape=[
            jax.ShapeDtypeStruct((B, D, S), BF16),
            nat_shape, nat_shape,
            vt_shape, vt_shape,
            nat_shape, nat_shape,
            jax.ShapeDtypeStruct((B, 4 * 16, S), F32),
        ],
        grid=(B, S // tm),
        in_specs=[
            pl.BlockSpec((1, tm, D), lambda b, i: (b, i, 0)), vec, vec,
            pl.BlockSpec((D, ncol), lambda b, i: (0, 0)), tab, tab, tab,
        ],
        out_specs=[
            pl.BlockSpec((1, D, tm), lambda b, i: (b, 0, i)),
            nat, nat, vt, vt, nat, nat,
            pl.BlockSpec((1, 4 * 16, tm), lambda b, i: (b, 0, i)),
        ],
        compiler_params=_cparams(("parallel", "parallel")),
        name="nsa_inproj",
    )(x, shift, scale, wp, *tabs)


def _compress_kernel(xk_ref, xv_ref, w1k_ref, w1v_ref, pk_ref, pv_ref, w2k_ref, w2vT_ref,
                     c_ref, s1_ref, s2_ref, kc_ref, vcT_ref):
    n = xk_ref.shape[2]
    half = w1k_ref.shape[0] // 2

    def hidden(x_ref, w1_ref, p_ref):
        x = x_ref[0, 0]
        first = _dot(x, w1_ref[:half])
        second = _dot(x, w1_ref[half:])
        bias = _dot(p_ref[...], w1_ref[...])[0:1]
        hid = first + pltpu.roll(second, n - 1, 0) + bias
        return jax.nn.gelu(hid).astype(BF16)

    kc = _dot(hidden(xk_ref, w1k_ref, pk_ref), w2k_ref[...])
    kc = _rope128(kc, c_ref[0], s1_ref[0], s2_ref[0])
    row = lax.broadcasted_iota(jnp.int32, kc.shape, 0)
    kc = jnp.where(row < n - 1, kc, 0.0)
    kc_ref[0, 0] = kc[:, :HEAD_DIM].astype(BF16)

    vcT = _dot_nt(w2vT_ref[...], hidden(xv_ref, w1v_ref, pv_ref))
    col = lax.broadcasted_iota(jnp.int32, vcT.shape, 1)
    vcT = jnp.where(col < n - 1, vcT, 0.0).astype(BF16)
    for k in range(n // KEY_TILE):
        vcT_ref[0, 0, k] = vcT[:, k * KEY_TILE:(k + 1) * KEY_TILE]


def _compress(kcmp, vcmp, pos_k, w1_k, w2_k, pos_v, w1_v, w2_v, tabs):
    B, G, S, E = kcmp.shape
    n = S // CMP_STRIDE
    wide = CMP_STRIDE * E
    xk = kcmp.reshape(B, G, n, wide)
    xv = vcmp.reshape(B, G, n, wide)
    def flat8(p):
        return jnp.zeros((8, CMP_LEN * E), BF16).at[0].set(p.reshape(-1).astype(BF16))
    w2k = jnp.zeros((CMP_HIDDEN, LANES), BF16).at[:, :E].set(w2_k.astype(BF16))
    w2vT = w2_v.T.astype(BF16)
    last = CMP_LEN - 1
    ctabs = [jnp.zeros((B, n, LANES), F32).at[:, :n - 1].set(t[:, last::CMP_STRIDE][:, :n - 1]) for t in tabs]
    xspec = pl.BlockSpec((1, 1, n, wide), lambda b, g: (b, g, 0, 0))
    w1spec = pl.BlockSpec((CMP_LEN * E, CMP_HIDDEN), lambda b, g: (0, 0))
    pspec = pl.BlockSpec((8, CMP_LEN * E), lambda b, g: (0, 0))
    tspec = pl.BlockSpec((1, n, LANES), lambda b, g: (b, 0, 0))
    return pl.pallas_call(
        _compress_kernel,
        out_shape=[
            jax.ShapeDtypeStruct((B, G, n, E), BF16),
            jax.ShapeDtypeStruct((B, G, n // KEY_TILE, E, KEY_TILE), BF16),
        ],
        grid=(B, G),
        in_specs=[xspec, xspec, w1spec, w1spec, pspec, pspec,
                  pl.BlockSpec((CMP_HIDDEN, LANES), lambda b, g: (0, 0)),
                  pl.BlockSpec((E, CMP_HIDDEN), lambda b, g: (0, 0)),
                  tspec, tspec, tspec],
        out_specs=[
            pl.BlockSpec((1, 1, n, E), lambda b, g: (b, g, 0, 0)),
            pl.BlockSpec((1, 1, n // KEY_TILE, E, KEY_TILE), lambda b, g: (b, g, 0, 0, 0)),
        ],
        compiler_params=_cparams(("parallel", "parallel")),
        name="nsa_compress",
    )(xk, xv, w1_k.astype(BF16), w1_v.astype(BF16), flat8(pos_k), flat8(pos_v), w2k, w2vT, *ctabs)


def _flash_update(s, vT_tiles, m, l, acc):
    m_new = jnp.maximum(m, jnp.max(s, axis=0, keepdims=True))
    m_safe = jnp.where(m_new == NEG_INF, 0.0, m_new)
    p = jnp.exp2(s - m_safe)
    alpha = jnp.exp2(m - m_safe)
    l_new = alpha * l + jnp.sum(p, axis=0, keepdims=True)
    pb = p.astype(BF16)
    pv = None
    for k, vT in enumerate(vT_tiles):
        part = _dot(vT, pb[k * KEY_TILE:(k + 1) * KEY_TILE])
        pv = part if pv is None else pv + part
    return m_new, l_new, alpha * acc + pv


def _nsa_attn_kernel(qT_ref, kc_ref, vcT_ref, ov_ref, ksel_ref, vTsel_ref, kwin_ref, vTwin_ref, gT_ref,
                     o_ref, s_buf, imp_buf, sel_buf, sq0_buf, sq1_buf):
    i = pl.program_id(2)
    tq = Q_TILE
    m_lanes = GQA_REP * tq
    e = HEAD_DIM
    t0 = i * tq
    n_chunks = kc_ref.shape[2]
    n_sel = sel_buf.shape[0]

    qT = qT_ref[0]
    qTm = jnp.concatenate([qT[r * e:(r + 1) * e] for r in range(GQA_REP)], axis=1)
    lane = lax.broadcasted_iota(jnp.int32, (1, m_lanes), 1)
    tok = t0 + (lane & (tq - 1))
    row_k = lax.broadcasted_iota(jnp.int32, (KEY_TILE, 1), 0)

    def new_state():
        return (jnp.full((1, m_lanes), NEG_INF, F32), jnp.zeros((1, m_lanes), F32),
                jnp.zeros((e, m_lanes), F32))

    def normalise(l, acc):
        return acc / jnp.maximum(l, 1e-30)

    cmp_tiles = s_buf.shape[1] // KEY_TILE
    cmp_rows = cmp_tiles * KEY_TILE
    n_vis = jnp.minimum((i // (KEY_TILE // 8)) // cmp_tiles + 1, n_chunks // cmp_tiles)
    row_c = lax.broadcasted_iota(jnp.int32, (cmp_rows, 1), 0)

    def cmp_scores(c, m):
        kc = jnp.concatenate([kc_ref[0, 0, c * cmp_tiles + k] for k in range(cmp_tiles)], axis=0)
        last_tok = (c * cmp_rows + row_c) * CMP_STRIDE + (CMP_LEN - 1)
        s = jnp.where(last_tok <= tok, _dot(kc, qTm), NEG_INF)
        s_buf[c] = s
        return jnp.maximum(m, jnp.max(s, axis=0, keepdims=True))

    m_c = lax.fori_loop(0, n_vis, cmp_scores, jnp.full((1, m_lanes), NEG_INF, F32))
    m_c = jnp.where(m_c == NEG_INF, 0.0, m_c)

    imp_buf[...] = jnp.zeros(imp_buf.shape, F32)

    def cmp_accum(c, carry):
        l, acc = carry
        p = jnp.exp2(s_buf[c] - m_c)
        pb = p.astype(BF16)
        imp = imp_buf[...]
        for k in range(cmp_tiles):
            pk = pb[k * KEY_TILE:(k + 1) * KEY_TILE]
            imp = imp + _dot(ov_ref[c * cmp_tiles + k], pk)
            acc = acc + _dot(vcT_ref[0, 0, c * cmp_tiles + k], pk)
        imp_buf[...] = imp
        return l + jnp.sum(p, axis=0, keepdims=True), acc

    l_c, acc_c = lax.fori_loop(
        0, n_vis, cmp_accum, (jnp.zeros((1, m_lanes), F32), jnp.zeros((e, m_lanes), F32)))
    inv_l = 1.0 / jnp.maximum(l_c, 1e-30)
    o_cmp = acc_c * inv_l
    imp_n = imp_buf[...] * inv_l
    imp = imp_n[:, 0:tq]
    for r in range(1, GQA_REP):
        imp = imp + imp_n[:, r * tq:(r + 1) * tq]

    sidx = lax.broadcasted_iota(jnp.int32, (n_sel, tq), 0)
    cur = (t0 + lax.broadcasted_iota(jnp.int32, (1, tq), 1)) // SEL_LEN
    forced = (sidx == 0) | (sidx == cur) | (sidx == cur - 1)
    vals = jnp.where(forced, FORCE_SCORE, imp)
    vals = jnp.where(sidx <= cur, vals, NEG_INF)

    def pick(_, rest):
        top = jnp.max(rest, axis=0, keepdims=True)
        first = jnp.min(jnp.where(rest == top, sidx, n_sel), axis=0, keepdims=True)
        return jnp.where(sidx == first, NEG_INF, rest)

    rest = lax.fori_loop(0, min(N_SELECT, n_sel), pick, vals)
    sel_buf[...] = jnp.where(rest < vals, 0.0, NEG_INF)

    blocks_per_tile = KEY_TILE // SEL_LEN

    def sel_scores(c):
        base = c * SEL_TILES
        ks = jnp.concatenate([ksel_ref[0, 0, base + k] for k in range(SEL_TILES)], axis=0)
        return _dot(ks, qTm)

    def sel_chunk(sq_ref, c, state, causal):
        m, l, acc = state
        base = c * SEL_TILES
        n_blocks = SEL_TILES * blocks_per_tile
        s = sq_ref[...]
        if causal:
            key = base * KEY_TILE + lax.broadcasted_iota(jnp.int32, (SEL_TILES * KEY_TILE, 1), 0)
            s = jnp.where(key <= tok, s, NEG_INF)
        blocks = [s[h * SEL_LEN:(h + 1) * SEL_LEN] for h in range(n_blocks)]
        bias = [jnp.concatenate([sel_buf[pl.ds(base * blocks_per_tile + h, 1), :]] * GQA_REP, axis=1)
                for h in range(n_blocks)]
        part = None
        for h in range(n_blocks):
            blk = jnp.max(blocks[h].reshape(SEL_LEN // 8, 8, m_lanes), axis=0) + bias[h]
            part = blk if part is None else jnp.maximum(part, blk)
        m_new = jnp.maximum(m, jnp.max(part, axis=0, keepdims=True))
        m_safe = jnp.where(m_new == NEG_INF, 0.0, m_new)
        p = jnp.concatenate([jnp.exp2(blocks[h] + (bias[h] - m_safe)) for h in range(n_blocks)], axis=0)
        alpha = jnp.exp2(m - m_safe)
        l_new = alpha * l + jnp.sum(p, axis=0, keepdims=True)
        pb = p.astype(BF16)
        pv = None
        for k in range(SEL_TILES):
            part = _dot(vTsel_ref[0, base + k, :, :], pb[k * KEY_TILE:(k + 1) * KEY_TILE])
            pv = part if pv is None else pv + part
        return m_new, l_new, alpha * acc + pv

    def sel_pair(cp, state):
        c0 = 2 * cp
        sq1_buf[...] = sel_scores(c0 + 1)
        state = sel_chunk(sq0_buf, c0, state, False)
        sq0_buf[...] = sel_scores(c0 + 2)
        return sel_chunk(sq1_buf, c0 + 1, state, False)

    n_pairs = (i // SEL_TILES) // 2
    sq0_buf[...] = sel_scores(0)
    state = lax.fori_loop(0, n_pairs, sel_pair, new_state())
    c_tail = 2 * n_pairs
    sq1_buf[...] = sel_scores(c_tail + 1)
    state = sel_chunk(sq0_buf, c_tail, state, True)
    _, l_s, acc_s = sel_chunk(sq1_buf, c_tail + 1, state, True)
    o_sel = normalise(l_s, acc_s)

    n_win = WIN_LEN // KEY_TILE + 1
    j0 = jnp.maximum(i - (n_win - 1), 0)
    kw = jnp.concatenate([kwin_ref[0, 0, j0 + k] for k in range(n_win)], axis=0)
    dist = tok - (j0 * KEY_TILE + lax.broadcasted_iota(jnp.int32, (n_win * KEY_TILE, 1), 0))
    s = jnp.where((dist >= 0) & (dist < WIN_LEN), _dot(kw, qTm), NEG_INF)
    _, l_w, acc_w = _flash_update(s, [vTwin_ref[0, j0 + k, :, :] for k in range(n_win)], *new_state())
    o_win = normalise(l_w, acc_w)

    def gate(branch):
        g = gT_ref[0]
        return jnp.concatenate([g[branch * GQA_REP + r:branch * GQA_REP + r + 1, :] for r in range(GQA_REP)], axis=1)

    oT = o_cmp * gate(0) + o_sel * gate(1) + o_win * gate(2)
    o_rows = jnp.concatenate([oT[:, r * tq:(r + 1) * tq] for r in range(GQA_REP)], axis=0)
    o_ref[0] = o_rows.T.astype(BF16)


def _overlap_tiles(n_sel, n_cmp_pad):
    cs = np.arange(n_cmp_pad)[None, :] * CMP_STRIDE
    ss = np.arange(n_sel)[:, None] * SEL_LEN
    ov = ((cs < ss + SEL_LEN) & (cs + CMP_LEN > ss)).astype(np.float32)
    ov = ov.reshape(n_sel, n_cmp_pad // KEY_TILE, KEY_TILE).transpose(1, 0, 2)
    return jnp.asarray(ov, BF16)


def _nsa_attention(qT, kc, vcT, ksel, vTsel, kwin, vTwin, gT):
    B, D, S = qT.shape
    G, E = N_KV_HEADS, HEAD_DIM
    n_sel = S // SEL_LEN
    n_tiles = S // KEY_TILE
    n_chunks = kc.shape[2] // KEY_TILE
    kc5 = kc.reshape(B, G, n_chunks, KEY_TILE, E)
    ksel5 = ksel.reshape(B, G, n_tiles, KEY_TILE, E)
    kwin5 = kwin.reshape(B, G, n_tiles, KEY_TILE, E)
    ov = _overlap_tiles(n_sel, n_chunks * KEY_TILE)
    cmp_tiles = 2 if n_chunks % 2 == 0 else 1
    kspec = pl.BlockSpec((1, 1, n_tiles, KEY_TILE, E), lambda b, g, i: (b, g, 0, 0, 0))
    vspec = pl.BlockSpec((1, n_tiles, E, KEY_TILE), lambda b, g, i: (b, 0, g, 0))
    m_lanes = GQA_REP * Q_TILE
    return pl.pallas_call(
        _nsa_attn_kernel,
        out_shape=jax.ShapeDtypeStruct((B, S, D), BF16),
        grid=(B, G, S // Q_TILE),
        in_specs=[
            pl.BlockSpec((1, GQA_REP * E, Q_TILE), lambda b, g, i: (b, g, i)),
            pl.BlockSpec((1, 1, n_chunks, KEY_TILE, E), lambda b, g, i: (b, g, 0, 0, 0)),
            pl.BlockSpec((1, 1, n_chunks, E, KEY_TILE), lambda b, g, i: (b, g, 0, 0, 0)),
            pl.BlockSpec((n_chunks, n_sel, KEY_TILE), lambda b, g, i: (0, 0, 0)),
            kspec, vspec, kspec, vspec,
            pl.BlockSpec((1, 16, Q_TILE), lambda b, g, i: (b, g, i)),
        ],
        out_specs=pl.BlockSpec((1, Q_TILE, GQA_REP * E), lambda b, g, i: (b, i, g)),
        scratch_shapes=[
            pltpu.VMEM((n_chunks // cmp_tiles, cmp_tiles * KEY_TILE, m_lanes), F32),
            pltpu.VMEM((n_sel, m_lanes), F32),
            pltpu.VMEM((n_sel, Q_TILE), F32),
            pltpu.VMEM((SEL_TILES * KEY_TILE, m_lanes), F32),
            pltpu.VMEM((SEL_TILES * KEY_TILE, m_lanes), F32),
        ],
        compiler_params=_cparams(("parallel", "parallel", "arbitrary")),
        name="nsa_attention",
    )(qT, kc5, vcT, ov, ksel5, vTsel, kwin5, vTwin, gT)


def _layer_norm(z, g, b):
    mu = jnp.mean(z, axis=-1, keepdims=True)
    d = z - mu
    var = jnp.mean(d * d, axis=-1, keepdims=True)
    return d * lax.rsqrt(var + LN_EPS) * g + b


def _proj_ln_kernel(o_ref, x_ref, w_ref, gate_ref, g_ref, b_ref, sh_ref, sc_ref, rw_ref,
                    x1_ref, h_ref, lgT_ref):
    y = _dot(o_ref[0], w_ref[...])
    xn = _layer_norm(DN_ALPHA * x_ref[0] + gate_ref[0] * y, g_ref[...], b_ref[...])
    x1_ref[0] = xn
    h = xn * (1.0 + sc_ref[0]) + sh_ref[0]
    h_ref[0] = h
    lgT_ref[0] = lax.dot_general(rw_ref[...], h, (((1,), (1,)), ((), ())),
                                 preferred_element_type=F32, precision=HIGHEST)


def _proj_ln(o, x, w_o, gate, ln_g, ln_b, shift2, scale2, router_w):
    B, S, D = x.shape
    tm = ROW_TILE
    rw = router_w.T
    vec = pl.BlockSpec((1, 1, D), lambda b, i: (b, 0, 0))
    par = pl.BlockSpec((1, D), lambda b, i: (0, 0))
    row = pl.BlockSpec((1, tm, D), lambda b, i: (b, i, 0))
    return pl.pallas_call(
        _proj_ln_kernel,
        out_shape=[
            jax.ShapeDtypeStruct((B, S, D), F32),
            jax.ShapeDtypeStruct((B, S, D), F32),
            jax.ShapeDtypeStruct((B, N_EXPERTS, S), F32),
        ],
        grid=(B, S // tm),
        in_specs=[row, row, pl.BlockSpec((D, D), lambda b, i: (0, 0)), vec, par, par, vec, vec,
                  pl.BlockSpec((N_EXPERTS, D), lambda b, i: (0, 0))],
        out_specs=[row, row, pl.BlockSpec((1, N_EXPERTS, tm), lambda b, i: (b, 0, i))],
        compiler_params=_cparams(("parallel", "parallel")),
        name="proj_ln",
    )(o, x, w_o.astype(BF16), gate, ln_g.reshape(1, D), ln_b.reshape(1, D), shift2, scale2, rw)


def _first_max(v, idx, big):
    top = jnp.max(v, axis=0, keepdims=True)
    first = jnp.min(jnp.where(v == top, idx, big), axis=0, keepdims=True)
    return top, first


def _route_kernel(lg_ref, rb_ref, e_ref, w_ref):
    scores = jax.nn.sigmoid(lg_ref[0])
    biased = scores + rb_ref[...]
    eidx = lax.broadcasted_iota(jnp.int32, scores.shape, 0)
    npg = EXPERTS_PER_GROUP
    best_v, best_g = None, None
    for g in range(N_GROUPS):
        v = biased[g * npg:(g + 1) * npg]
        ii = g * npg + lax.broadcasted_iota(jnp.int32, v.shape, 0)
        top1, i1 = _first_max(v, ii, N_EXPERTS)
        top2 = jnp.max(jnp.where(ii == i1, NEG_INF, v), axis=0, keepdims=True)
        gs = top1 + top2
        if g == 0:
            best_v, best_g = gs, jnp.zeros_like(i1)
        else:
            better = gs > best_v
            best_g = jnp.where(better, g, best_g)
            best_v = jnp.where(better, gs, best_v)
    masked = jnp.where(eidx // npg == best_g, biased, NEG_INF)
    _, e1 = _first_max(masked, eidx, N_EXPERTS)
    _, e2 = _first_max(jnp.where(eidx == e1, NEG_INF, masked), eidx, N_EXPERTS)
    sc1 = jnp.sum(jnp.where(eidx == e1, scores, 0.0), axis=0, keepdims=True)
    sc2 = jnp.sum(jnp.where(eidx == e2, scores, 0.0), axis=0, keepdims=True)
    tot = sc1 + sc2
    e_ref[0] = jnp.concatenate([e1, e2], axis=0)
    w_ref[0] = jnp.concatenate([sc1 / tot, sc2 / tot], axis=0)


def _route(lgT, router_b):
    B, E, S = lgT.shape
    tn = min(S, 2048)
    return pl.pallas_call(
        _route_kernel,
        out_shape=[jax.ShapeDtypeStruct((B, TOP_K, S), jnp.int32), jax.ShapeDtypeStruct((B, TOP_K, S), F32)],
        grid=(B, S // tn),
        in_specs=[pl.BlockSpec((1, E, tn), lambda b, i: (b, 0, i)), pl.BlockSpec((E, 1), lambda b, i: (0, 0))],
        out_specs=[pl.BlockSpec((1, TOP_K, tn), lambda b, i: (b, 0, i))] * 2,
        compiler_params=_cparams(("parallel", "parallel")),
        name="moe_route",
    )(lgT, router_b.reshape(E, 1))


def _dispatch_plan(eidx):
    B, K, S = eidx.shape
    n_asg = B * S * K
    e_flat = eidx.transpose(0, 2, 1).reshape(n_asg)
    chunk = LANES
    onehot = (e_flat[:, None] == jnp.arange(N_EXPERTS, dtype=jnp.int32)[None, :]).astype(F32)
    oh = onehot.reshape(n_asg // chunk, chunk, N_EXPERTS)
    tri = jnp.tril(jnp.ones((chunk, chunk), F32))
    within = jnp.einsum("ij,cjk->cik", tri, oh)
    chunk_tot = within[:, -1, :]
    chunk_end = jnp.cumsum(chunk_tot, axis=0)
    incl = within + (chunk_end - chunk_tot)[:, None, :]
    rank = (jnp.sum(incl * oh, axis=-1) - 1.0).reshape(n_asg).astype(jnp.int32)
    counts = chunk_end[-1].astype(jnp.int32)
    padded = (counts + MOE_BLOCK - 1) // MOE_BLOCK * MOE_BLOCK
    pad_ends = jnp.cumsum(padded)
    dest = (pad_ends - padded)[e_flat] + rank
    n_blk = n_asg // MOE_BLOCK + N_EXPERTS
    blk_start = jnp.arange(n_blk, dtype=jnp.int32) * MOE_BLOCK
    blk_exp = jnp.minimum(jnp.sum((pad_ends[None, :] <= blk_start[:, None]).astype(jnp.int32), axis=1),
                          N_EXPERTS - 1)
    return blk_exp, dest


def _moe_dispatch_kernel(n_steps, dest_ref, x_ref, xs_init_ref, xs_hbm, stage, sems):
    del xs_init_ref
    i = pl.program_id(0)
    tm = x_ref.shape[0]
    slot = i & 1

    def drain(s):
        for _ in range(TOP_K):
            pltpu.make_async_copy(stage.at[s], xs_hbm.at[pl.ds(0, tm), :], sems.at[s]).wait()

    @pl.when(i >= 2)
    def _():
        drain(slot)

    stage[slot] = x_ref[...]

    def issue(r, c):
        for k in range(TOP_K):
            pltpu.make_async_copy(stage.at[slot, pl.ds(r, 1), :],
                                  xs_hbm.at[pl.ds(dest_ref[0, 0, TOP_K * r + k], 1), :],
                                  sems.at[slot]).start(priority=k % 2)
        return c

    lax.fori_loop(0, tm, issue, 0, unroll=8)

    @pl.when(i == n_steps - 1)
    def _():
        drain(slot)
        if n_steps >= 2:
            drain(1 - slot)


def _moe_dispatch(h2d, dest, n_rows):
    n_tok, D = h2d.shape
    tm = MOE_IO_TILE
    n_steps = n_tok // tm
    return pl.pallas_call(
        functools.partial(_moe_dispatch_kernel, n_steps),
        out_shape=jax.ShapeDtypeStruct((n_rows, D), F32),
        grid=(n_steps,),
        in_specs=[
            pl.BlockSpec((1, 1, TOP_K * tm), lambda i: (i, 0, 0), memory_space=pltpu.SMEM),
            pl.BlockSpec((tm, D), lambda i: (i, 0)),
            pl.BlockSpec(memory_space=pl.ANY),
        ],
        out_specs=pl.BlockSpec(memory_space=pl.ANY),
        scratch_shapes=[pltpu.VMEM((2, tm, D), F32), pltpu.SemaphoreType.DMA((2,))],
        input_output_aliases={2: 0},
        compiler_params=_cparams(("arbitrary",)),
        name="moe_dispatch",
    )(dest.reshape(n_steps, 1, TOP_K * tm), h2d, jnp.zeros((n_rows, D), F32))


def _moe_expert_kernel(blk_exp_ref, x_ref, wg_ref, wu_ref, wd_ref, y_ref):
    del blk_exp_ref
    x = x_ref[...].astype(BF16)
    gate = _dot(x, wg_ref[0])
    up = _dot(x, wu_ref[0])
    hid = (gate * jax.nn.sigmoid(gate) * up).astype(BF16)
    y_ref[...] = _dot(hid, wd_ref[0])


def _moe_experts(xs, blk_exp, w_gate, w_up, w_down):
    n_rows, D = xs.shape
    F = w_gate.shape[2]
    rows = pl.BlockSpec((MOE_BLOCK, D), lambda i, be: (i, 0))
    grid_spec = pltpu.PrefetchScalarGridSpec(
        num_scalar_prefetch=1,
        grid=(n_rows // MOE_BLOCK,),
        in_specs=[
            rows,
            pl.BlockSpec((1, D, F), lambda i, be: (be[i], 0, 0)),
            pl.BlockSpec((1, D, F), lambda i, be: (be[i], 0, 0)),
            pl.BlockSpec((1, F, D), lambda i, be: (be[i], 0, 0)),
        ],
        out_specs=rows,
    )
    return pl.pallas_call(
        _moe_expert_kernel,
        out_shape=jax.ShapeDtypeStruct((n_rows, D), F32),
        grid_spec=grid_spec,
        compiler_params=_cparams(("arbitrary",)),
        name="moe_experts",
    )(blk_exp, xs, w_gate.astype(BF16), w_up.astype(BF16), w_down.astype(BF16))


def _moe_combine_kernel(n_steps, dcur_ref, dnext_ref, w_ref, x_ref, gate_ref, g_ref, b_ref, ys_hbm,
                        o_ref, gbuf, sems):
    i = pl.program_id(0)
    tm = x_ref.shape[0]
    slot = i & 1

    def issue(d_ref, s):
        def body(r, c):
            for k in range(TOP_K):
                pltpu.make_async_copy(ys_hbm.at[pl.ds(d_ref[0, 0, TOP_K * r + k], 1), :],
                                      gbuf.at[s, k, pl.ds(r, 1), :], sems.at[s]).start(priority=k % 2)
            return c
        lax.fori_loop(0, tm, body, 0, unroll=8)

    @pl.when(i == 0)
    def _():
        issue(dcur_ref, slot)

    @pl.when(i + 1 < n_steps)
    def _():
        issue(dnext_ref, 1 - slot)

    for k in range(TOP_K):
        pltpu.make_async_copy(ys_hbm.at[pl.ds(0, tm), :], gbuf.at[slot, k], sems.at[slot]).wait()
    w = w_ref[...]
    y = gbuf[slot, 0] * w[:, 0:1] + gbuf[slot, 1] * w[:, 1:2]
    o_ref[...] = _layer_norm(DN_ALPHA * x_ref[...] + gate_ref[0] * y, g_ref[...], b_ref[...])


def _moe_combine_ln(ys, dest, wts, x, gate, ln_g, ln_b):
    B, S, D = x.shape
    n_tok = B * S
    tm = MOE_IO_TILE
    n_steps = n_tok // tm
    per_b = S // tm
    d3 = dest.reshape(n_steps, 1, TOP_K * tm)
    w2 = wts.transpose(0, 2, 1).reshape(n_tok, TOP_K)
    idx = lambda f: pl.BlockSpec((1, 1, TOP_K * tm), f, memory_space=pltpu.SMEM)
    par = pl.BlockSpec((1, D), lambda i: (0, 0))
    row = pl.BlockSpec((tm, D), lambda i: (i, 0))
    out = pl.pallas_call(
        functools.partial(_moe_combine_kernel, n_steps),
        out_shape=jax.ShapeDtypeStruct((n_tok, D), F32),
        grid=(n_steps,),
        in_specs=[
            idx(lambda i: (i, 0, 0)),
            idx(lambda i: (jnp.minimum(i + 1, n_steps - 1), 0, 0)),
            pl.BlockSpec((tm, TOP_K), lambda i: (i, 0)),
            row,
            pl.BlockSpec((1, 1, D), lambda i: (i // per_b, 0, 0)),
            par, par,
            pl.BlockSpec(memory_space=pl.ANY),
        ],
        out_specs=row,
        scratch_shapes=[pltpu.VMEM((2, TOP_K, tm, D), F32), pltpu.SemaphoreType.DMA((2,))],
        compiler_params=_cparams(("arbitrary",)),
        name="moe_combine_ln",
    )(d3, d3, w2, x.reshape(n_tok, D), gate, ln_g.reshape(1, D), ln_b.reshape(1, D), ys)
    return out.reshape(B, S, D)


def _moe_sublayer(x1, h2, lgT, router_b, w_gate, w_up, w_down, gate, ln_g, ln_b):
    B, S, D = x1.shape
    eidx, wts = _route(lgT, router_b)
    blk_exp, dest = _dispatch_plan(eidx)
    n_rows = blk_exp.shape[0] * MOE_BLOCK
    xs = _moe_dispatch(h2.reshape(B * S, D), dest, n_rows)
    ys = _moe_experts(xs, blk_exp, w_gate, w_up, w_down)
    return _moe_combine_ln(ys, dest, wts, x1, gate, ln_g, ln_b)


def _dil_weight_cols():
    d = D_MODEL
    cols = list(range(d))
    for which in range(2):
        for p in range(len(DIL_PATTERNS)):
            base = d + (p * 2 + which) * KV_COLS
            cols += list(range(base, base + KV_COLS))
    return np.asarray(cols)


def _dil_inproj_kernel(x_ref, sh_ref, sc_ref, w_ref, c_ref, s1_ref, s2_ref, q_ref, *kv_refs):
    h = (x_ref[0] * (1.0 + sc_ref[0]) + sh_ref[0]).astype(BF16)
    c, s1, s2 = c_ref[0], s1_ref[0], s2_ref[0]
    d = q_ref.shape[2]
    w = KV_COLS
    n_pat = len(DIL_PATTERNS)
    for j in range(d // w):
        a = _rope_cols(_dot(h, w_ref[:, j * w:(j + 1) * w]), c, s1, s2) * Q_SCALE
        q_ref[0, :, j * w:(j + 1) * w] = a.astype(BF16)
    for p in range(n_pat):
        a = _rope_cols(_dot(h, w_ref[:, d + p * w:d + (p + 1) * w]), c, s1, s2)
        kv_refs[p][0] = a.astype(BF16)
    for p in range(n_pat):
        a = _dot(h, w_ref[:, d + (n_pat + p) * w:d + (n_pat + p + 1) * w])
        kv_refs[n_pat + p][0] = a.astype(BF16)


def _dil_inproj(x, shift, scale, w_in, tabs):
    B, S, D = x.shape
    tm = ROW_TILE
    wp = _permute_cols(w_in, _dil_weight_cols())
    ncol = wp.shape[1]
    n_pat = len(DIL_PATTERNS)
    vec = pl.BlockSpec((1, 1, D), lambda b, i: (b, 0, 0))
    tab = pl.BlockSpec((1, tm, LANES), lambda b, i: (b, i, 0))
    kvspec = pl.BlockSpec((1, tm, KV_COLS), lambda b, i: (b, i, 0))
    outs = pl.pallas_call(
        _dil_inproj_kernel,
        out_shape=[jax.ShapeDtypeStruct((B, S, D), BF16)]
        + [jax.ShapeDtypeStruct((B, S, KV_COLS), BF16)] * (2 * n_pat),
        grid=(B, S // tm),
        in_specs=[pl.BlockSpec((1, tm, D), lambda b, i: (b, i, 0)), vec, vec,
                  pl.BlockSpec((D, ncol), lambda b, i: (0, 0)), tab, tab, tab],
        out_specs=[pl.BlockSpec((1, tm, D), lambda b, i: (b, i, 0))] + [kvspec] * (2 * n_pat),
        compiler_params=_cparams(("parallel", "parallel")),
        name="dil_inproj",
    )(x, shift, scale, wp, *tabs)
    return outs[0], outs[1:1 + n_pat], outs[1 + n_pat:]


def _dil_attn_kernel(steps, first, last, *refs):
    if first:
        q_ref, kc_ref, kp_ref, vc_ref, vp_ref = refs[:5]
        acc_in = ml_in = None
        outs = refs[5:]
    else:
        q_ref, kc_ref, kp_ref, vc_ref, vp_ref, acc_in, ml_in = refs[:7]
        outs = refs[7:]
    nb = pl.program_id(2)
    blk = DIL_BLOCK
    e = HEAD_DIM
    gw = GQA_REP * e
    kj = lax.broadcasted_iota(jnp.int32, (2 * blk, blk), 0)
    qi = lax.broadcasted_iota(jnp.int32, (2 * blk, blk), 1)
    dist = blk + qi - kj
    valid = (dist >= 0) & (dist <= steps) & ((nb - 1) * blk + kj >= 0)
    bias = jnp.where(valid, 0.0, NEG_INF)
    bias = jnp.concatenate([bias] * GQA_REP, axis=1)
    zeros_half = jnp.zeros((e, GQA_REP * blk), F32)
    mlT_old = None if first else ml_in[0].T
    m_rows, l_rows = [], []

    def heads_to_lanes(t):
        return jnp.concatenate([t[r * e:(r + 1) * e] for r in range(GQA_REP)], axis=1)

    for g in range(N_KV_HEADS):
        seg = slice((g // 2) * LANES, (g // 2 + 1) * LANES)
        qs = slice(g * gw, (g + 1) * gw)
        qTm = heads_to_lanes(q_ref[0][:, qs].astype(F32).T)
        qT2 = jnp.concatenate([qTm, zeros_half] if g % 2 == 0 else [zeros_half, qTm], axis=0).astype(BF16)
        kcat = jnp.concatenate([kp_ref[0][:, seg], kc_ref[0][:, seg]], axis=0)
        s = _dot(kcat, qT2) + bias
        m_new = jnp.max(s, axis=0, keepdims=True)
        if not first:
            m_old = jnp.concatenate([mlT_old[g * GQA_REP + r:g * GQA_REP + r + 1] for r in range(GQA_REP)], axis=1)
            l_old = jnp.concatenate([mlT_old[N_Q_HEADS + g * GQA_REP + r:N_Q_HEADS + g * GQA_REP + r + 1]
                                     for r in range(GQA_REP)], axis=1)
            m_new = jnp.maximum(m_old, m_new)
            alpha = jnp.exp2(m_old - m_new)
        p = jnp.exp2(s - m_new)
        l_new = jnp.sum(p, axis=0, keepdims=True)
        vcat = jnp.concatenate([vp_ref[0][:, seg], vc_ref[0][:, seg]], axis=0).astype(F32)
        vT = vcat.T[(g % 2) * e:(g % 2 + 1) * e].astype(BF16)
        accT = _dot(vT, p.astype(BF16))
        if not first:
            l_new = l_new + alpha * l_old
            accT = accT + alpha * heads_to_lanes(acc_in[0][:, qs].T)
        if last:
            accT = accT / l_new
        o_rows = jnp.concatenate([accT[:, r * blk:(r + 1) * blk] for r in range(GQA_REP)], axis=0)
        outs[0][0, :, qs] = o_rows.T.astype(outs[0].dtype)
        m_rows += [m_new[:, r * blk:(r + 1) * blk] for r in range(GQA_REP)]
        l_rows += [l_new[:, r * blk:(r + 1) * blk] for r in range(GQA_REP)]
    if not last:
        pad = jnp.zeros((LANES - 2 * N_Q_HEADS, blk), F32)
        outs[1][0] = jnp.concatenate(m_rows + l_rows + [pad], axis=0).T


def _dil_attention(q, kds, vds):
    B, S, D = q.shape
    n_pat = len(DIL_PATTERNS)
    acc = ml = None
    for p, (window, dil) in enumerate(DIL_PATTERNS):
        first, last = p == 0, p == n_pat - 1
        L = S // dil
        nblk = L // DIL_BLOCK
        kw = KV_COLS

        def view(a):
            return a.reshape(B, L, dil * a.shape[2])

        cur = lambda b, c, n: (b, n, c)
        prev = lambda b, c, n: (b, jnp.maximum(n - 1, 0), c)
        qspec = pl.BlockSpec((1, DIL_BLOCK, D), cur)
        in_specs = [qspec, pl.BlockSpec((1, DIL_BLOCK, kw), cur), pl.BlockSpec((1, DIL_BLOCK, kw), prev),
                    pl.BlockSpec((1, DIL_BLOCK, kw), cur), pl.BlockSpec((1, DIL_BLOCK, kw), prev)]
        args = [view(q), view(kds[p]), view(kds[p]), view(vds[p]), view(vds[p])]
        if not first:
            in_specs += [qspec, pl.BlockSpec((1, DIL_BLOCK, LANES), cur)]
            args += [view(acc), view(ml)]
        if last:
            out_shape = [jax.ShapeDtypeStruct((B, L, dil * D), BF16)]
            out_specs = [qspec]
        else:
            out_shape = [jax.ShapeDtypeStruct((B, L, dil * D), F32),
                         jax.ShapeDtypeStruct((B, L, dil * LANES), F32)]
            out_specs = [qspec, pl.BlockSpec((1, DIL_BLOCK, LANES), cur)]
        res = pl.pallas_call(
            functools.partial(_dil_attn_kernel, window // dil, first, last),
            out_shape=out_shape,
            grid=(B, dil, nblk),
            in_specs=in_specs,
            out_specs=out_specs,
            compiler_params=_cparams(("parallel", "parallel", "arbitrary")),
            name=f"dil_attention_{p}",
        )(*args)
        if last:
            return res[0].reshape(B, S, D)
        acc, ml = res[0].reshape(B, S, D), res[1].reshape(B, S, LANES)


def kernel(x, c, positions, ada_w, ada_b, ln_g, ln_b, nsa_w_in, nsa_cmp_pos_k, nsa_cmp_w1_k, nsa_cmp_w2_k, nsa_cmp_pos_v, nsa_cmp_w1_v, nsa_cmp_w2_v, nsa_w_o, dil_w_in, dil_w_o, router_w, router_b, moe_w_gate, moe_w_up, moe_w_down):
    B, S, D = x.shape
    mods = _ada_mods(c, ada_w, ada_b)
    def mod(i, sub):
        m = mods[i * 2 + sub]
        return [m[:, k * D:(k + 1) * D].reshape(B, 1, D) for k in range(3)]
    tabs = _rope_tables(positions)

    for i in range(DEPTH):
        shift, scale, gate = mod(i, 0)
        shift2, scale2, gate2 = mod(i, 1)
        j = i // 2
        if i % 2 == 0:
            qT, ksel, kwin, vTsel, vTwin, kcmp, vcmp, gT = _nsa_inproj(x, shift, scale, nsa_w_in[j], tabs)
            kc, vcT = _compress(kcmp, vcmp, nsa_cmp_pos_k[j], nsa_cmp_w1_k[j], nsa_cmp_w2_k[j],
                                nsa_cmp_pos_v[j], nsa_cmp_w1_v[j], nsa_cmp_w2_v[j], tabs)
            o = _nsa_attention(qT, kc, vcT, ksel, vTsel, kwin, vTwin, gT)
            w_o = nsa_w_o[j]
        else:
            q, kds, vds = _dil_inproj(x, shift, scale, dil_w_in[j], tabs)
            o = _dil_attention(q, kds, vds)
            w_o = dil_w_o[j]
        x1, h2, lgT = _proj_ln(o, x, w_o, gate, ln_g[i, 0], ln_b[i, 0], shift2, scale2, router_w)
        x = _moe_sublayer(x1, h2, lgT, router_b, moe_w_gate[i], moe_w_up[i], moe_w_down[i],
                          gate2, ln_g[i, 1], ln_b[i, 1])
    return x
```

```python
import functools

import numpy as np
import jax
import jax.numpy as jnp
from jax import lax
from jax.experimental import pallas as pl
from jax.experimental.pallas import tpu as pltpu

F32 = jnp.float32
BF16 = jnp.bfloat16
HIGHEST = lax.Precision.HIGHEST
NEG_INF = float("-inf")

D_MODEL = 1024
DEPTH = 2
HEAD_DIM = 64
N_Q_HEADS = D_MODEL // HEAD_DIM
N_KV_HEADS = 4
GQA_REP = N_Q_HEADS // N_KV_HEADS
ROPE_DIM = HEAD_DIM // 4
ROPE_THETA = 500000.0
ATTN_SCALE = HEAD_DIM ** -0.5
LOG2_E = 1.4426950408889634
Q_SCALE = ATTN_SCALE * LOG2_E
KV_COLS = N_KV_HEADS * HEAD_DIM
N_BRANCH = 3
CMP_LEN = 32
CMP_STRIDE = 16
CMP_HIDDEN = 256
SEL_LEN = 64
N_SELECT = 16
WIN_LEN = 512
FORCE_SCORE = 1.0e4
DIL_PATTERNS = ((128, 1), (512, 4), (2048, 16))
DIL_BLOCK = 128
N_EXPERTS = 32
N_GROUPS = 4
EXPERTS_PER_GROUP = N_EXPERTS // N_GROUPS
TOP_K = 2
D_EXPERT = 512
MOE_BLOCK = 128
DN_ALPHA = (2.0 * DEPTH) ** 0.25
LN_EPS = 1e-5

LANES = 128
VMEM_LIMIT_BYTES = 48 * 1024 * 1024

Q_TILE = 256
KEY_TILE = 128
SEL_TILES = 4
V_EXT = HEAD_DIM + 16
ROW_TILE = 512
MOE_IO_TILE = 256


def _cparams(semantics):
    return pltpu.CompilerParams(dimension_semantics=semantics, vmem_limit_bytes=VMEM_LIMIT_BYTES)


def _dot(a, b):
    return jnp.dot(a, b, preferred_element_type=F32)


def _dot_nt(a, b):
    return lax.dot_general(a, b, (((1,), (1,)), ((), ())), preferred_element_type=F32)


def _ada_kernel(c_ref, w_ref, b_ref, o_ref):
    c = c_ref[...]
    cond = c * jax.nn.sigmoid(c)
    o_ref[0] = jnp.dot(cond, w_ref[0], preferred_element_type=F32, precision=HIGHEST) + b_ref[0]


def _ada_mods(c, ada_w, ada_b):
    B, D = c.shape
    n_sub = ada_w.shape[0] * ada_w.shape[1]
    w = ada_w.reshape(n_sub, D, 3 * D)
    b = ada_b.reshape(n_sub, 1, 3 * D)
    c8 = jnp.zeros((8, D), F32).at[:B].set(c)
    tn = 768
    out = pl.pallas_call(
        _ada_kernel,
        out_shape=jax.ShapeDtypeStruct((n_sub, 8, 3 * D), F32),
        grid=(n_sub, 3 * D // tn),
        in_specs=[
            pl.BlockSpec((8, D), lambda s, j: (0, 0)),
            pl.BlockSpec((1, D, tn), lambda s, j: (s, 0, j)),
            pl.BlockSpec((1, 1, tn), lambda s, j: (s, 0, j)),
        ],
        out_specs=pl.BlockSpec((1, 8, tn), lambda s, j: (s, 0, j)),
        compiler_params=_cparams(("parallel", "parallel")),
        name="ada_mods",
    )(c8, w, b)
    return out[:, :B]


def _rope_tab_kernel(pos_ref, inv_ref, sg1_ref, sg2_ref, c_ref, s1_ref, s2_ref):
    ang = pos_ref[0] * inv_ref[...]
    sin = jnp.sin(ang)
    c_ref[0] = jnp.cos(ang)
    s1_ref[0] = sin * sg1_ref[...]
    s2_ref[0] = sin * sg2_ref[...]


def _rope_tables(positions):
    B, S = positions.shape
    half = ROPE_DIM // 2
    inv = ROPE_THETA ** (-jnp.arange(half, dtype=F32) * (2.0 / ROPE_DIM))
    li = np.arange(LANES) % HEAD_DIM
    in_rope = li < ROPE_DIM
    inv_row = jnp.where(jnp.asarray(in_rope), inv[li % half], 0.0).reshape(1, LANES)
    sg1 = jnp.asarray(np.where(li < half, -1.0, 0.0), F32).reshape(1, LANES)
    sg2 = jnp.asarray(np.where((li >= half) & in_rope, 1.0, 0.0), F32).reshape(1, LANES)
    pos = positions.astype(F32).reshape(B, S, 1)
    tm = min(S, 2048)
    row = pl.BlockSpec((1, LANES), lambda b, i: (0, 0))
    tab = pl.BlockSpec((1, tm, LANES), lambda b, i: (b, i, 0))
    return pl.pallas_call(
        _rope_tab_kernel,
        out_shape=[jax.ShapeDtypeStruct((B, S, LANES), F32)] * 3,
        grid=(B, S // tm),
        in_specs=[pl.BlockSpec((1, tm, 1), lambda b, i: (b, i, 0)), row, row, row],
        out_specs=[tab, tab, tab],
        compiler_params=_cparams(("parallel", "parallel")),
        name="rope_tables",
    )(pos, inv_row, sg1, sg2)


def _rope128(t, c, s1, s2):
    return t * c + pltpu.roll(t, LANES - ROPE_DIM // 2, 1) * s1 + pltpu.roll(t, ROPE_DIM // 2, 1) * s2


def _rope_cols(a, c, s1, s2):
    n = a.shape[1] // LANES
    return jnp.concatenate(
        [_rope128(a[:, k * LANES:(k + 1) * LANES], c, s1, s2) for k in range(n)], axis=1)


def _nsa_weight_cols():
    d = D_MODEL
    def kv(branch, which):
        base = d + (branch * 2 + which) * KV_COLS
        return list(range(base, base + KV_COLS))
    cols = list(range(d))
    cols += kv(1, 0) + kv(2, 0) + kv(1, 1) + kv(2, 1) + kv(0, 0) + kv(0, 1)
    gate0 = d + N_BRANCH * 2 * KV_COLS
    gcols = [-1] * LANES
    for g in range(N_KV_HEADS):
        for br in range(N_BRANCH):
            for r in range(GQA_REP):
                gcols[g * 16 + br * GQA_REP + r] = gate0 + (g * GQA_REP + r) * N_BRANCH + br
    return np.asarray(cols + gcols)


def _permute_cols(w, cols):
    picked = w[:, np.maximum(cols, 0)]
    return jnp.where(jnp.asarray(cols >= 0)[None, :], picked, 0.0).astype(BF16)


def _nsa_inproj_kernel(x_ref, sh_ref, sc_ref, w_ref, c_ref, s1_ref, s2_ref,
                       qT_ref, ksel_ref, kwin_ref, vTsel_ref, vTwin_ref, kcmp_ref, vcmp_ref, gT_ref):
    tm = x_ref.shape[1]
    h = (x_ref[0] * (1.0 + sc_ref[0]) + sh_ref[0]).astype(BF16)
    c, s1, s2 = c_ref[0], s1_ref[0], s2_ref[0]
    w = KV_COLS

    def proj(j, n=w):
        return _dot(h, w_ref[:, j * w:j * w + n])

    for j in range(4):
        a = _rope_cols(proj(j), c, s1, s2) * Q_SCALE
        qT_ref[0, j * w:(j + 1) * w, :] = a.T.astype(BF16)
    for j, ref in ((4, ksel_ref), (5, kwin_ref)):
        a = _rope_cols(proj(j), c, s1, s2)
        for g in range(N_KV_HEADS):
            ref[0, g] = a[:, g * HEAD_DIM:(g + 1) * HEAD_DIM].astype(BF16)
    ones_rows = jnp.ones((V_EXT - HEAD_DIM, KEY_TILE), BF16)
    for j, ref in ((6, vTsel_ref), (7, vTwin_ref)):
        aT = proj(j).T.astype(BF16)
        for k in range(tm // KEY_TILE):
            for g in range(N_KV_HEADS):
                ref[0, k, g * V_EXT:g * V_EXT + HEAD_DIM] = aT[g * HEAD_DIM:(g + 1) * HEAD_DIM,
                                                               k * KEY_TILE:(k + 1) * KEY_TILE]
                ref[0, k, g * V_EXT + HEAD_DIM:(g + 1) * V_EXT] = ones_rows
    for j, ref in ((8, kcmp_ref), (9, vcmp_ref)):
        a = proj(j)
        for g in range(N_KV_HEADS):
            ref[0, g] = a[:, g * HEAD_DIM:(g + 1) * HEAD_DIM].astype(BF16)
    gates = jax.nn.sigmoid(proj(10, LANES))
    gT_ref[0] = gates.T[:4 * 16]


def _nsa_inproj(x, shift, scale, w_in, tabs):
    B, S, D = x.shape
    tm = ROW_TILE
    wp = _permute_cols(w_in, _nsa_weight_cols())
    ncol = wp.shape[1]
    vec = pl.BlockSpec((1, 1, D), lambda b, i: (b, 0, 0))
    tab = pl.BlockSpec((1, tm, LANES), lambda b, i: (b, i, 0))
    nat = pl.BlockSpec((1, N_KV_HEADS, tm, HEAD_DIM), lambda b, i: (b, 0, i, 0))
    vt = pl.BlockSpec((1, tm // KEY_TILE, N_KV_HEADS * V_EXT, KEY_TILE), lambda b, i: (b, i, 0, 0))
    nat_shape = jax.ShapeDtypeStruct((B, N_KV_HEADS, S, HEAD_DIM), BF16)
    vt_shape = jax.ShapeDtypeStruct((B, S // KEY_TILE, N_KV_HEADS * V_EXT, KEY_TILE), BF16)
    return pl.pallas_call(
        _nsa_inproj_kernel,
        out_shape=[
            jax.ShapeDtypeStruct((B, D, S), BF16),
            nat_shape, nat_shape,
            vt_shape, vt_shape,
            nat_shape, nat_shape,
            jax.ShapeDtypeStruct((B, 4 * 16, S), F32),
        ],
        grid=(B, S // tm),
        in_specs=[
            pl.BlockSpec((1, tm, D), lambda b, i: (b, i, 0)), vec, vec,
            pl.BlockSpec((D, ncol), lambda b, i: (0, 0)), tab, tab, tab,
        ],
        out_specs=[
            pl.BlockSpec((1, D, tm), lambda b, i: (b, 0, i)),
            nat, nat, vt, vt, nat, nat,
            pl.BlockSpec((1, 4 * 16, tm), lambda b, i: (b, 0, i)),
        ],
        compiler_params=_cparams(("parallel", "parallel")),
        name="nsa_inproj",
    )(x, shift, scale, wp, *tabs)


def _compress_kernel(xk_ref, xv_ref, w1k_ref, w1v_ref, pk_ref, pv_ref, w2k_ref, w2vT_ref,
                     c_ref, s1_ref, s2_ref, kc_ref, vcT_ref):
    n = xk_ref.shape[2]
    half = w1k_ref.shape[0] // 2

    def hidden(x_ref, w1_ref, p_ref):
        x = x_ref[0, 0]
        first = _dot(x, w1_ref[:half])
        second = _dot(x, w1_ref[half:])
        bias = _dot(p_ref[...], w1_ref[...])[0:1]
        hid = first + pltpu.roll(second, n - 1, 0) + bias
        return jax.nn.gelu(hid).astype(BF16)

    kc = _dot(hidden(xk_ref, w1k_ref, pk_ref), w2k_ref[...])
    kc = _rope128(kc, c_ref[0], s1_ref[0], s2_ref[0])
    row = lax.broadcasted_iota(jnp.int32, kc.shape, 0)
    kc = jnp.where(row < n - 1, kc, 0.0)
    kc_ref[0, 0] = kc[:, :HEAD_DIM].astype(BF16)

    vcT = _dot_nt(w2vT_ref[...], hidden(xv_ref, w1v_ref, pv_ref))
    col = lax.broadcasted_iota(jnp.int32, vcT.shape, 1)
    vcT = jnp.where(col < n - 1, vcT, 0.0).astype(BF16)
    for k in range(n // KEY_TILE):
        vcT_ref[0, 0, k] = vcT[:, k * KEY_TILE:(k + 1) * KEY_TILE]


def _compress(kcmp, vcmp, pos_k, w1_k, w2_k, pos_v, w1_v, w2_v, tabs):
    B, G, S, E = kcmp.shape
    n = S // CMP_STRIDE
    wide = CMP_STRIDE * E
    xk = kcmp.reshape(B, G, n, wide)
    xv = vcmp.reshape(B, G, n, wide)
    def flat8(p):
        return jnp.zeros((8, CMP_LEN * E), BF16).at[0].set(p.reshape(-1).astype(BF16))
    w2k = jnp.zeros((CMP_HIDDEN, LANES), BF16).at[:, :E].set(w2_k.astype(BF16))
    w2vT = w2_v.T.astype(BF16)
    last = CMP_LEN - 1
    ctabs = [jnp.zeros((B, n, LANES), F32).at[:, :n - 1].set(t[:, last::CMP_STRIDE][:, :n - 1]) for t in tabs]
    xspec = pl.BlockSpec((1, 1, n, wide), lambda b, g: (b, g, 0, 0))
    w1spec = pl.BlockSpec((CMP_LEN * E, CMP_HIDDEN), lambda b, g: (0, 0))
    pspec = pl.BlockSpec((8, CMP_LEN * E), lambda b, g: (0, 0))
    tspec = pl.BlockSpec((1, n, LANES), lambda b, g: (b, 0, 0))
    return pl.pallas_call(
        _compress_kernel,
        out_shape=[
            jax.ShapeDtypeStruct((B, G, n, E), BF16),
            jax.ShapeDtypeStruct((B, G, n // KEY_TILE, E, KEY_TILE), BF16),
        ],
        grid=(B, G),
        in_specs=[xspec, xspec, w1spec, w1spec, pspec, pspec,
                  pl.BlockSpec((CMP_HIDDEN, LANES), lambda b, g: (0, 0)),
                  pl.BlockSpec((E, CMP_HIDDEN), lambda b, g: (0, 0)),
                  tspec, tspec, tspec],
        out_specs=[
            pl.BlockSpec((1, 1, n, E), lambda b, g: (b, g, 0, 0)),
            pl.BlockSpec((1, 1, n // KEY_TILE, E, KEY_TILE), lambda b, g: (b, g, 0, 0, 0)),
        ],
        compiler_params=_cparams(("parallel", "parallel")),
        name="nsa_compress",
    )(xk, xv, w1_k.astype(BF16), w1_v.astype(BF16), flat8(pos_k), flat8(pos_v), w2k, w2vT, *ctabs)


def _nsa_attn_kernel(qT_ref, kc_ref, vcT_ref, ov_ref, ksel_ref, vTsel_ref, kwin_ref, vTwin_ref, gT_ref,
                     o_ref, s_buf, imp_buf, sel_buf, sq0_buf, sq1_buf):
    i = pl.program_id(2)
    tq = Q_TILE
    m_lanes = GQA_REP * tq
    e = HEAD_DIM
    t0 = i * tq
    n_chunks = kc_ref.shape[2]
    n_sel = sel_buf.shape[0]

    qT = qT_ref[0]
    qTm = jnp.concatenate([qT[r * e:(r + 1) * e] for r in range(GQA_REP)], axis=1)
    lane = lax.broadcasted_iota(jnp.int32, (1, m_lanes), 1)
    tok = t0 + (lane & (tq - 1))
    row_k = lax.broadcasted_iota(jnp.int32, (KEY_TILE, 1), 0)

    cmp_tiles = s_buf.shape[1] // KEY_TILE
    cmp_rows = cmp_tiles * KEY_TILE
    last_cmp_tile = ((t0 + tq - CMP_LEN) // CMP_STRIDE) // KEY_TILE
    n_vis = jnp.minimum(last_cmp_tile // cmp_tiles + 1, n_chunks // cmp_tiles)
    row_c = lax.broadcasted_iota(jnp.int32, (cmp_rows, 1), 0)

    def cmp_scores(c, m):
        kc = jnp.concatenate([kc_ref[0, 0, c * cmp_tiles + k] for k in range(cmp_tiles)], axis=0)
        last_tok = (c * cmp_rows + row_c) * CMP_STRIDE + (CMP_LEN - 1)
        s = jnp.where(last_tok <= tok, _dot(kc, qTm), NEG_INF)
        s_buf[c] = s
        return jnp.maximum(m, jnp.max(s, axis=0, keepdims=True))

    m_c = lax.fori_loop(0, n_vis, cmp_scores, jnp.full((1, m_lanes), NEG_INF, F32))
    m_c = jnp.where(m_c == NEG_INF, 0.0, m_c)

    imp_buf[...] = jnp.zeros(imp_buf.shape, F32)

    def cmp_accum(c, carry):
        l, acc = carry
        p = jnp.exp2(s_buf[c] - m_c)
        pb = p.astype(BF16)
        imp = imp_buf[...]
        for k in range(cmp_tiles):
            pk = pb[k * KEY_TILE:(k + 1) * KEY_TILE]
            imp = imp + _dot(ov_ref[c * cmp_tiles + k], pk)
            acc = acc + _dot(vcT_ref[0, 0, c * cmp_tiles + k], pk)
        imp_buf[...] = imp
        return l + jnp.sum(p, axis=0, keepdims=True), acc

    l_c, acc_c = lax.fori_loop(
        0, n_vis, cmp_accum, (jnp.zeros((1, m_lanes), F32), jnp.zeros((e, m_lanes), F32)))
    inv_l = 1.0 / jnp.maximum(l_c, 1e-30)
    o_cmp = acc_c * inv_l
    imp_n = imp_buf[...] * inv_l
    imp = imp_n[:, 0:tq]
    for r in range(1, GQA_REP):
        imp = imp + imp_n[:, r * tq:(r + 1) * tq]

    sidx = lax.broadcasted_iota(jnp.int32, (n_sel, tq), 0)
    cur = (t0 + lax.broadcasted_iota(jnp.int32, (1, tq), 1)) // SEL_LEN
    forced = (sidx == 0) | (sidx == cur) | (sidx == cur - 1)
    vals = jnp.where(forced, FORCE_SCORE, imp)
    vals = jnp.where(sidx <= cur, vals, NEG_INF)

    def pick(_, rest):
        top = jnp.max(rest, axis=0, keepdims=True)
        first = jnp.min(jnp.where(rest == top, sidx, n_sel), axis=0, keepdims=True)
        return jnp.where(sidx == first, NEG_INF, rest)

    rest = lax.fori_loop(0, min(N_SELECT, n_sel), pick, vals)
    sel_buf[...] = jnp.where(rest < vals, 0.0, NEG_INF)

    def update(blocks, bias, vT_tiles, m, acc):
        part = None
        for blk, b in zip(blocks, bias):
            t = jnp.max(blk.reshape(blk.shape[0] // 8, 8, m_lanes), axis=0) + b
            part = t if part is None else jnp.maximum(part, t)
        m_new = jnp.maximum(m, jnp.max(part, axis=0, keepdims=True))
        m_safe = jnp.where(m_new == NEG_INF, 0.0, m_new)
        pb = jnp.concatenate([jnp.exp2(blk + (b - m_safe)) for blk, b in zip(blocks, bias)], axis=0).astype(BF16)
        pv = None
        for k, vT in enumerate(vT_tiles):
            t = _dot(vT, pb[k * KEY_TILE:(k + 1) * KEY_TILE])
            pv = t if pv is None else pv + t
        return m_new, jnp.exp2(m - m_safe) * acc + pv

    def empty_state(v_ref):
        return jnp.full((1, m_lanes), NEG_INF, F32), jnp.zeros((v_ref.shape[2], m_lanes), F32)

    def finish(acc):
        return acc[:e] / jnp.maximum(acc[e:e + 1], 1e-30)

    q_tiles = tq // KEY_TILE
    first_diag = i * q_tiles
    q_col = lane & (tq - 1)

    def tri_le(d):
        return jnp.where(row_k + d * KEY_TILE <= q_col, 0.0, NEG_INF)

    def tri_gt(d):
        return jnp.where(row_k + d * KEY_TILE > q_col, 0.0, NEG_INF)

    blocks_per_tile = KEY_TILE // SEL_LEN
    chunk_blocks = SEL_TILES * blocks_per_tile
    n_sel_chunks = ksel_ref.shape[2] // SEL_TILES

    def sel_scores(c):
        base = jnp.minimum(c, n_sel_chunks - 1) * SEL_TILES
        ks = jnp.concatenate([ksel_ref[0, 0, base + k] for k in range(SEL_TILES)], axis=0)
        return _dot(ks, qTm)

    def block_rows(first_block, n):
        return [jnp.concatenate([sel_buf[pl.ds(first_block + h, 1), :]] * GQA_REP, axis=1) for h in range(n)]

    def sel_chunk(sq_ref, c, state):
        s = sq_ref[...]
        blocks = [s[h * SEL_LEN:(h + 1) * SEL_LEN] for h in range(chunk_blocks)]
        rows = block_rows(c * chunk_blocks, chunk_blocks)
        bias = [rows[h] + jnp.where(c * SEL_TILES + h // blocks_per_tile < first_diag, 0.0, NEG_INF)
                for h in range(chunk_blocks)]
        return update(blocks, bias, [vTsel_ref[0, c * SEL_TILES + k, :, :] for k in range(SEL_TILES)], *state)

    def sel_pair(cp, state):
        c0 = 2 * cp
        sq1_buf[...] = sel_scores(c0 + 1)
        state = sel_chunk(sq0_buf, c0, state)
        sq0_buf[...] = sel_scores(c0 + 2)
        return sel_chunk(sq1_buf, c0 + 1, state)

    n_before = (first_diag + SEL_TILES - 1) // SEL_TILES
    n_pairs = n_before // 2
    sq0_buf[...] = sel_scores(0)
    state = lax.fori_loop(0, n_pairs, sel_pair, empty_state(vTsel_ref))
    state = lax.cond(n_before % 2 == 1, lambda st: sel_chunk(sq0_buf, 2 * n_pairs, st), lambda st: st, state)
    kd = jnp.concatenate([ksel_ref[0, 0, first_diag + d] for d in range(q_tiles)], axis=0)
    s_d = _dot(kd, qTm) + jnp.concatenate([tri_le(d) for d in range(q_tiles)], axis=0)
    _, acc_s = update([s_d[h * SEL_LEN:(h + 1) * SEL_LEN] for h in range(q_tiles * blocks_per_tile)],
                      block_rows(first_diag * blocks_per_tile, q_tiles * blocks_per_tile),
                      [vTsel_ref[0, first_diag + d, :, :] for d in range(q_tiles)], *state)
    o_sel = finish(acc_s)

    n_back = WIN_LEN // KEY_TILE
    n_win = n_back + q_tiles
    first_tile = first_diag - n_back
    kw = jnp.concatenate([kwin_ref[0, 0, jnp.maximum(first_tile + d, 0)] for d in range(n_win)], axis=0)
    s_w = _dot(kw, qTm)
    w_blocks = [s_w[d * KEY_TILE:(d + 1) * KEY_TILE] for d in range(n_win)]
    for d in range(q_tiles):
        w_blocks[d] = w_blocks[d] + tri_gt(d)
        w_blocks[n_back + d] = w_blocks[n_back + d] + tri_le(d)
    w_bias = [jnp.where(first_tile + d >= 0, 0.0, NEG_INF) for d in range(n_win)]
    _, acc_w = update(w_blocks, w_bias,
                      [vTwin_ref[0, jnp.maximum(first_tile + d, 0), :, :] for d in range(n_win)],
                      *empty_state(vTwin_ref))
    o_win = finish(acc_w)

    def gate(branch):
        g = gT_ref[0]
        return jnp.concatenate([g[branch * GQA_REP + r:branch * GQA_REP + r + 1, :] for r in range(GQA_REP)], axis=1)

    oT = o_cmp * gate(0) + o_sel * gate(1) + o_win * gate(2)
    o_rows = jnp.concatenate([oT[:, r * tq:(r + 1) * tq] for r in range(GQA_REP)], axis=0)
    o_ref[0] = o_rows.T.astype(BF16)


def _overlap_tiles(n_sel, n_cmp_pad):
    cs = np.arange(n_cmp_pad)[None, :] * CMP_STRIDE
    ss = np.arange(n_sel)[:, None] * SEL_LEN
    ov = ((cs < ss + SEL_LEN) & (cs + CMP_LEN > ss)).astype(np.float32)
    ov = ov.reshape(n_sel, n_cmp_pad // KEY_TILE, KEY_TILE).transpose(1, 0, 2)
    return jnp.asarray(ov, BF16)


def _nsa_attention(qT, kc, vcT, ksel, vTsel, kwin, vTwin, gT):
    B, D, S = qT.shape
    G, E = N_KV_HEADS, HEAD_DIM
    n_sel = S // SEL_LEN
    n_tiles = S // KEY_TILE
    n_chunks = kc.shape[2] // KEY_TILE
    kc5 = kc.reshape(B, G, n_chunks, KEY_TILE, E)
    ksel5 = ksel.reshape(B, G, n_tiles, KEY_TILE, E)
    kwin5 = kwin.reshape(B, G, n_tiles, KEY_TILE, E)
    ov = _overlap_tiles(n_sel, n_chunks * KEY_TILE)
    cmp_tiles = 2 if n_chunks % 2 == 0 else 1
    kspec = pl.BlockSpec((1, 1, n_tiles, KEY_TILE, E), lambda b, g, i: (b, g, 0, 0, 0))
    vspec = pl.BlockSpec((1, n_tiles, V_EXT, KEY_TILE), lambda b, g, i: (b, 0, g, 0))
    m_lanes = GQA_REP * Q_TILE
    return pl.pallas_call(
        _nsa_attn_kernel,
        out_shape=jax.ShapeDtypeStruct((B, S, D), BF16),
        grid=(B, G, S // Q_TILE),
        in_specs=[
            pl.BlockSpec((1, GQA_REP * E, Q_TILE), lambda b, g, i: (b, g, i)),
            pl.BlockSpec((1, 1, n_chunks, KEY_TILE, E), lambda b, g, i: (b, g, 0, 0, 0)),
            pl.BlockSpec((1, 1, n_chunks, E, KEY_TILE), lambda b, g, i: (b, g, 0, 0, 0)),
            pl.BlockSpec((n_chunks, n_sel, KEY_TILE), lambda b, g, i: (0, 0, 0)),
            kspec, vspec, kspec, vspec,
            pl.BlockSpec((1, 16, Q_TILE), lambda b, g, i: (b, g, i)),
        ],
        out_specs=pl.BlockSpec((1, Q_TILE, GQA_REP * E), lambda b, g, i: (b, i, g)),
        scratch_shapes=[
            pltpu.VMEM((n_chunks // cmp_tiles, cmp_tiles * KEY_TILE, m_lanes), F32),
            pltpu.VMEM((n_sel, m_lanes), F32),
            pltpu.VMEM((n_sel, Q_TILE), F32),
            pltpu.VMEM((SEL_TILES * KEY_TILE, m_lanes), F32),
            pltpu.VMEM((SEL_TILES * KEY_TILE, m_lanes), F32),
        ],
        compiler_params=_cparams(("parallel", "parallel", "arbitrary")),
        name="nsa_attention",
    )(qT, kc5, vcT, ov, ksel5, vTsel, kwin5, vTwin, gT)


def _layer_norm(z, g, b):
    mu = jnp.mean(z, axis=-1, keepdims=True)
    d = z - mu
    var = jnp.mean(d * d, axis=-1, keepdims=True)
    return d * lax.rsqrt(var + LN_EPS) * g + b


def _proj_ln_kernel(o_ref, x_ref, w_ref, gate_ref, g_ref, b_ref, sh_ref, sc_ref, rw_ref,
                    x1_ref, h_ref, lgT_ref):
    y = _dot(o_ref[0], w_ref[...])
    xn = _layer_norm(DN_ALPHA * x_ref[0] + gate_ref[0] * y, g_ref[...], b_ref[...])
    x1_ref[0] = xn
    h = xn * (1.0 + sc_ref[0]) + sh_ref[0]
    h_ref[0] = h
    lgT_ref[0] = lax.dot_general(rw_ref[...], h, (((1,), (1,)), ((), ())),
                                 preferred_element_type=F32, precision=HIGHEST)


def _proj_ln(o, x, w_o, gate, ln_g, ln_b, shift2, scale2, router_w):
    B, S, D = x.shape
    tm = ROW_TILE
    rw = router_w.T
    vec = pl.BlockSpec((1, 1, D), lambda b, i: (b, 0, 0))
    par = pl.BlockSpec((1, D), lambda b, i: (0, 0))
    row = pl.BlockSpec((1, tm, D), lambda b, i: (b, i, 0))
    return pl.pallas_call(
        _proj_ln_kernel,
        out_shape=[
            jax.ShapeDtypeStruct((B, S, D), F32),
            jax.ShapeDtypeStruct((B, S, D), F32),
            jax.ShapeDtypeStruct((B, N_EXPERTS, S), F32),
        ],
        grid=(B, S // tm),
        in_specs=[row, row, pl.BlockSpec((D, D), lambda b, i: (0, 0)), vec, par, par, vec, vec,
                  pl.BlockSpec((N_EXPERTS, D), lambda b, i: (0, 0))],
        out_specs=[row, row, pl.BlockSpec((1, N_EXPERTS, tm), lambda b, i: (b, 0, i))],
        compiler_params=_cparams(("parallel", "parallel")),
        name="proj_ln",
    )(o, x, w_o.astype(BF16), gate, ln_g.reshape(1, D), ln_b.reshape(1, D), shift2, scale2, rw)


def _first_max(v, idx, big):
    top = jnp.max(v, axis=0, keepdims=True)
    first = jnp.min(jnp.where(v == top, idx, big), axis=0, keepdims=True)
    return top, first


def _route_kernel(lg_ref, rb_ref, e_ref, w_ref):
    scores = jax.nn.sigmoid(lg_ref[0])
    biased = scores + rb_ref[...]
    eidx = lax.broadcasted_iota(jnp.int32, scores.shape, 0)
    npg = EXPERTS_PER_GROUP
    best_v, best_g = None, None
    for g in range(N_GROUPS):
        v = biased[g * npg:(g + 1) * npg]
        ii = g * npg + lax.broadcasted_iota(jnp.int32, v.shape, 0)
        top1, i1 = _first_max(v, ii, N_EXPERTS)
        top2 = jnp.max(jnp.where(ii == i1, NEG_INF, v), axis=0, keepdims=True)
        gs = top1 + top2
        if g == 0:
            best_v, best_g = gs, jnp.zeros_like(i1)
        else:
            better = gs > best_v
            best_g = jnp.where(better, g, best_g)
            best_v = jnp.where(better, gs, best_v)
    masked = jnp.where(eidx // npg == best_g, biased, NEG_INF)
    _, e1 = _first_max(masked, eidx, N_EXPERTS)
    _, e2 = _first_max(jnp.where(eidx == e1, NEG_INF, masked), eidx, N_EXPERTS)
    sc1 = jnp.sum(jnp.where(eidx == e1, scores, 0.0), axis=0, keepdims=True)
    sc2 = jnp.sum(jnp.where(eidx == e2, scores, 0.0), axis=0, keepdims=True)
    tot = sc1 + sc2
    e_ref[0] = jnp.concatenate([e1, e2], axis=0)
    w_ref[0] = jnp.concatenate([sc1 / tot, sc2 / tot], axis=0)


def _route(lgT, router_b):
    B, E, S = lgT.shape
    tn = min(S, 2048)
    return pl.pallas_call(
        _route_kernel,
        out_shape=[jax.ShapeDtypeStruct((B, TOP_K, S), jnp.int32), jax.ShapeDtypeStruct((B, TOP_K, S), F32)],
        grid=(B, S // tn),
        in_specs=[pl.BlockSpec((1, E, tn), lambda b, i: (b, 0, i)), pl.BlockSpec((E, 1), lambda b, i: (0, 0))],
        out_specs=[pl.BlockSpec((1, TOP_K, tn), lambda b, i: (b, 0, i))] * 2,
        compiler_params=_cparams(("parallel", "parallel")),
        name="moe_route",
    )(lgT, router_b.reshape(E, 1))


def _dispatch_plan(eidx):
    B, K, S = eidx.shape
    n_asg = B * S * K
    e_flat = eidx.transpose(0, 2, 1).reshape(n_asg)
    chunk = LANES
    onehot = (e_flat[:, None] == jnp.arange(N_EXPERTS, dtype=jnp.int32)[None, :]).astype(F32)
    oh = onehot.reshape(n_asg // chunk, chunk, N_EXPERTS)
    tri = jnp.tril(jnp.ones((chunk, chunk), F32))
    within = jnp.einsum("ij,cjk->cik", tri, oh)
    chunk_tot = within[:, -1, :]
    chunk_end = jnp.cumsum(chunk_tot, axis=0)
    incl = within + (chunk_end - chunk_tot)[:, None, :]
    rank = (jnp.sum(incl * oh, axis=-1) - 1.0).reshape(n_asg).astype(jnp.int32)
    counts = chunk_end[-1].astype(jnp.int32)
    padded = (counts + MOE_BLOCK - 1) // MOE_BLOCK * MOE_BLOCK
    pad_ends = jnp.cumsum(padded)
    dest = (pad_ends - padded)[e_flat] + rank
    n_blk = n_asg // MOE_BLOCK + N_EXPERTS
    blk_start = jnp.arange(n_blk, dtype=jnp.int32) * MOE_BLOCK
    blk_exp = jnp.minimum(jnp.sum((pad_ends[None, :] <= blk_start[:, None]).astype(jnp.int32), axis=1),
                          N_EXPERTS - 1)
    return blk_exp, dest


def _moe_dispatch_kernel(n_steps, dest_ref, x_ref, xs_init_ref, xs_hbm, stage, sems):
    del xs_init_ref
    i = pl.program_id(0)
    tm = x_ref.shape[0]
    slot = i & 1

    def drain(s):
        for _ in range(TOP_K):
            pltpu.make_async_copy(stage.at[s], xs_hbm.at[pl.ds(0, tm), :], sems.at[s]).wait()

    @pl.when(i >= 2)
    def _():
        drain(slot)

    stage[slot] = x_ref[...]

    def issue(r, c):
        for k in range(TOP_K):
            pltpu.make_async_copy(stage.at[slot, pl.ds(r, 1), :],
                                  xs_hbm.at[pl.ds(dest_ref[0, 0, TOP_K * r + k], 1), :],
                                  sems.at[slot]).start(priority=k % 2)
        return c

    lax.fori_loop(0, tm, issue, 0, unroll=8)

    @pl.when(i == n_steps - 1)
    def _():
        drain(slot)
        if n_steps >= 2:
            drain(1 - slot)


def _moe_dispatch(h2d, dest, n_rows):
    n_tok, D = h2d.shape
    tm = MOE_IO_TILE
    n_steps = n_tok // tm
    return pl.pallas_call(
        functools.partial(_moe_dispatch_kernel, n_steps),
        out_shape=jax.ShapeDtypeStruct((n_rows, D), F32),
        grid=(n_steps,),
        in_specs=[
            pl.BlockSpec((1, 1, TOP_K * tm), lambda i: (i, 0, 0), memory_space=pltpu.SMEM),
            pl.BlockSpec((tm, D), lambda i: (i, 0)),
            pl.BlockSpec(memory_space=pl.ANY),
        ],
        out_specs=pl.BlockSpec(memory_space=pl.ANY),
        scratch_shapes=[pltpu.VMEM((2, tm, D), F32), pltpu.SemaphoreType.DMA((2,))],
        input_output_aliases={2: 0},
        compiler_params=_cparams(("arbitrary",)),
        name="moe_dispatch",
    )(dest.reshape(n_steps, 1, TOP_K * tm), h2d, jnp.zeros((n_rows, D), F32))


def _moe_expert_kernel(blk_exp_ref, x_ref, wg_ref, wu_ref, wd_ref, y_ref):
    del blk_exp_ref
    x = x_ref[...].astype(BF16)
    gate = _dot(x, wg_ref[0])
    up = _dot(x, wu_ref[0])
    hid = (gate * jax.nn.sigmoid(gate) * up).astype(BF16)
    y_ref[...] = _dot(hid, wd_ref[0])


def _moe_experts(xs, blk_exp, w_gate, w_up, w_down):
    n_rows, D = xs.shape
    F = w_gate.shape[2]
    rows = pl.BlockSpec((MOE_BLOCK, D), lambda i, be: (i, 0))
    grid_spec = pltpu.PrefetchScalarGridSpec(
        num_scalar_prefetch=1,
        grid=(n_rows // MOE_BLOCK,),
        in_specs=[
            rows,
            pl.BlockSpec((1, D, F), lambda i, be: (be[i], 0, 0)),
            pl.BlockSpec((1, D, F), lambda i, be: (be[i], 0, 0)),
            pl.BlockSpec((1, F, D), lambda i, be: (be[i], 0, 0)),
        ],
        out_specs=rows,
    )
    return pl.pallas_call(
        _moe_expert_kernel,
        out_shape=jax.ShapeDtypeStruct((n_rows, D), F32),
        grid_spec=grid_spec,
        compiler_params=_cparams(("arbitrary",)),
        name="moe_experts",
    )(blk_exp, xs, w_gate.astype(BF16), w_up.astype(BF16), w_down.astype(BF16))


def _moe_combine_kernel(n_steps, dcur_ref, dnext_ref, w_ref, x_ref, gate_ref, g_ref, b_ref, ys_hbm,
                        o_ref, gbuf, sems):
    i = pl.program_id(0)
    tm = x_ref.shape[0]
    slot = i & 1

    def issue(d_ref, s):
        def body(r, c):
            for k in range(TOP_K):
                pltpu.make_async_copy(ys_hbm.at[pl.ds(d_ref[0, 0, TOP_K * r + k], 1), :],
                                      gbuf.at[s, k, pl.ds(r, 1), :], sems.at[s]).start(priority=k % 2)
            return c
        lax.fori_loop(0, tm, body, 0, unroll=8)

    @pl.when(i == 0)
    def _():
        issue(dcur_ref, slot)

    @pl.when(i + 1 < n_steps)
    def _():
        issue(dnext_ref, 1 - slot)

    for k in range(TOP_K):
        pltpu.make_async_copy(ys_hbm.at[pl.ds(0, tm), :], gbuf.at[slot, k], sems.at[slot]).wait()
    w = w_ref[...]
    y = gbuf[slot, 0] * w[:, 0:1] + gbuf[slot, 1] * w[:, 1:2]
    o_ref[...] = _layer_norm(DN_ALPHA * x_ref[...] + gate_ref[0] * y, g_ref[...], b_ref[...])


def _moe_combine_ln(ys, dest, wts, x, gate, ln_g, ln_b):
    B, S, D = x.shape
    n_tok = B * S
    tm = MOE_IO_TILE
    n_steps = n_tok // tm
    per_b = S // tm
    d3 = dest.reshape(n_steps, 1, TOP_K * tm)
    w2 = wts.transpose(0, 2, 1).reshape(n_tok, TOP_K)
    idx = lambda f: pl.BlockSpec((1, 1, TOP_K * tm), f, memory_space=pltpu.SMEM)
    par = pl.BlockSpec((1, D), lambda i: (0, 0))
    row = pl.BlockSpec((tm, D), lambda i: (i, 0))
    out = pl.pallas_call(
        functools.partial(_moe_combine_kernel, n_steps),
        out_shape=jax.ShapeDtypeStruct((n_tok, D), F32),
        grid=(n_steps,),
        in_specs=[
            idx(lambda i: (i, 0, 0)),
            idx(lambda i: (jnp.minimum(i + 1, n_steps - 1), 0, 0)),
            pl.BlockSpec((tm, TOP_K), lambda i: (i, 0)),
            row,
            pl.BlockSpec((1, 1, D), lambda i: (i // per_b, 0, 0)),
            par, par,
            pl.BlockSpec(memory_space=pl.ANY),
        ],
        out_specs=row,
        scratch_shapes=[pltpu.VMEM((2, TOP_K, tm, D), F32), pltpu.SemaphoreType.DMA((2,))],
        compiler_params=_cparams(("arbitrary",)),
        name="moe_combine_ln",
    )(d3, d3, w2, x.reshape(n_tok, D), gate, ln_g.reshape(1, D), ln_b.reshape(1, D), ys)
    return out.reshape(B, S, D)


def _moe_sublayer(x1, h2, lgT, router_b, w_gate, w_up, w_down, gate, ln_g, ln_b):
    B, S, D = x1.shape
    eidx, wts = _route(lgT, router_b)
    blk_exp, dest = _dispatch_plan(eidx)
    n_rows = blk_exp.shape[0] * MOE_BLOCK
    xs = _moe_dispatch(h2.reshape(B * S, D), dest, n_rows)
    ys = _moe_experts(xs, blk_exp, w_gate, w_up, w_down)
    return _moe_combine_ln(ys, dest, wts, x1, gate, ln_g, ln_b)


def _dil_weight_cols():
    d = D_MODEL
    cols = list(range(d))
    for which in range(2):
        for p in range(len(DIL_PATTERNS)):
            base = d + (p * 2 + which) * KV_COLS
            cols += list(range(base, base + KV_COLS))
    return np.asarray(cols)


def _dil_inproj_kernel(x_ref, sh_ref, sc_ref, w_ref, c_ref, s1_ref, s2_ref, q_ref, *kv_refs):
    h = (x_ref[0] * (1.0 + sc_ref[0]) + sh_ref[0]).astype(BF16)
    c, s1, s2 = c_ref[0], s1_ref[0], s2_ref[0]
    d = q_ref.shape[2]
    w = KV_COLS
    n_pat = len(DIL_PATTERNS)
    for j in range(d // w):
        a = _rope_cols(_dot(h, w_ref[:, j * w:(j + 1) * w]), c, s1, s2) * Q_SCALE
        q_ref[0, :, j * w:(j + 1) * w] = a.astype(BF16)
    for p in range(n_pat):
        a = _rope_cols(_dot(h, w_ref[:, d + p * w:d + (p + 1) * w]), c, s1, s2)
        kv_refs[p][0] = a.astype(BF16)
    for p in range(n_pat):
        a = _dot(h, w_ref[:, d + (n_pat + p) * w:d + (n_pat + p + 1) * w])
        kv_refs[n_pat + p][0] = a.astype(BF16)


def _dil_inproj(x, shift, scale, w_in, tabs):
    B, S, D = x.shape
    tm = ROW_TILE
    wp = _permute_cols(w_in, _dil_weight_cols())
    ncol = wp.shape[1]
    n_pat = len(DIL_PATTERNS)
    vec = pl.BlockSpec((1, 1, D), lambda b, i: (b, 0, 0))
    tab = pl.BlockSpec((1, tm, LANES), lambda b, i: (b, i, 0))
    kvspec = pl.BlockSpec((1, tm, KV_COLS), lambda b, i: (b, i, 0))
    outs = pl.pallas_call(
        _dil_inproj_kernel,
        out_shape=[jax.ShapeDtypeStruct((B, S, D), BF16)]
        + [jax.ShapeDtypeStruct((B, S, KV_COLS), BF16)] * (2 * n_pat),
        grid=(B, S // tm),
        in_specs=[pl.BlockSpec((1, tm, D), lambda b, i: (b, i, 0)), vec, vec,
                  pl.BlockSpec((D, ncol), lambda b, i: (0, 0)), tab, tab, tab],
        out_specs=[pl.BlockSpec((1, tm, D), lambda b, i: (b, i, 0))] + [kvspec] * (2 * n_pat),
        compiler_params=_cparams(("parallel", "parallel")),
        name="dil_inproj",
    )(x, shift, scale, wp, *tabs)
    return outs[0], outs[1:1 + n_pat], outs[1 + n_pat:]


def _dil_attn_kernel(steps, first, last, *refs):
    if first:
        q_ref, kc_ref, kp_ref, vc_ref, vp_ref = refs[:5]
        acc_in = ml_in = None
        outs = refs[5:]
    else:
        q_ref, kc_ref, kp_ref, vc_ref, vp_ref, acc_in, ml_in = refs[:7]
        outs = refs[7:]
    nb = pl.program_id(2)
    blk = DIL_BLOCK
    e = HEAD_DIM
    gw = GQA_REP * e
    kj = lax.broadcasted_iota(jnp.int32, (2 * blk, blk), 0)
    qi = lax.broadcasted_iota(jnp.int32, (2 * blk, blk), 1)
    dist = blk + qi - kj
    valid = (dist >= 0) & (dist <= steps) & ((nb - 1) * blk + kj >= 0)
    bias = jnp.where(valid, 0.0, NEG_INF)
    bias = jnp.concatenate([bias] * GQA_REP, axis=1)
    zeros_half = jnp.zeros((e, GQA_REP * blk), F32)
    mlT_old = None if first else ml_in[0].T
    m_rows, l_rows = [], []

    def heads_to_lanes(t):
        return jnp.concatenate([t[r * e:(r + 1) * e] for r in range(GQA_REP)], axis=1)

    for g in range(N_KV_HEADS):
        seg = slice((g // 2) * LANES, (g // 2 + 1) * LANES)
        qs = slice(g * gw, (g + 1) * gw)
        qTm = heads_to_lanes(q_ref[0][:, qs].astype(F32).T)
        qT2 = jnp.concatenate([qTm, zeros_half] if g % 2 == 0 else [zeros_half, qTm], axis=0).astype(BF16)
        kcat = jnp.concatenate([kp_ref[0][:, seg], kc_ref[0][:, seg]], axis=0)
        s = _dot(kcat, qT2) + bias
        m_new = jnp.max(s, axis=0, keepdims=True)
        if not first:
            m_old = jnp.concatenate([mlT_old[g * GQA_REP + r:g * GQA_REP + r + 1] for r in range(GQA_REP)], axis=1)
            l_old = jnp.concatenate([mlT_old[N_Q_HEADS + g * GQA_REP + r:N_Q_HEADS + g * GQA_REP + r + 1]
                                     for r in range(GQA_REP)], axis=1)
            m_new = jnp.maximum(m_old, m_new)
            alpha = jnp.exp2(m_old - m_new)
        p = jnp.exp2(s - m_new)
        l_new = jnp.sum(p, axis=0, keepdims=True)
        vcat = jnp.concatenate([vp_ref[0][:, seg], vc_ref[0][:, seg]], axis=0).astype(F32)
        vT = vcat.T[(g % 2) * e:(g % 2 + 1) * e].astype(BF16)
        accT = _dot(vT, p.astype(BF16))
        if not first:
            l_new = l_new + alpha * l_old
            accT = accT + alpha * heads_to_lanes(acc_in[0][:, qs].T)
        if last:
            accT = accT / l_new
        o_rows = jnp.concatenate([accT[:, r * blk:(r + 1) * blk] for r in range(GQA_REP)], axis=0)
        outs[0][0, :, qs] = o_rows.T.astype(outs[0].dtype)
        m_rows += [m_new[:, r * blk:(r + 1) * blk] for r in range(GQA_REP)]
        l_rows += [l_new[:, r * blk:(r + 1) * blk] for r in range(GQA_REP)]
    if not last:
        pad = jnp.zeros((LANES - 2 * N_Q_HEADS, blk), F32)
        outs[1][0] = jnp.concatenate(m_rows + l_rows + [pad], axis=0).T


def _dil_attention(q, kds, vds):
    B, S, D = q.shape
    n_pat = len(DIL_PATTERNS)
    acc = ml = None
    for p, (window, dil) in enumerate(DIL_PATTERNS):
        first, last = p == 0, p == n_pat - 1
        L = S // dil
        nblk = L // DIL_BLOCK
        kw = KV_COLS

        def view(a):
            return a.reshape(B, L, dil * a.shape[2])

        cur = lambda b, c, n: (b, n, c)
        prev = lambda b, c, n: (b, jnp.maximum(n - 1, 0), c)
        qspec = pl.BlockSpec((1, DIL_BLOCK, D), cur)
        in_specs = [qspec, pl.BlockSpec((1, DIL_BLOCK, kw), cur), pl.BlockSpec((1, DIL_BLOCK, kw), prev),
                    pl.BlockSpec((1, DIL_BLOCK, kw), cur), pl.BlockSpec((1, DIL_BLOCK, kw), prev)]
        args = [view(q), view(kds[p]), view(kds[p]), view(vds[p]), view(vds[p])]
        if not first:
            in_specs += [qspec, pl.BlockSpec((1, DIL_BLOCK, LANES), cur)]
            args += [view(acc), view(ml)]
        if last:
            out_shape = [jax.ShapeDtypeStruct((B, L, dil * D), BF16)]
            out_specs = [qspec]
        else:
            out_shape = [jax.ShapeDtypeStruct((B, L, dil * D), F32),
                         jax.ShapeDtypeStruct((B, L, dil * LANES), F32)]
            out_specs = [qspec, pl.BlockSpec((1, DIL_BLOCK, LANES), cur)]
        res = pl.pallas_call(
            functools.partial(_dil_attn_kernel, window // dil, first, last),
            out_shape=out_shape,
            grid=(B, dil, nblk),
            in_specs=in_specs,
            out_specs=out_specs,
            compiler_params=_cparams(("parallel", "parallel", "arbitrary")),
            name=f"dil_attention_{p}",
        )(*args)
        if last:
            return res[0].reshape(B, S, D)
        acc, ml = res[0].reshape(B, S, D), res[1].reshape(B, S, LANES)


def kernel(x, c, positions, ada_w, ada_b, ln_g, ln_b, nsa_w_in, nsa_cmp_pos_k, nsa_cmp_w1_k, nsa_cmp_w2_k, nsa_cmp_pos_v, nsa_cmp_w1_v, nsa_cmp_w2_v, nsa_w_o, dil_w_in, dil_w_o, router_w, router_b, moe_w_gate, moe_w_up, moe_w_down):
    B, S, D = x.shape
    mods = _ada_mods(c, ada_w, ada_b)
    def mod(i, sub):
        m = mods[i * 2 + sub]
        return [m[:, k * D:(k + 1) * D].reshape(B, 1, D) for k in range(3)]
    tabs = _rope_tables(positions)

    for i in range(DEPTH):
        shift, scale, gate = mod(i, 0)
        shift2, scale2, gate2 = mod(i, 1)
        j = i // 2
        if i % 2 == 0:
            qT, ksel, kwin, vTsel, vTwin, kcmp, vcmp, gT = _nsa_inproj(x, shift, scale, nsa_w_in[j], tabs)
            kc, vcT = _compress(kcmp, vcmp, nsa_cmp_pos_k[j], nsa_cmp_w1_k[j], nsa_cmp_w2_k[j],
                                nsa_cmp_pos_v[j], nsa_cmp_w1_v[j], nsa_cmp_w2_v[j], tabs)
            o = _nsa_attention(qT, kc, vcT, ksel, vTsel, kwin, vTwin, gT)
            w_o = nsa_w_o[j]
        else:
            q, kds, vds = _dil_inproj(x, shift, scale, dil_w_in[j], tabs)
            o = _dil_attention(q, kds, vds)
            w_o = dil_w_o[j]
        x1, h2, lgT = _proj_ln(o, x, w_o, gate, ln_g[i, 0], ln_b[i, 0], shift2, scale2, router_w)
        x = _moe_sublayer(x1, h2, lgT, router_b, moe_w_gate[i], moe_w_up[i], moe_w_down[i],
                          gate2, ln_g[i, 1], ln_b[i, 1])
    return x
```

```python
import functools

import numpy as np
import jax
import jax.numpy as jnp
from jax import lax
from jax.experimental import pallas as pl
from jax.experimental.pallas import tpu as pltpu

F32 = jnp.float32
BF16 = jnp.bfloat16
HIGHEST = lax.Precision.HIGHEST
NEG_INF = float("-inf")

D_MODEL = 1024
DEPTH = 2
HEAD_DIM = 64
N_Q_HEADS = D_MODEL // HEAD_DIM
N_KV_HEADS = 4
GQA_REP = N_Q_HEADS // N_KV_HEADS
ROPE_DIM = HEAD_DIM // 4
ROPE_THETA = 500000.0
ATTN_SCALE = HEAD_DIM ** -0.5
LOG2_E = 1.4426950408889634
Q_SCALE = ATTN_SCALE * LOG2_E
KV_COLS = N_KV_HEADS * HEAD_DIM
N_BRANCH = 3
CMP_LEN = 32
CMP_STRIDE = 16
CMP_HIDDEN = 256
SEL_LEN = 64
N_SELECT = 16
WIN_LEN = 512
FORCE_SCORE = 1.0e4
DIL_PATTERNS = ((128, 1), (512, 4), (2048, 16))
DIL_BLOCK = 128
N_EXPERTS = 32
N_GROUPS = 4
EXPERTS_PER_GROUP = N_EXPERTS // N_GROUPS
TOP_K = 2
D_EXPERT = 512
MOE_BLOCK = 128
DN_ALPHA = (2.0 * DEPTH) ** 0.25
LN_EPS = 1e-5

LANES = 128
VMEM_LIMIT_BYTES = 48 * 1024 * 1024

Q_TILE = 256
KEY_TILE = 128
SEL_TILES = 4
V_EXT = HEAD_DIM + 16
ROW_TILE = 512
MOE_IO_TILE = 256


def _cparams(semantics):
    return pltpu.CompilerParams(dimension_semantics=semantics, vmem_limit_bytes=VMEM_LIMIT_BYTES)


def _dot(a, b):
    return jnp.dot(a, b, preferred_element_type=F32)


def _dot_nt(a, b):
    return lax.dot_general(a, b, (((1,), (1,)), ((), ())), preferred_element_type=F32)


def _ada_kernel(c_ref, w_ref, b_ref, o_ref):
    c = c_ref[...]
    cond = c * jax.nn.sigmoid(c)
    o_ref[0] = jnp.dot(cond, w_ref[0], preferred_element_type=F32, precision=HIGHEST) + b_ref[0]


def _ada_mods(c, ada_w, ada_b):
    B, D = c.shape
    n_sub = ada_w.shape[0] * ada_w.shape[1]
    w = ada_w.reshape(n_sub, D, 3 * D)
    b = ada_b.reshape(n_sub, 1, 3 * D)
    c8 = jnp.zeros((8, D), F32).at[:B].set(c)
    tn = 768
    out = pl.pallas_call(
        _ada_kernel,
        out_shape=jax.ShapeDtypeStruct((n_sub, 8, 3 * D), F32),
        grid=(n_sub, 3 * D // tn),
        in_specs=[
            pl.BlockSpec((8, D), lambda s, j: (0, 0)),
            pl.BlockSpec((1, D, tn), lambda s, j: (s, 0, j)),
            pl.BlockSpec((1, 1, tn), lambda s, j: (s, 0, j)),
        ],
        out_specs=pl.BlockSpec((1, 8, tn), lambda s, j: (s, 0, j)),
        compiler_params=_cparams(("parallel", "parallel")),
        name="ada_mods",
    )(c8, w, b)
    return out[:, :B]


def _rope_tab_kernel(pos_ref, inv_ref, sg1_ref, sg2_ref, c_ref, s1_ref, s2_ref):
    ang = pos_ref[0] * inv_ref[...]
    sin = jnp.sin(ang)
    c_ref[0] = jnp.cos(ang)
    s1_ref[0] = sin * sg1_ref[...]
    s2_ref[0] = sin * sg2_ref[...]


def _rope_tables(positions):
    B, S = positions.shape
    half = ROPE_DIM // 2
    inv = ROPE_THETA ** (-jnp.arange(half, dtype=F32) * (2.0 / ROPE_DIM))
    li = np.arange(LANES) % HEAD_DIM
    in_rope = li < ROPE_DIM
    inv_row = jnp.where(jnp.asarray(in_rope), inv[li % half], 0.0).reshape(1, LANES)
    sg1 = jnp.asarray(np.where(li < half, -1.0, 0.0), F32).reshape(1, LANES)
    sg2 = jnp.asarray(np.where((li >= half) & in_rope, 1.0, 0.0), F32).reshape(1, LANES)
    pos = positions.astype(F32).reshape(B, S, 1)
    tm = min(S, 2048)
    row = pl.BlockSpec((1, LANES), lambda b, i: (0, 0))
    tab = pl.BlockSpec((1, tm, LANES), lambda b, i: (b, i, 0))
    return pl.pallas_call(
        _rope_tab_kernel,
        out_shape=[jax.ShapeDtypeStruct((B, S, LANES), F32)] * 3,
        grid=(B, S // tm),
        in_specs=[pl.BlockSpec((1, tm, 1), lambda b, i: (b, i, 0)), row, row, row],
        out_specs=[tab, tab, tab],
        compiler_params=_cparams(("parallel", "parallel")),
        name="rope_tables",
    )(pos, inv_row, sg1, sg2)


def _rope128(t, c, s1, s2):
    return t * c + pltpu.roll(t, LANES - ROPE_DIM // 2, 1) * s1 + pltpu.roll(t, ROPE_DIM // 2, 1) * s2


def _rope_cols(a, c, s1, s2):
    n = a.shape[1] // LANES
    return jnp.concatenate(
        [_rope128(a[:, k * LANES:(k + 1) * LANES], c, s1, s2) for k in range(n)], axis=1)


def _nsa_weight_cols():
    d = D_MODEL
    def kv(branch, which):
        base = d + (branch * 2 + which) * KV_COLS
        return list(range(base, base + KV_COLS))
    cols = list(range(d))
    cols += kv(1, 0) + kv(2, 0) + kv(1, 1) + kv(2, 1) + kv(0, 0) + kv(0, 1)
    gate0 = d + N_BRANCH * 2 * KV_COLS
    gcols = [-1] * LANES
    for g in range(N_KV_HEADS):
        for br in range(N_BRANCH):
            for r in range(GQA_REP):
                gcols[g * 16 + br * GQA_REP + r] = gate0 + (g * GQA_REP + r) * N_BRANCH + br
    return np.asarray(cols + gcols)


def _permute_cols(w, cols):
    picked = w[:, np.maximum(cols, 0)]
    return jnp.where(jnp.asarray(cols >= 0)[None, :], picked, 0.0).astype(BF16)


def _nsa_inproj_kernel(x_ref, sh_ref, sc_ref, w_ref, c_ref, s1_ref, s2_ref,
                       qT_ref, ksel_ref, kwin_ref, vTsel_ref, vTwin_ref, kcmp_ref, vcmp_ref, gT_ref):
    tm = x_ref.shape[1]
    h = (x_ref[0] * (1.0 + sc_ref[0]) + sh_ref[0]).astype(BF16)
    c, s1, s2 = c_ref[0], s1_ref[0], s2_ref[0]
    w = KV_COLS

    def proj(j, n=w):
        return _dot(h, w_ref[:, j * w:j * w + n])

    for j in range(4):
        a = _rope_cols(proj(j), c, s1, s2) * Q_SCALE
        qT_ref[0, j * w:(j + 1) * w, :] = a.T.astype(BF16)
    for j, ref in ((4, ksel_ref), (5, kwin_ref)):
        a = _rope_cols(proj(j), c, s1, s2)
        for g in range(N_KV_HEADS):
            ref[0, g] = a[:, g * HEAD_DIM:(g + 1) * HEAD_DIM].astype(BF16)
    ones_rows = jnp.ones((V_EXT - HEAD_DIM, KEY_TILE), BF16)
    for j, ref in ((6, vTsel_ref), (7, vTwin_ref)):
        aT = proj(j).T.astype(BF16)
        for k in range(tm // KEY_TILE):
            for g in range(N_KV_HEADS):
                ref[0, k, g * V_EXT:g * V_EXT + HEAD_DIM] = aT[g * HEAD_DIM:(g + 1) * HEAD_DIM,
                                                               k * KEY_TILE:(k + 1) * KEY_TILE]
                ref[0, k, g * V_EXT + HEAD_DIM:(g + 1) * V_EXT] = ones_rows
    for j, ref in ((8, kcmp_ref), (9, vcmp_ref)):
        a = proj(j)
        for g in range(N_KV_HEADS):
            ref[0, g] = a[:, g * HEAD_DIM:(g + 1) * HEAD_DIM].astype(BF16)
    gates = jax.nn.sigmoid(proj(10, LANES))
    gT_ref[0] = gates.T[:4 * 16]


def _nsa_inproj(x, shift, scale, w_in, tabs):
    B, S, D = x.shape
    tm = ROW_TILE
    wp = _permute_cols(w_in, _nsa_weight_cols())
    ncol = wp.shape[1]
    vec = pl.BlockSpec((1, 1, D), lambda b, i: (b, 0, 0))
    tab = pl.BlockSpec((1, tm, LANES), lambda b, i: (b, i, 0))
    nat = pl.BlockSpec((1, N_KV_HEADS, tm, HEAD_DIM), lambda b, i: (b, 0, i, 0))
    vt = pl.BlockSpec((1, tm // KEY_TILE, N_KV_HEADS * V_EXT, KEY_TILE), lambda b, i: (b, i, 0, 0))
    nat_shape = jax.ShapeDtypeStruct((B, N_KV_HEADS, S, HEAD_DIM), BF16)
    vt_shape = jax.ShapeDtypeStruct((B, S // KEY_TILE, N_KV_HEADS * V_EXT, KEY_TILE), BF16)
    return pl.pallas_call(
        _nsa_inproj_kernel,
        out_shape=[
            jax.ShapeDtypeStruct((B, D, S), BF16),
            nat_shape, nat_shape,
            vt_shape, vt_shape,
            nat_shape, nat_shape,
            jax.ShapeDtypeStruct((B, 4 * 16, S), F32),
        ],
        grid=(B, S // tm),
        in_specs=[
            pl.BlockSpec((1, tm, D), lambda b, i: (b, i, 0)), vec, vec,
            pl.BlockSpec((D, ncol), lambda b, i: (0, 0)), tab, tab, tab,
        ],
        out_specs=[
            pl.BlockSpec((1, D, tm), lambda b, i: (b, 0, i)),
            nat, nat, vt, vt, nat, nat,
            pl.BlockSpec((1, 4 * 16, tm), lambda b, i: (b, 0, i)),
        ],
        compiler_params=_cparams(("parallel", "parallel")),
        name="nsa_inproj",
    )(x, shift, scale, wp, *tabs)


def _compress_kernel(xk_ref, xv_ref, w1k_ref, w1v_ref, pk_ref, pv_ref, w2k_ref, w2vT_ref,
                     c_ref, s1_ref, s2_ref, kc_ref, vcT_ref):
    n = xk_ref.shape[2]
    half = w1k_ref.shape[0] // 2

    def hidden(x_ref, w1_ref, p_ref):
        x = x_ref[0, 0]
        first = _dot(x, w1_ref[:half])
        second = _dot(x, w1_ref[half:])
        bias = _dot(p_ref[...], w1_ref[...])[0:1]
        hid = first + pltpu.roll(second, n - 1, 0) + bias
        return jax.nn.gelu(hid).astype(BF16)

    kc = _dot(hidden(xk_ref, w1k_ref, pk_ref), w2k_ref[...])
    kc = _rope128(kc, c_ref[0], s1_ref[0], s2_ref[0])
    row = lax.broadcasted_iota(jnp.int32, kc.shape, 0)
    kc = jnp.where(row < n - 1, kc, 0.0)
    kc_ref[0, 0] = kc[:, :HEAD_DIM].astype(BF16)

    vcT = _dot_nt(w2vT_ref[...], hidden(xv_ref, w1v_ref, pv_ref))
    col = lax.broadcasted_iota(jnp.int32, vcT.shape, 1)
    vcT = jnp.where(col < n - 1, vcT, 0.0).astype(BF16)
    for k in range(n // KEY_TILE):
        vcT_ref[0, 0, k] = vcT[:, k * KEY_TILE:(k + 1) * KEY_TILE]


def _compress(kcmp, vcmp, pos_k, w1_k, w2_k, pos_v, w1_v, w2_v, tabs):
    B, G, S, E = kcmp.shape
    n = S // CMP_STRIDE
    wide = CMP_STRIDE * E
    xk = kcmp.reshape(B, G, n, wide)
    xv = vcmp.reshape(B, G, n, wide)
    def flat8(p):
        return jnp.zeros((8, CMP_LEN * E), BF16).at[0].set(p.reshape(-1).astype(BF16))
    w2k = jnp.zeros((CMP_HIDDEN, LANES), BF16).at[:, :E].set(w2_k.astype(BF16))
    w2vT = w2_v.T.astype(BF16)
    last = CMP_LEN - 1
    ctabs = [jnp.zeros((B, n, LANES), F32).at[:, :n - 1].set(t[:, last::CMP_STRIDE][:, :n - 1]) for t in tabs]
    xspec = pl.BlockSpec((1, 1, n, wide), lambda b, g: (b, g, 0, 0))
    w1spec = pl.BlockSpec((CMP_LEN * E, CMP_HIDDEN), lambda b, g: (0, 0))
    pspec = pl.BlockSpec((8, CMP_LEN * E), lambda b, g: (0, 0))
    tspec = pl.BlockSpec((1, n, LANES), lambda b, g: (b, 0, 0))
    return pl.pallas_call(
        _compress_kernel,
        out_shape=[
            jax.ShapeDtypeStruct((B, G, n, E), BF16),
            jax.ShapeDtypeStruct((B, G, n // KEY_TILE, E, KEY_TILE), BF16),
        ],
        grid=(B, G),
        in_specs=[xspec, xspec, w1spec, w1spec, pspec, pspec,
                  pl.BlockSpec((CMP_HIDDEN, LANES), lambda b, g: (0, 0)),
                  pl.BlockSpec((E, CMP_HIDDEN), lambda b, g: (0, 0)),
                  tspec, tspec, tspec],
        out_specs=[
            pl.BlockSpec((1, 1, n, E), lambda b, g: (b, g, 0, 0)),
            pl.BlockSpec((1, 1, n // KEY_TILE, E, KEY_TILE), lambda b, g: (b, g, 0, 0, 0)),
        ],
        compiler_params=_cparams(("parallel", "parallel")),
        name="nsa_compress",
    )(xk, xv, w1_k.astype(BF16), w1_v.astype(BF16), flat8(pos_k), flat8(pos_v), w2k, w2vT, *ctabs)


def _nsa_attn_kernel(qT_ref, kc_ref, vcT_ref, ov_ref, ksel_ref, vTsel_ref, kwin_ref, vTwin_ref, gT_ref,
                     o_ref, s_buf, imp_buf, sel_buf, sq0_buf, sq1_buf):
    i = pl.program_id(2)
    tq = Q_TILE
    m_lanes = GQA_REP * tq
    e = HEAD_DIM
    t0 = i * tq
    n_chunks = kc_ref.shape[2]
    n_sel = sel_buf.shape[0]
    ov_per_tile, ov_rows = _overlap_window(n_sel)

    qT = qT_ref[0]
    qTm = jnp.concatenate([qT[r * e:(r + 1) * e] for r in range(GQA_REP)], axis=1)
    lane = lax.broadcasted_iota(jnp.int32, (1, m_lanes), 1)
    tok = t0 + (lane & (tq - 1))
    row_k = lax.broadcasted_iota(jnp.int32, (KEY_TILE, 1), 0)

    cmp_tiles = s_buf.shape[1] // KEY_TILE
    cmp_rows = cmp_tiles * KEY_TILE
    last_cmp_tile = ((t0 + tq - CMP_LEN) // CMP_STRIDE) // KEY_TILE
    n_vis = jnp.minimum(last_cmp_tile // cmp_tiles + 1, n_chunks // cmp_tiles)
    row_c = lax.broadcasted_iota(jnp.int32, (cmp_rows, 1), 0)

    def cmp_scores(c, m):
        kc = jnp.concatenate([kc_ref[0, 0, c * cmp_tiles + k] for k in range(cmp_tiles)], axis=0)
        last_tok = (c * cmp_rows + row_c) * CMP_STRIDE + (CMP_LEN - 1)
        s = jnp.where(last_tok <= tok, _dot(kc, qTm), NEG_INF)
        s_buf[c] = s
        return jnp.maximum(m, jnp.max(s, axis=0, keepdims=True))

    m_c = lax.fori_loop(0, n_vis, cmp_scores, jnp.full((1, m_lanes), NEG_INF, F32))
    m_c = jnp.where(m_c == NEG_INF, 0.0, m_c)

    imp_buf[...] = jnp.zeros(imp_buf.shape, F32)

    def cmp_accum(c, carry):
        l, acc = carry
        p = jnp.exp2(s_buf[c] - m_c)
        pb = p.astype(BF16)
        for k in range(cmp_tiles):
            tile = c * cmp_tiles + k
            pk = pb[k * KEY_TILE:(k + 1) * KEY_TILE]
            rows = pl.ds(pl.multiple_of(jnp.minimum(tile * ov_per_tile, n_sel - ov_rows), 16), ov_rows)
            imp_buf[rows, :] += _dot(ov_ref[tile], pk)
            acc = acc + _dot(vcT_ref[0, 0, tile], pk)
        return l + jnp.sum(p, axis=0, keepdims=True), acc

    l_c, acc_c = lax.fori_loop(
        0, n_vis, cmp_accum, (jnp.zeros((1, m_lanes), F32), jnp.zeros((e, m_lanes), F32)))
    inv_l = 1.0 / jnp.maximum(l_c, 1e-30)
    o_cmp = acc_c * inv_l
    imp_n = imp_buf[...] * inv_l
    imp = imp_n[:, 0:tq]
    for r in range(1, GQA_REP):
        imp = imp + imp_n[:, r * tq:(r + 1) * tq]

    sidx = lax.broadcasted_iota(jnp.int32, (n_sel, tq), 0)
    cur = (t0 + lax.broadcasted_iota(jnp.int32, (1, tq), 1)) // SEL_LEN
    n_forced = 3
    forced = (sidx == 0) | (sidx == cur) | (sidx == cur - 1)
    vals = jnp.where((sidx <= cur) & jnp.logical_not(forced), imp, NEG_INF)

    def pick(_, rest):
        top = jnp.max(rest, axis=0, keepdims=True)
        first = jnp.min(jnp.where(rest == top, sidx, n_sel), axis=0, keepdims=True)
        return jnp.where(sidx == first, NEG_INF, rest)

    rest = lax.fori_loop(0, min(N_SELECT, n_sel) - n_forced, pick, vals)
    sel_buf[...] = jnp.where(forced | (rest < vals), 0.0, NEG_INF)

    def update(blocks, bias, vT_tiles, m, acc):
        part = None
        for blk, b in zip(blocks, bias):
            t = jnp.max(blk.reshape(blk.shape[0] // 8, 8, m_lanes), axis=0) + b
            part = t if part is None else jnp.maximum(part, t)
        m_new = jnp.maximum(m, jnp.max(part, axis=0, keepdims=True))
        m_safe = jnp.where(m_new == NEG_INF, 0.0, m_new)
        pb = jnp.concatenate([jnp.exp2(blk + (b - m_safe)) for blk, b in zip(blocks, bias)], axis=0).astype(BF16)
        pv = None
        for k, vT in enumerate(vT_tiles):
            t = _dot(vT, pb[k * KEY_TILE:(k + 1) * KEY_TILE])
            pv = t if pv is None else pv + t
        return m_new, jnp.exp2(m - m_safe) * acc + pv

    def empty_state(v_ref):
        return jnp.full((1, m_lanes), NEG_INF, F32), jnp.zeros((v_ref.shape[2], m_lanes), F32)

    def finish(acc):
        return acc[:e] / jnp.maximum(acc[e:e + 1], 1e-30)

    q_tiles = tq // KEY_TILE
    first_diag = i * q_tiles
    q_col = lane & (tq - 1)

    def tri_le(d):
        return jnp.where(row_k + d * KEY_TILE <= q_col, 0.0, NEG_INF)

    def tri_gt(d):
        return jnp.where(row_k + d * KEY_TILE > q_col, 0.0, NEG_INF)

    blocks_per_tile = KEY_TILE // SEL_LEN
    chunk_blocks = SEL_TILES * blocks_per_tile
    n_sel_chunks = ksel_ref.shape[2] // SEL_TILES

    def sel_scores(c):
        base = jnp.minimum(c, n_sel_chunks - 1) * SEL_TILES
        ks = jnp.concatenate([ksel_ref[0, 0, base + k] for k in range(SEL_TILES)], axis=0)
        return _dot(ks, qTm)

    def block_rows(first_block, n):
        return [jnp.concatenate([sel_buf[pl.ds(first_block + h, 1), :]] * GQA_REP, axis=1) for h in range(n)]

    def sel_chunk(sq_ref, c, state):
        s = sq_ref[...]
        blocks = [s[h * SEL_LEN:(h + 1) * SEL_LEN] for h in range(chunk_blocks)]
        rows = block_rows(c * chunk_blocks, chunk_blocks)
        bias = [rows[h] + jnp.where(c * SEL_TILES + h // blocks_per_tile < first_diag, 0.0, NEG_INF)
                for h in range(chunk_blocks)]
        return update(blocks, bias, [vTsel_ref[0, c * SEL_TILES + k, :, :] for k in range(SEL_TILES)], *state)

    def sel_pair(cp, state):
        c0 = 2 * cp
        sq1_buf[...] = sel_scores(c0 + 1)
        state = sel_chunk(sq0_buf, c0, state)
        sq0_buf[...] = sel_scores(c0 + 2)
        return sel_chunk(sq1_buf, c0 + 1, state)

    def sel_quad(cq, state):
        return sel_pair(2 * cq + 1, sel_pair(2 * cq, state))

    n_before = (first_diag + SEL_TILES - 1) // SEL_TILES
    n_pairs = n_before // 2
    n_quads = n_pairs // 2
    sq0_buf[...] = sel_scores(0)
    state = lax.fori_loop(0, n_quads, sel_quad, empty_state(vTsel_ref))
    state = lax.fori_loop(2 * n_quads, n_pairs, sel_pair, state)
    state = lax.cond(n_before % 2 == 1, lambda st: sel_chunk(sq0_buf, 2 * n_pairs, st), lambda st: st, state)
    kd = jnp.concatenate([ksel_ref[0, 0, first_diag + d] for d in range(q_tiles)], axis=0)
    s_d = _dot(kd, qTm) + jnp.concatenate([tri_le(d) for d in range(q_tiles)], axis=0)
    _, acc_s = update([s_d[h * SEL_LEN:(h + 1) * SEL_LEN] for h in range(q_tiles * blocks_per_tile)],
                      block_rows(first_diag * blocks_per_tile, q_tiles * blocks_per_tile),
                      [vTsel_ref[0, first_diag + d, :, :] for d in range(q_tiles)], *state)
    o_sel = finish(acc_s)

    n_back = WIN_LEN // KEY_TILE
    n_win = n_back + q_tiles
    first_tile = first_diag - n_back
    kw = jnp.concatenate([kwin_ref[0, 0, jnp.maximum(first_tile + d, 0)] for d in range(n_win)], axis=0)
    s_w = _dot(kw, qTm)
    w_blocks = [s_w[d * KEY_TILE:(d + 1) * KEY_TILE] for d in range(n_win)]
    for d in range(q_tiles):
        w_blocks[d] = w_blocks[d] + tri_gt(d)
        w_blocks[n_back + d] = w_blocks[n_back + d] + tri_le(d)
    w_bias = [jnp.where(first_tile + d >= 0, 0.0, NEG_INF) for d in range(n_win)]
    _, acc_w = update(w_blocks, w_bias,
                      [vTwin_ref[0, jnp.maximum(first_tile + d, 0), :, :] for d in range(n_win)],
                      *empty_state(vTwin_ref))
    o_win = finish(acc_w)

    def gate(branch):
        g = gT_ref[0]
        return jnp.concatenate([g[branch * GQA_REP + r:branch * GQA_REP + r + 1, :] for r in range(GQA_REP)], axis=1)

    oT = o_cmp * gate(0) + o_sel * gate(1) + o_win * gate(2)
    o_rows = jnp.concatenate([oT[:, r * tq:(r + 1) * tq] for r in range(GQA_REP)], axis=0)
    o_ref[0] = o_rows.T.astype(BF16)


def _overlap_window(n_sel):
    per_tile = KEY_TILE * CMP_STRIDE // SEL_LEN
    return per_tile, min(per_tile + 16, n_sel)


def _overlap_tiles(n_sel, n_cmp_pad):
    cs = np.arange(n_cmp_pad)[None, :] * CMP_STRIDE
    ss = np.arange(n_sel)[:, None] * SEL_LEN
    ov = ((cs < ss + SEL_LEN) & (cs + CMP_LEN > ss)).astype(np.float32)
    per_tile, rows = _overlap_window(n_sel)
    tiles = []
    for k in range(n_cmp_pad // KEY_TILE):
        start = min(k * per_tile, n_sel - rows)
        cols = ov[:, k * KEY_TILE:(k + 1) * KEY_TILE]
        assert not cols[:start].any() and not cols[start + rows:].any()
        tiles.append(cols[start:start + rows])
    return jnp.asarray(np.stack(tiles), BF16)


def _nsa_attention(qT, kc, vcT, ksel, vTsel, kwin, vTwin, gT):
    B, D, S = qT.shape
    G, E = N_KV_HEADS, HEAD_DIM
    n_sel = S // SEL_LEN
    n_tiles = S // KEY_TILE
    n_chunks = kc.shape[2] // KEY_TILE
    kc5 = kc.reshape(B, G, n_chunks, KEY_TILE, E)
    ksel5 = ksel.reshape(B, G, n_tiles, KEY_TILE, E)
    kwin5 = kwin.reshape(B, G, n_tiles, KEY_TILE, E)
    ov = _overlap_tiles(n_sel, n_chunks * KEY_TILE)
    cmp_tiles = 2 if n_chunks % 2 == 0 else 1
    kspec = pl.BlockSpec((1, 1, n_tiles, KEY_TILE, E), lambda b, g, i: (b, g, 0, 0, 0))
    vspec = pl.BlockSpec((1, n_tiles, V_EXT, KEY_TILE), lambda b, g, i: (b, 0, g, 0))
    m_lanes = GQA_REP * Q_TILE
    return pl.pallas_call(
        _nsa_attn_kernel,
        out_shape=jax.ShapeDtypeStruct((B, S, D), BF16),
        grid=(B, G, S // Q_TILE),
        in_specs=[
            pl.BlockSpec((1, GQA_REP * E, Q_TILE), lambda b, g, i: (b, g, i)),
            pl.BlockSpec((1, 1, n_chunks, KEY_TILE, E), lambda b, g, i: (b, g, 0, 0, 0)),
            pl.BlockSpec((1, 1, n_chunks, E, KEY_TILE), lambda b, g, i: (b, g, 0, 0, 0)),
            pl.BlockSpec(ov.shape, lambda b, g, i: (0, 0, 0)),
            kspec, vspec, kspec, vspec,
            pl.BlockSpec((1, 16, Q_TILE), lambda b, g, i: (b, g, i)),
        ],
        out_specs=pl.BlockSpec((1, Q_TILE, GQA_REP * E), lambda b, g, i: (b, i, g)),
        scratch_shapes=[
            pltpu.VMEM((n_chunks // cmp_tiles, cmp_tiles * KEY_TILE, m_lanes), F32),
            pltpu.VMEM((n_sel, m_lanes), F32),
            pltpu.VMEM((n_sel, Q_TILE), F32),
            pltpu.VMEM((SEL_TILES * KEY_TILE, m_lanes), F32),
            pltpu.VMEM((SEL_TILES * KEY_TILE, m_lanes), F32),
        ],
        compiler_params=_cparams(("parallel", "parallel", "arbitrary")),
        name="nsa_attention",
    )(qT, kc5, vcT, ov, ksel5, vTsel, kwin5, vTwin, gT)


def _layer_norm(z, g, b):
    mu = jnp.mean(z, axis=-1, keepdims=True)
    d = z - mu
    var = jnp.mean(d * d, axis=-1, keepdims=True)
    return d * lax.rsqrt(var + LN_EPS) * g + b


def _proj_ln_kernel(o_ref, x_ref, w_ref, gate_ref, g_ref, b_ref, sh_ref, sc_ref, rw_ref,
                    x1_ref, h_ref, lgT_ref):
    y = _dot(o_ref[0], w_ref[...])
    xn = _layer_norm(DN_ALPHA * x_ref[0] + gate_ref[0] * y, g_ref[...], b_ref[...])
    x1_ref[0] = xn
    h = xn * (1.0 + sc_ref[0]) + sh_ref[0]
    h_ref[0] = h
    lgT_ref[0] = lax.dot_general(rw_ref[...], h, (((1,), (1,)), ((), ())),
                                 preferred_element_type=F32, precision=HIGHEST)


def _proj_ln(o, x, w_o, gate, ln_g, ln_b, shift2, scale2, router_w):
    B, S, D = x.shape
    tm = ROW_TILE
    rw = router_w.T
    vec = pl.BlockSpec((1, 1, D), lambda b, i: (b, 0, 0))
    par = pl.BlockSpec((1, D), lambda b, i: (0, 0))
    row = pl.BlockSpec((1, tm, D), lambda b, i: (b, i, 0))
    return pl.pallas_call(
        _proj_ln_kernel,
        out_shape=[
            jax.ShapeDtypeStruct((B, S, D), F32),
            jax.ShapeDtypeStruct((B, S, D), F32),
            jax.ShapeDtypeStruct((B, N_EXPERTS, S), F32),
        ],
        grid=(B, S // tm),
        in_specs=[row, row, pl.BlockSpec((D, D), lambda b, i: (0, 0)), vec, par, par, vec, vec,
                  pl.BlockSpec((N_EXPERTS, D), lambda b, i: (0, 0))],
        out_specs=[row, row, pl.BlockSpec((1, N_EXPERTS, tm), lambda b, i: (b, 0, i))],
        compiler_params=_cparams(("parallel", "parallel")),
        name="proj_ln",
    )(o, x, w_o.astype(BF16), gate, ln_g.reshape(1, D), ln_b.reshape(1, D), shift2, scale2, rw)


def _first_max(v, idx, big):
    top = jnp.max(v, axis=0, keepdims=True)
    first = jnp.min(jnp.where(v == top, idx, big), axis=0, keepdims=True)
    return top, first


def _route_kernel(lg_ref, rb_ref, e_ref, w_ref):
    scores = jax.nn.sigmoid(lg_ref[0])
    biased = scores + rb_ref[...]
    eidx = lax.broadcasted_iota(jnp.int32, scores.shape, 0)
    npg = EXPERTS_PER_GROUP
    best_v, best_g = None, None
    for g in range(N_GROUPS):
        v = biased[g * npg:(g + 1) * npg]
        ii = g * npg + lax.broadcasted_iota(jnp.int32, v.shape, 0)
        top1, i1 = _first_max(v, ii, N_EXPERTS)
        top2 = jnp.max(jnp.where(ii == i1, NEG_INF, v), axis=0, keepdims=True)
        gs = top1 + top2
        if g == 0:
            best_v, best_g = gs, jnp.zeros_like(i1)
        else:
            better = gs > best_v
            best_g = jnp.where(better, g, best_g)
            best_v = jnp.where(better, gs, best_v)
    masked = jnp.where(eidx // npg == best_g, biased, NEG_INF)
    _, e1 = _first_max(masked, eidx, N_EXPERTS)
    _, e2 = _first_max(jnp.where(eidx == e1, NEG_INF, masked), eidx, N_EXPERTS)
    sc1 = jnp.sum(jnp.where(eidx == e1, scores, 0.0), axis=0, keepdims=True)
    sc2 = jnp.sum(jnp.where(eidx == e2, scores, 0.0), axis=0, keepdims=True)
    tot = sc1 + sc2
    e_ref[0] = jnp.concatenate([e1, e2], axis=0)
    w_ref[0] = jnp.concatenate([sc1 / tot, sc2 / tot], axis=0)


def _route(lgT, router_b):
    B, E, S = lgT.shape
    tn = min(S, 2048)
    return pl.pallas_call(
        _route_kernel,
        out_shape=[jax.ShapeDtypeStruct((B, TOP_K, S), jnp.int32), jax.ShapeDtypeStruct((B, TOP_K, S), F32)],
        grid=(B, S // tn),
        in_specs=[pl.BlockSpec((1, E, tn), lambda b, i: (b, 0, i)), pl.BlockSpec((E, 1), lambda b, i: (0, 0))],
        out_specs=[pl.BlockSpec((1, TOP_K, tn), lambda b, i: (b, 0, i))] * 2,
        compiler_params=_cparams(("parallel", "parallel")),
        name="moe_route",
    )(lgT, router_b.reshape(E, 1))


def _dispatch_plan(eidx):
    B, K, S = eidx.shape
    n_asg = B * S * K
    e_flat = eidx.transpose(0, 2, 1).reshape(n_asg)
    chunk = LANES
    onehot = (e_flat[:, None] == jnp.arange(N_EXPERTS, dtype=jnp.int32)[None, :]).astype(F32)
    oh = onehot.reshape(n_asg // chunk, chunk, N_EXPERTS)
    tri = jnp.tril(jnp.ones((chunk, chunk), F32))
    within = jnp.einsum("ij,cjk->cik", tri, oh)
    chunk_tot = within[:, -1, :]
    chunk_end = jnp.cumsum(chunk_tot, axis=0)
    incl = within + (chunk_end - chunk_tot)[:, None, :]
    rank = (jnp.sum(incl * oh, axis=-1) - 1.0).reshape(n_asg).astype(jnp.int32)
    counts = chunk_end[-1].astype(jnp.int32)
    padded = (counts + MOE_BLOCK - 1) // MOE_BLOCK * MOE_BLOCK
    pad_ends = jnp.cumsum(padded)
    dest = (pad_ends - padded)[e_flat] + rank
    n_blk = n_asg // MOE_BLOCK + N_EXPERTS
    blk_start = jnp.arange(n_blk, dtype=jnp.int32) * MOE_BLOCK
    blk_exp = jnp.minimum(jnp.sum((pad_ends[None, :] <= blk_start[:, None]).astype(jnp.int32), axis=1),
                          N_EXPERTS - 1)
    return blk_exp, dest


def _moe_dispatch_kernel(n_steps, dest_ref, x_ref, xs_init_ref, xs_hbm, stage, sems):
    del xs_init_ref
    i = pl.program_id(0)
    tm = x_ref.shape[0]
    slot = i & 1

    def drain(s):
        for _ in range(TOP_K):
            pltpu.make_async_copy(stage.at[s], xs_hbm.at[pl.ds(0, tm), :], sems.at[s]).wait()

    @pl.when(i >= 2)
    def _():
        drain(slot)

    stage[slot] = x_ref[...]

    def issue(r, c):
        for k in range(TOP_K):
            pltpu.make_async_copy(stage.at[slot, pl.ds(r, 1), :],
                                  xs_hbm.at[pl.ds(dest_ref[0, 0, TOP_K * r + k], 1), :],
                                  sems.at[slot]).start(priority=k % 2)
        return c

    lax.fori_loop(0, tm, issue, 0, unroll=8)

    @pl.when(i == n_steps - 1)
    def _():
        drain(slot)
        if n_steps >= 2:
            drain(1 - slot)


def _moe_dispatch(h2d, dest, n_rows):
    n_tok, D = h2d.shape
    tm = MOE_IO_TILE
    n_steps = n_tok // tm
    return pl.pallas_call(
        functools.partial(_moe_dispatch_kernel, n_steps),
        out_shape=jax.ShapeDtypeStruct((n_rows, D), F32),
        grid=(n_steps,),
        in_specs=[
            pl.BlockSpec((1, 1, TOP_K * tm), lambda i: (i, 0, 0), memory_space=pltpu.SMEM),
            pl.BlockSpec((tm, D), lambda i: (i, 0)),
            pl.BlockSpec(memory_space=pl.ANY),
        ],
        out_specs=pl.BlockSpec(memory_space=pl.ANY),
        scratch_shapes=[pltpu.VMEM((2, tm, D), F32), pltpu.SemaphoreType.DMA((2,))],
        input_output_aliases={2: 0},
        compiler_params=_cparams(("arbitrary",)),
        name="moe_dispatch",
    )(dest.reshape(n_steps, 1, TOP_K * tm), h2d, jnp.zeros((n_rows, D), F32))


def _moe_expert_kernel(blk_exp_ref, x_ref, wg_ref, wu_ref, wd_ref, y_ref):
    del blk_exp_ref
    x = x_ref[...].astype(BF16)
    gate = _dot(x, wg_ref[0])
    up = _dot(x, wu_ref[0])
    hid = (gate * jax.nn.sigmoid(gate) * up).astype(BF16)
    y_ref[...] = _dot(hid, wd_ref[0])


def _moe_experts(xs, blk_exp, w_gate, w_up, w_down):
    n_rows, D = xs.shape
    F = w_gate.shape[2]
    rows = pl.BlockSpec((MOE_BLOCK, D), lambda i, be: (i, 0))
    grid_spec = pltpu.PrefetchScalarGridSpec(
        num_scalar_prefetch=1,
        grid=(n_rows // MOE_BLOCK,),
        in_specs=[
            rows,
            pl.BlockSpec((1, D, F), lambda i, be: (be[i], 0, 0)),
            pl.BlockSpec((1, D, F), lambda i, be: (be[i], 0, 0)),
            pl.BlockSpec((1, F, D), lambda i, be: (be[i], 0, 0)),
        ],
        out_specs=rows,
    )
    return pl.pallas_call(
        _moe_expert_kernel,
        out_shape=jax.ShapeDtypeStruct((n_rows, D), F32),
        grid_spec=grid_spec,
        compiler_params=_cparams(("arbitrary",)),
        name="moe_experts",
    )(blk_exp, xs, w_gate.astype(BF16), w_up.astype(BF16), w_down.astype(BF16))


def _moe_combine_kernel(n_steps, dcur_ref, dnext_ref, w_ref, x_ref, gate_ref, g_ref, b_ref, ys_hbm,
                        o_ref, gbuf, sems):
    i = pl.program_id(0)
    tm = x_ref.shape[0]
    slot = i & 1

    def issue(d_ref, s):
        def body(r, c):
            for k in range(TOP_K):
                pltpu.make_async_copy(ys_hbm.at[pl.ds(d_ref[0, 0, TOP_K * r + k], 1), :],
                                      gbuf.at[s, k, pl.ds(r, 1), :], sems.at[s]).start(priority=k % 2)
            return c
        lax.fori_loop(0, tm, body, 0, unroll=8)

    @pl.when(i == 0)
    def _():
        issue(dcur_ref, slot)

    @pl.when(i + 1 < n_steps)
    def _():
        issue(dnext_ref, 1 - slot)

    for k in range(TOP_K):
        pltpu.make_async_copy(ys_hbm.at[pl.ds(0, tm), :], gbuf.at[slot, k], sems.at[slot]).wait()
    w = w_ref[...]
    y = gbuf[slot, 0] * w[:, 0:1] + gbuf[slot, 1] * w[:, 1:2]
    o_ref[...] = _layer_norm(DN_ALPHA * x_ref[...] + gate_ref[0] * y, g_ref[...], b_ref[...])


def _moe_combine_ln(ys, dest, wts, x, gate, ln_g, ln_b):
    B, S, D = x.shape
    n_tok = B * S
    tm = MOE_IO_TILE
    n_steps = n_tok // tm
    per_b = S // tm
    d3 = dest.reshape(n_steps, 1, TOP_K * tm)
    w2 = wts.transpose(0, 2, 1).reshape(n_tok, TOP_K)
    idx = lambda f: pl.BlockSpec((1, 1, TOP_K * tm), f, memory_space=pltpu.SMEM)
    par = pl.BlockSpec((1, D), lambda i: (0, 0))
    row = pl.BlockSpec((tm, D), lambda i: (i, 0))
    out = pl.pallas_call(
        functools.partial(_moe_combine_kernel, n_steps),
        out_shape=jax.ShapeDtypeStruct((n_tok, D), F32),
        grid=(n_steps,),
        in_specs=[
            idx(lambda i: (i, 0, 0)),
            idx(lambda i: (jnp.minimum(i + 1, n_steps - 1), 0, 0)),
            pl.BlockSpec((tm, TOP_K), lambda i: (i, 0)),
            row,
            pl.BlockSpec((1, 1, D), lambda i: (i // per_b, 0, 0)),
            par, par,
            pl.BlockSpec(memory_space=pl.ANY),
        ],
        out_specs=row,
        scratch_shapes=[pltpu.VMEM((2, TOP_K, tm, D), F32), pltpu.SemaphoreType.DMA((2,))],
        compiler_params=_cparams(("arbitrary",)),
        name="moe_combine_ln",
    )(d3, d3, w2, x.reshape(n_tok, D), gate, ln_g.reshape(1, D), ln_b.reshape(1, D), ys)
    return out.reshape(B, S, D)


def _moe_sublayer(x1, h2, lgT, router_b, w_gate, w_up, w_down, gate, ln_g, ln_b):
    B, S, D = x1.shape
    eidx, wts = _route(lgT, router_b)
    blk_exp, dest = _dispatch_plan(eidx)
    n_rows = blk_exp.shape[0] * MOE_BLOCK
    xs = _moe_dispatch(h2.reshape(B * S, D), dest, n_rows)
    ys = _moe_experts(xs, blk_exp, w_gate, w_up, w_down)
    return _moe_combine_ln(ys, dest, wts, x1, gate, ln_g, ln_b)


def _dil_weight_cols():
    d = D_MODEL
    cols = list(range(d))
    for which in range(2):
        for p in range(len(DIL_PATTERNS)):
            base = d + (p * 2 + which) * KV_COLS
            cols += list(range(base, base + KV_COLS))
    return np.asarray(cols)


def _dil_inproj_kernel(x_ref, sh_ref, sc_ref, w_ref, c_ref, s1_ref, s2_ref, q_ref, *kv_refs):
    h = (x_ref[0] * (1.0 + sc_ref[0]) + sh_ref[0]).astype(BF16)
    c, s1, s2 = c_ref[0], s1_ref[0], s2_ref[0]
    d = q_ref.shape[2]
    w = KV_COLS
    n_pat = len(DIL_PATTERNS)
    for j in range(d // w):
        a = _rope_cols(_dot(h, w_ref[:, j * w:(j + 1) * w]), c, s1, s2) * Q_SCALE
        q_ref[0, :, j * w:(j + 1) * w] = a.astype(BF16)
    for p in range(n_pat):
        a = _rope_cols(_dot(h, w_ref[:, d + p * w:d + (p + 1) * w]), c, s1, s2)
        kv_refs[p][0] = a.astype(BF16)
    for p in range(n_pat):
        a = _dot(h, w_ref[:, d + (n_pat + p) * w:d + (n_pat + p + 1) * w])
        kv_refs[n_pat + p][0] = a.astype(BF16)


def _dil_inproj(x, shift, scale, w_in, tabs):
    B, S, D = x.shape
    tm = ROW_TILE
    wp = _permute_cols(w_in, _dil_weight_cols())
    ncol = wp.shape[1]
    n_pat = len(DIL_PATTERNS)
    vec = pl.BlockSpec((1, 1, D), lambda b, i: (b, 0, 0))
    tab = pl.BlockSpec((1, tm, LANES), lambda b, i: (b, i, 0))
    kvspec = pl.BlockSpec((1, tm, KV_COLS), lambda b, i: (b, i, 0))
    outs = pl.pallas_call(
        _dil_inproj_kernel,
        out_shape=[jax.ShapeDtypeStruct((B, S, D), BF16)]
        + [jax.ShapeDtypeStruct((B, S, KV_COLS), BF16)] * (2 * n_pat),
        grid=(B, S // tm),
        in_specs=[pl.BlockSpec((1, tm, D), lambda b, i: (b, i, 0)), vec, vec,
                  pl.BlockSpec((D, ncol), lambda b, i: (0, 0)), tab, tab, tab],
        out_specs=[pl.BlockSpec((1, tm, D), lambda b, i: (b, i, 0))] + [kvspec] * (2 * n_pat),
        compiler_params=_cparams(("parallel", "parallel")),
        name="dil_inproj",
    )(x, shift, scale, wp, *tabs)
    return outs[0], outs[1:1 + n_pat], outs[1 + n_pat:]


def _dil_attn_kernel(steps, first, last, *refs):
    if first:
        q_ref, kc_ref, kp_ref, vc_ref, vp_ref = refs[:5]
        acc_in = ml_in = None
        outs = refs[5:]
    else:
        q_ref, kc_ref, kp_ref, vc_ref, vp_ref, acc_in, ml_in = refs[:7]
        outs = refs[7:]
    nb = pl.program_id(2)
    blk = DIL_BLOCK
    e = HEAD_DIM
    gw = GQA_REP * e
    kj = lax.broadcasted_iota(jnp.int32, (2 * blk, blk), 0)
    qi = lax.broadcasted_iota(jnp.int32, (2 * blk, blk), 1)
    dist = blk + qi - kj
    valid = (dist >= 0) & (dist <= steps) & ((nb - 1) * blk + kj >= 0)
    bias = jnp.where(valid, 0.0, NEG_INF)
    bias = jnp.concatenate([bias] * GQA_REP, axis=1)
    zeros_half = jnp.zeros((e, GQA_REP * blk), F32)
    mlT_old = None if first else ml_in[0].T
    m_rows, l_rows = [], []

    def heads_to_lanes(t):
        return jnp.concatenate([t[r * e:(r + 1) * e] for r in range(GQA_REP)], axis=1)

    for g in range(N_KV_HEADS):
        seg = slice((g // 2) * LANES, (g // 2 + 1) * LANES)
        qs = slice(g * gw, (g + 1) * gw)
        qTm = heads_to_lanes(q_ref[0][:, qs].astype(F32).T)
        qT2 = jnp.concatenate([qTm, zeros_half] if g % 2 == 0 else [zeros_half, qTm], axis=0).astype(BF16)
        kcat = jnp.concatenate([kp_ref[0][:, seg], kc_ref[0][:, seg]], axis=0)
        s = _dot(kcat, qT2) + bias
        m_new = jnp.max(s, axis=0, keepdims=True)
        if not first:
            m_old = jnp.concatenate([mlT_old[g * GQA_REP + r:g * GQA_REP + r + 1] for r in range(GQA_REP)], axis=1)
            l_old = jnp.concatenate([mlT_old[N_Q_HEADS + g * GQA_REP + r:N_Q_HEADS + g * GQA_REP + r + 1]
                                     for r in range(GQA_REP)], axis=1)
            m_new = jnp.maximum(m_old, m_new)
            alpha = jnp.exp2(m_old - m_new)
        p = jnp.exp2(s - m_new)
        l_new = jnp.sum(p, axis=0, keepdims=True)
        vcat = jnp.concatenate([vp_ref[0][:, seg], vc_ref[0][:, seg]], axis=0).astype(F32)
        vT = vcat.T[(g % 2) * e:(g % 2 + 1) * e].astype(BF16)
        accT = _dot(vT, p.astype(BF16))
        if not first:
            l_new = l_new + alpha * l_old
            accT = accT + alpha * heads_to_lanes(acc_in[0][:, qs].T)
        if last:
            accT = accT / l_new
        o_rows = jnp.concatenate([accT[:, r * blk:(r + 1) * blk] for r in range(GQA_REP)], axis=0)
        outs[0][0, :, qs] = o_rows.T.astype(outs[0].dtype)
        m_rows += [m_new[:, r * blk:(r + 1) * blk] for r in range(GQA_REP)]
        l_rows += [l_new[:, r * blk:(r + 1) * blk] for r in range(GQA_REP)]
    if not last:
        pad = jnp.zeros((LANES - 2 * N_Q_HEADS, blk), F32)
        outs[1][0] = jnp.concatenate(m_rows + l_rows + [pad], axis=0).T


def _dil_attention(q, kds, vds):
    B, S, D = q.shape
    n_pat = len(DIL_PATTERNS)
    acc = ml = None
    for p, (window, dil) in enumerate(DIL_PATTERNS):
        first, last = p == 0, p == n_pat - 1
        L = S // dil
        nblk = L // DIL_BLOCK
        kw = KV_COLS

        def view(a):
            return a.reshape(B, L, dil * a.shape[2])

        cur = lambda b, c, n: (b, n, c)
        prev = lambda b, c, n: (b, jnp.maximum(n - 1, 0), c)
        qspec = pl.BlockSpec((1, DIL_BLOCK, D), cur)
        in_specs = [qspec, pl.BlockSpec((1, DIL_BLOCK, kw), cur), pl.BlockSpec((1, DIL_BLOCK, kw), prev),
                    pl.BlockSpec((1, DIL_BLOCK, kw), cur), pl.BlockSpec((1, DIL_BLOCK, kw), prev)]
        args = [view(q), view(kds[p]), view(kds[p]), view(vds[p]), view(vds[p])]
        if not first:
            in_specs += [qspec, pl.BlockSpec((1, DIL_BLOCK, LANES), cur)]
            args += [view(acc), view(ml)]
        if last:
            out_shape = [jax.ShapeDtypeStruct((B, L, dil * D), BF16)]
            out_specs = [qspec]
        else:
            out_shape = [jax.ShapeDtypeStruct((B, L, dil * D), F32),
                         jax.ShapeDtypeStruct((B, L, dil * LANES), F32)]
            out_specs = [qspec, pl.BlockSpec((1, DIL_BLOCK, LANES), cur)]
        res = pl.pallas_call(
            functools.partial(_dil_attn_kernel, window // dil, first, last),
            out_shape=out_shape,
            grid=(B, dil, nblk),
            in_specs=in_specs,
            out_specs=out_specs,
            compiler_params=_cparams(("parallel", "parallel", "arbitrary")),
            name=f"dil_attention_{p}",
        )(*args)
        if last:
            return res[0].reshape(B, S, D)
        acc, ml = res[0].reshape(B, S, D), res[1].reshape(B, S, LANES)


def kernel(x, c, positions, ada_w, ada_b, ln_g, ln_b, nsa_w_in, nsa_cmp_pos_k, nsa_cmp_w1_k, nsa_cmp_w2_k, nsa_cmp_pos_v, nsa_cmp_w1_v, nsa_cmp_w2_v, nsa_w_o, dil_w_in, dil_w_o, router_w, router_b, moe_w_gate, moe_w_up, moe_w_down):
    B, S, D = x.shape
    mods = _ada_mods(c, ada_w, ada_b)
    def mod(i, sub):
        m = mods[i * 2 + sub]
        return [m[:, k * D:(k + 1) * D].reshape(B, 1, D) for k in range(3)]
    tabs = _rope_tables(positions)

    for i in range(DEPTH):
        shift, scale, gate = mod(i, 0)
        shift2, scale2, gate2 = mod(i, 1)
        j = i // 2
        if i % 2 == 0:
            qT, ksel, kwin, vTsel, vTwin, kcmp, vcmp, gT = _nsa_inproj(x, shift, scale, nsa_w_in[j], tabs)
            kc, vcT = _compress(kcmp, vcmp, nsa_cmp_pos_k[j], nsa_cmp_w1_k[j], nsa_cmp_w2_k[j],
                                nsa_cmp_pos_v[j], nsa_cmp_w1_v[j], nsa_cmp_w2_v[j], tabs)
            o = _nsa_attention(qT, kc, vcT, ksel, vTsel, kwin, vTwin, gT)
            w_o = nsa_w_o[j]
        else:
            q, kds, vds = _dil_inproj(x, shift, scale, dil_w_in[j], tabs)
            o = _dil_attention(q, kds, vds)
            w_o = dil_w_o[j]
        x1, h2, lgT = _proj_ln(o, x, w_o, gate, ln_g[i, 0], ln_b[i, 0], shift2, scale2, router_w)
        x = _moe_sublayer(x1, h2, lgT, router_b, moe_w_gate[i], moe_w_up[i], moe_w_down[i],
                          gate2, ln_g[i, 1], ln_b[i, 1])
    return x
```

```python
import functools

import numpy as np
import jax
import jax.numpy as jnp
from jax import lax
from jax.experimental import pallas as pl
from jax.experimental.pallas import tpu as pltpu

F32 = jnp.float32
BF16 = jnp.bfloat16
HIGHEST = lax.Precision.HIGHEST
NEG_INF = float("-inf")

D_MODEL = 1024
DEPTH = 2
HEAD_DIM = 64
N_Q_HEADS = D_MODEL // HEAD_DIM
N_KV_HEADS = 4
GQA_REP = N_Q_HEADS // N_KV_HEADS
ROPE_DIM = HEAD_DIM // 4
ROPE_THETA = 500000.0
ATTN_SCALE = HEAD_DIM ** -0.5
LOG2_E = 1.4426950408889634
Q_SCALE = ATTN_SCALE * LOG2_E
KV_COLS = N_KV_HEADS * HEAD_DIM
N_BRANCH = 3
CMP_LEN = 32
CMP_STRIDE = 16
CMP_HIDDEN = 256
SEL_LEN = 64
N_SELECT = 16
WIN_LEN = 512
FORCE_SCORE = 1.0e4
DIL_PATTERNS = ((128, 1), (512, 4), (2048, 16))
DIL_BLOCK = 128
N_EXPERTS = 32
N_GROUPS = 4
EXPERTS_PER_GROUP = N_EXPERTS // N_GROUPS
TOP_K = 2
D_EXPERT = 512
MOE_BLOCK = 128
DN_ALPHA = (2.0 * DEPTH) ** 0.25
LN_EPS = 1e-5

LANES = 128
VMEM_LIMIT_BYTES = 48 * 1024 * 1024

Q_TILE = 256
KEY_TILE = 128
SEL_TILES = 4
V_EXT = HEAD_DIM + 16
DIL_Q_TILE = 256
ROW_TILE = 512
MOE_IO_TILE = 256


def _cparams(semantics):
    return pltpu.CompilerParams(dimension_semantics=semantics, vmem_limit_bytes=VMEM_LIMIT_BYTES)


def _dot(a, b):
    return jnp.dot(a, b, preferred_element_type=F32)


def _dot_nt(a, b):
    return lax.dot_general(a, b, (((1,), (1,)), ((), ())), preferred_element_type=F32)


def _ada_kernel(c_ref, w_ref, b_ref, o_ref):
    c = c_ref[...]
    cond = c * jax.nn.sigmoid(c)
    o_ref[0] = jnp.dot(cond, w_ref[0], preferred_element_type=F32, precision=HIGHEST) + b_ref[0]


def _ada_mods(c, ada_w, ada_b):
    B, D = c.shape
    n_sub = ada_w.shape[0] * ada_w.shape[1]
    w = ada_w.reshape(n_sub, D, 3 * D)
    b = ada_b.reshape(n_sub, 1, 3 * D)
    c8 = jnp.zeros((8, D), F32).at[:B].set(c)
    tn = 768
    out = pl.pallas_call(
        _ada_kernel,
        out_shape=jax.ShapeDtypeStruct((n_sub, 8, 3 * D), F32),
        grid=(n_sub, 3 * D // tn),
        in_specs=[
            pl.BlockSpec((8, D), lambda s, j: (0, 0)),
            pl.BlockSpec((1, D, tn), lambda s, j: (s, 0, j)),
            pl.BlockSpec((1, 1, tn), lambda s, j: (s, 0, j)),
        ],
        out_specs=pl.BlockSpec((1, 8, tn), lambda s, j: (s, 0, j)),
        compiler_params=_cparams(("parallel", "parallel")),
        name="ada_mods",
    )(c8, w, b)
    return out[:, :B]


def _rope_tab_kernel(pos_ref, inv_ref, sg1_ref, sg2_ref, c_ref, s1_ref, s2_ref):
    ang = pos_ref[0] * inv_ref[...]
    sin = jnp.sin(ang)
    c_ref[0] = jnp.cos(ang)
    s1_ref[0] = sin * sg1_ref[...]
    s2_ref[0] = sin * sg2_ref[...]


def _rope_tables(positions):
    B, S = positions.shape
    half = ROPE_DIM // 2
    inv = ROPE_THETA ** (-jnp.arange(half, dtype=F32) * (2.0 / ROPE_DIM))
    li = np.arange(LANES) % HEAD_DIM
    in_rope = li < ROPE_DIM
    inv_row = jnp.where(jnp.asarray(in_rope), inv[li % half], 0.0).reshape(1, LANES)
    sg1 = jnp.asarray(np.where(li < half, -1.0, 0.0), F32).reshape(1, LANES)
    sg2 = jnp.asarray(np.where((li >= half) & in_rope, 1.0, 0.0), F32).reshape(1, LANES)
    pos = positions.astype(F32).reshape(B, S, 1)
    tm = min(S, 2048)
    row = pl.BlockSpec((1, LANES), lambda b, i: (0, 0))
    tab = pl.BlockSpec((1, tm, LANES), lambda b, i: (b, i, 0))
    return pl.pallas_call(
        _rope_tab_kernel,
        out_shape=[jax.ShapeDtypeStruct((B, S, LANES), F32)] * 3,
        grid=(B, S // tm),
        in_specs=[pl.BlockSpec((1, tm, 1), lambda b, i: (b, i, 0)), row, row, row],
        out_specs=[tab, tab, tab],
        compiler_params=_cparams(("parallel", "parallel")),
        name="rope_tables",
    )(pos, inv_row, sg1, sg2)


def _rope128(t, c, s1, s2):
    return t * c + pltpu.roll(t, LANES - ROPE_DIM // 2, 1) * s1 + pltpu.roll(t, ROPE_DIM // 2, 1) * s2


def _rope_cols(a, c, s1, s2):
    n = a.shape[1] // LANES
    return jnp.concatenate(
        [_rope128(a[:, k * LANES:(k + 1) * LANES], c, s1, s2) for k in range(n)], axis=1)


def _nsa_weight_cols():
    d = D_MODEL
    def kv(branch, which):
        base = d + (branch * 2 + which) * KV_COLS
        return list(range(base, base + KV_COLS))
    cols = list(range(d))
    cols += kv(1, 0) + kv(2, 0) + kv(1, 1) + kv(2, 1) + kv(0, 0) + kv(0, 1)
    gate0 = d + N_BRANCH * 2 * KV_COLS
    gcols = [-1] * LANES
    for g in range(N_KV_HEADS):
        for br in range(N_BRANCH):
            for r in range(GQA_REP):
                gcols[g * 16 + br * GQA_REP + r] = gate0 + (g * GQA_REP + r) * N_BRANCH + br
    return np.asarray(cols + gcols)


def _permute_cols(w, cols):
    picked = w[:, np.maximum(cols, 0)]
    return jnp.where(jnp.asarray(cols >= 0)[None, :], picked, 0.0).astype(BF16)


def _nsa_inproj_kernel(x_ref, sh_ref, sc_ref, w_ref, c_ref, s1_ref, s2_ref,
                       qT_ref, ksel_ref, kwin_ref, vTsel_ref, vTwin_ref, kcmp_ref, vcmp_ref, gT_ref):
    tm = x_ref.shape[1]
    h = (x_ref[0] * (1.0 + sc_ref[0]) + sh_ref[0]).astype(BF16)
    c, s1, s2 = c_ref[0], s1_ref[0], s2_ref[0]
    w = KV_COLS

    def proj(j, n=w):
        return _dot(h, w_ref[:, j * w:j * w + n])

    for j in range(4):
        a = _rope_cols(proj(j), c, s1, s2) * Q_SCALE
        qT_ref[0, j * w:(j + 1) * w, :] = a.T.astype(BF16)
    for j, ref in ((4, ksel_ref), (5, kwin_ref)):
        a = _rope_cols(proj(j), c, s1, s2)
        for g in range(N_KV_HEADS):
            ref[0, g] = a[:, g * HEAD_DIM:(g + 1) * HEAD_DIM].astype(BF16)
    ones_rows = jnp.ones((V_EXT - HEAD_DIM, KEY_TILE), BF16)
    for j, ref in ((6, vTsel_ref), (7, vTwin_ref)):
        aT = proj(j).T.astype(BF16)
        for k in range(tm // KEY_TILE):
            for g in range(N_KV_HEADS):
                ref[0, k, g * V_EXT:g * V_EXT + HEAD_DIM] = aT[g * HEAD_DIM:(g + 1) * HEAD_DIM,
                                                               k * KEY_TILE:(k + 1) * KEY_TILE]
                ref[0, k, g * V_EXT + HEAD_DIM:(g + 1) * V_EXT] = ones_rows
    for j, ref in ((8, kcmp_ref), (9, vcmp_ref)):
        a = proj(j)
        for g in range(N_KV_HEADS):
            ref[0, g] = a[:, g * HEAD_DIM:(g + 1) * HEAD_DIM].astype(BF16)
    gates = jax.nn.sigmoid(proj(10, LANES))
    gT_ref[0] = gates.T[:4 * 16]


def _nsa_inproj(x, shift, scale, w_in, tabs):
    B, S, D = x.shape
    tm = ROW_TILE
    wp = _permute_cols(w_in, _nsa_weight_cols())
    ncol = wp.shape[1]
    vec = pl.BlockSpec((1, 1, D), lambda b, i: (b, 0, 0))
    tab = pl.BlockSpec((1, tm, LANES), lambda b, i: (b, i, 0))
    nat = pl.BlockSpec((1, N_KV_HEADS, tm, HEAD_DIM), lambda b, i: (b, 0, i, 0))
    vt = pl.BlockSpec((1, tm // KEY_TILE, N_KV_HEADS * V_EXT, KEY_TILE), lambda b, i: (b, i, 0, 0))
    nat_shape = jax.ShapeDtypeStruct((B, N_KV_HEADS, S, HEAD_DIM), BF16)
    vt_shape = jax.ShapeDtypeStruct((B, S // KEY_TILE, N_KV_HEADS * V_EXT, KEY_TILE), BF16)
    return pl.pallas_call(
        _nsa_inproj_kernel,
        out_shape=[
            jax.ShapeDtypeStruct((B, D, S), BF16),
            nat_shape, nat_shape,
            vt_shape, vt_shape,
            nat_shape, nat_shape,
            jax.ShapeDtypeStruct((B, 4 * 16, S), F32),
        ],
        grid=(B, S // tm),
        in_specs=[
            pl.BlockSpec((1, tm, D), lambda b, i: (b, i, 0)), vec, vec,
            pl.BlockSpec((D, ncol), lambda b, i: (0, 0)), tab, tab, tab,
        ],
        out_specs=[
            pl.BlockSpec((1, D, tm), lambda b, i: (b, 0, i)),
            nat, nat, vt, vt, nat, nat,
            pl.BlockSpec((1, 4 * 16, tm), lambda b, i: (b, 0, i)),
        ],
        compiler_params=_cparams(("parallel", "parallel")),
        name="nsa_inproj",
    )(x, shift, scale, wp, *tabs)


def _compress_kernel(xk_ref, xv_ref, w1k_ref, w1v_ref, pk_ref, pv_ref, w2k_ref, w2vT_ref,
                     c_ref, s1_ref, s2_ref, kc_ref, vcT_ref):
    n = xk_ref.shape[2]
    half = w1k_ref.shape[0] // 2

    def hidden(x_ref, w1_ref, p_ref):
        x = x_ref[0, 0]
        first = _dot(x, w1_ref[:half])
        second = _dot(x, w1_ref[half:])
        bias = _dot(p_ref[...], w1_ref[...])[0:1]
        hid = first + pltpu.roll(second, n - 1, 0) + bias
        return jax.nn.gelu(hid).astype(BF16)

    kc = _dot(hidden(xk_ref, w1k_ref, pk_ref), w2k_ref[...])
    kc = _rope128(kc, c_ref[0], s1_ref[0], s2_ref[0])
    row = lax.broadcasted_iota(jnp.int32, kc.shape, 0)
    kc = jnp.where(row < n - 1, kc, 0.0)
    kc_ref[0, 0] = kc[:, :HEAD_DIM].astype(BF16)

    vcT = _dot_nt(w2vT_ref[...], hidden(xv_ref, w1v_ref, pv_ref))
    col = lax.broadcasted_iota(jnp.int32, vcT.shape, 1)
    vcT = jnp.where(col < n - 1, vcT, 0.0).astype(BF16)
    for k in range(n // KEY_TILE):
        vcT_ref[0, 0, k] = vcT[:, k * KEY_TILE:(k + 1) * KEY_TILE]


def _compress(kcmp, vcmp, pos_k, w1_k, w2_k, pos_v, w1_v, w2_v, tabs):
    B, G, S, E = kcmp.shape
    n = S // CMP_STRIDE
    wide = CMP_STRIDE * E
    xk = kcmp.reshape(B, G, n, wide)
    xv = vcmp.reshape(B, G, n, wide)
    def flat8(p):
        return jnp.zeros((8, CMP_LEN * E), BF16).at[0].set(p.reshape(-1).astype(BF16))
    w2k = jnp.zeros((CMP_HIDDEN, LANES), BF16).at[:, :E].set(w2_k.astype(BF16))
    w2vT = w2_v.T.astype(BF16)
    last = CMP_LEN - 1
    ctabs = [jnp.zeros((B, n, LANES), F32).at[:, :n - 1].set(t[:, last::CMP_STRIDE][:, :n - 1]) for t in tabs]
    xspec = pl.BlockSpec((1, 1, n, wide), lambda b, g: (b, g, 0, 0))
    w1spec = pl.BlockSpec((CMP_LEN * E, CMP_HIDDEN), lambda b, g: (0, 0))
    pspec = pl.BlockSpec((8, CMP_LEN * E), lambda b, g: (0, 0))
    tspec = pl.BlockSpec((1, n, LANES), lambda b, g: (b, 0, 0))
    return pl.pallas_call(
        _compress_kernel,
        out_shape=[
            jax.ShapeDtypeStruct((B, G, n, E), BF16),
            jax.ShapeDtypeStruct((B, G, n // KEY_TILE, E, KEY_TILE), BF16),
        ],
        grid=(B, G),
        in_specs=[xspec, xspec, w1spec, w1spec, pspec, pspec,
                  pl.BlockSpec((CMP_HIDDEN, LANES), lambda b, g: (0, 0)),
                  pl.BlockSpec((E, CMP_HIDDEN), lambda b, g: (0, 0)),
                  tspec, tspec, tspec],
        out_specs=[
            pl.BlockSpec((1, 1, n, E), lambda b, g: (b, g, 0, 0)),
            pl.BlockSpec((1, 1, n // KEY_TILE, E, KEY_TILE), lambda b, g: (b, g, 0, 0, 0)),
        ],
        compiler_params=_cparams(("parallel", "parallel")),
        name="nsa_compress",
    )(xk, xv, w1_k.astype(BF16), w1_v.astype(BF16), flat8(pos_k), flat8(pos_v), w2k, w2vT, *ctabs)


def _nsa_attn_kernel(qT_ref, kc_ref, vcT_ref, ov_ref, ksel_ref, vTsel_ref, kwin_ref, vTwin_ref, gT_ref,
                     o_ref, s_buf, imp_buf, sel_buf, sq0_buf, sq1_buf):
    i = pl.program_id(2)
    tq = Q_TILE
    m_lanes = GQA_REP * tq
    e = HEAD_DIM
    t0 = i * tq
    n_chunks = kc_ref.shape[2]
    n_sel = sel_buf.shape[0]
    ov_per_tile, ov_rows = _overlap_window(n_sel)

    qT = qT_ref[0]
    qTm = jnp.concatenate([qT[r * e:(r + 1) * e] for r in range(GQA_REP)], axis=1)
    lane = lax.broadcasted_iota(jnp.int32, (1, m_lanes), 1)
    tok = t0 + (lane & (tq - 1))
    row_k = lax.broadcasted_iota(jnp.int32, (KEY_TILE, 1), 0)

    cmp_tiles = s_buf.shape[1] // KEY_TILE
    cmp_rows = cmp_tiles * KEY_TILE
    last_cmp_tile = ((t0 + tq - CMP_LEN) // CMP_STRIDE) // KEY_TILE
    n_vis = jnp.minimum(last_cmp_tile // cmp_tiles + 1, n_chunks // cmp_tiles)
    row_c = lax.broadcasted_iota(jnp.int32, (cmp_rows, 1), 0)

    def cmp_scores(c, m):
        kc = jnp.concatenate([kc_ref[0, 0, c * cmp_tiles + k] for k in range(cmp_tiles)], axis=0)
        last_tok = (c * cmp_rows + row_c) * CMP_STRIDE + (CMP_LEN - 1)
        s = jnp.where(last_tok <= tok, _dot(kc, qTm), NEG_INF)
        s_buf[c] = s
        return jnp.maximum(m, jnp.max(s, axis=0, keepdims=True))

    m_c = lax.fori_loop(0, n_vis, cmp_scores, jnp.full((1, m_lanes), NEG_INF, F32))
    m_c = jnp.where(m_c == NEG_INF, 0.0, m_c)

    imp_buf[...] = jnp.zeros(imp_buf.shape, F32)

    def cmp_accum(c, carry):
        l, acc = carry
        p = jnp.exp2(s_buf[c] - m_c)
        pb = p.astype(BF16)
        for k in range(cmp_tiles):
            tile = c * cmp_tiles + k
            pk = pb[k * KEY_TILE:(k + 1) * KEY_TILE]
            rows = pl.ds(pl.multiple_of(jnp.minimum(tile * ov_per_tile, n_sel - ov_rows), 16), ov_rows)
            imp_buf[rows, :] += _dot(ov_ref[tile], pk)
            acc = acc + _dot(vcT_ref[0, 0, tile], pk)
        return l + jnp.sum(p, axis=0, keepdims=True), acc

    l_c, acc_c = lax.fori_loop(
        0, n_vis, cmp_accum, (jnp.zeros((1, m_lanes), F32), jnp.zeros((e, m_lanes), F32)))
    inv_l = 1.0 / jnp.maximum(l_c, 1e-30)
    o_cmp = acc_c * inv_l
    imp_n = imp_buf[...] * inv_l
    imp = imp_n[:, 0:tq]
    for r in range(1, GQA_REP):
        imp = imp + imp_n[:, r * tq:(r + 1) * tq]

    sidx = lax.broadcasted_iota(jnp.int32, (n_sel, tq), 0)
    cur = (t0 + lax.broadcasted_iota(jnp.int32, (1, tq), 1)) // SEL_LEN
    n_forced = 3
    forced = (sidx == 0) | (sidx == cur) | (sidx == cur - 1)
    vals = jnp.where((sidx <= cur) & jnp.logical_not(forced), imp, NEG_INF)

    def pick(_, rest):
        idx = lax.broadcasted_iota(jnp.int32, rest.shape, 0)
        top = jnp.max(rest, axis=0, keepdims=True)
        first = jnp.min(jnp.where(rest == top, idx, n_sel), axis=0, keepdims=True)
        return jnp.where(idx == first, NEG_INF, rest)

    n_rounds = min(N_SELECT, n_sel) - n_forced
    half = n_sel // 2
    if half % 8 == 0:
        def pick_all(v):
            return lax.fori_loop(0, n_rounds, pick, v)

        def pick_first_half(v):
            return jnp.concatenate([lax.fori_loop(0, n_rounds, pick, v[:half]), v[half:]], axis=0)

        rest = lax.cond((t0 + tq - 1) // SEL_LEN < half, pick_first_half, pick_all, vals)
    else:
        rest = lax.fori_loop(0, n_rounds, pick, vals)
    sel_buf[...] = jnp.where(forced | (rest < vals), 0.0, NEG_INF)

    def update(blocks, bias, vT_tiles, m, acc):
        part = None
        for blk, b in zip(blocks, bias):
            t = jnp.max(blk.reshape(blk.shape[0] // 8, 8, m_lanes), axis=0) + b
            part = t if part is None else jnp.maximum(part, t)
        m_new = jnp.maximum(m, jnp.max(part, axis=0, keepdims=True))
        m_safe = jnp.where(m_new == NEG_INF, 0.0, m_new)
        pb = jnp.concatenate([jnp.exp2(blk + (b - m_safe)) for blk, b in zip(blocks, bias)], axis=0).astype(BF16)
        pv = None
        for k, vT in enumerate(vT_tiles):
            t = _dot(vT, pb[k * KEY_TILE:(k + 1) * KEY_TILE])
            pv = t if pv is None else pv + t
        return m_new, jnp.exp2(m - m_safe) * acc + pv

    def empty_state(v_ref):
        return jnp.full((1, m_lanes), NEG_INF, F32), jnp.zeros((v_ref.shape[2], m_lanes), F32)

    def finish(acc):
        return acc[:e] / jnp.maximum(acc[e:e + 1], 1e-30)

    q_tiles = tq // KEY_TILE
    first_diag = i * q_tiles
    q_col = lane & (tq - 1)

    def tri_le(d):
        return jnp.where(row_k + d * KEY_TILE <= q_col, 0.0, NEG_INF)

    def tri_gt(d):
        return jnp.where(row_k + d * KEY_TILE > q_col, 0.0, NEG_INF)

    blocks_per_tile = KEY_TILE // SEL_LEN
    chunk_blocks = SEL_TILES * blocks_per_tile
    n_sel_chunks = ksel_ref.shape[2] // SEL_TILES

    def sel_scores(c):
        base = jnp.minimum(c, n_sel_chunks - 1) * SEL_TILES
        ks = jnp.concatenate([ksel_ref[0, 0, base + k] for k in range(SEL_TILES)], axis=0)
        return _dot(ks, qTm)

    def block_rows(first_block, n):
        return [jnp.concatenate([sel_buf[pl.ds(first_block + h, 1), :]] * GQA_REP, axis=1) for h in range(n)]

    def sel_chunk(sq_ref, c, state):
        s = sq_ref[...]
        blocks = [s[h * SEL_LEN:(h + 1) * SEL_LEN] for h in range(chunk_blocks)]
        rows = block_rows(c * chunk_blocks, chunk_blocks)
        bias = [rows[h] + jnp.where(c * SEL_TILES + h // blocks_per_tile < first_diag, 0.0, NEG_INF)
                for h in range(chunk_blocks)]
        return update(blocks, bias, [vTsel_ref[0, c * SEL_TILES + k, :, :] for k in range(SEL_TILES)], *state)

    def sel_pair(cp, state):
        c0 = 2 * cp
        sq1_buf[...] = sel_scores(c0 + 1)
        state = sel_chunk(sq0_buf, c0, state)
        sq0_buf[...] = sel_scores(c0 + 2)
        return sel_chunk(sq1_buf, c0 + 1, state)

    def sel_quad(cq, state):
        return sel_pair(2 * cq + 1, sel_pair(2 * cq, state))

    n_before = (first_diag + SEL_TILES - 1) // SEL_TILES
    n_pairs = n_before // 2
    n_quads = n_pairs // 2
    sq0_buf[...] = sel_scores(0)
    state = lax.fori_loop(0, n_quads, sel_quad, empty_state(vTsel_ref))
    state = lax.fori_loop(2 * n_quads, n_pairs, sel_pair, state)
    state = lax.cond(n_before % 2 == 1, lambda st: sel_chunk(sq0_buf, 2 * n_pairs, st), lambda st: st, state)
    kd = jnp.concatenate([ksel_ref[0, 0, first_diag + d] for d in range(q_tiles)], axis=0)
    s_d = _dot(kd, qTm) + jnp.concatenate([tri_le(d) for d in range(q_tiles)], axis=0)
    _, acc_s = update([s_d[h * SEL_LEN:(h + 1) * SEL_LEN] for h in range(q_tiles * blocks_per_tile)],
                      block_rows(first_diag * blocks_per_tile, q_tiles * blocks_per_tile),
                      [vTsel_ref[0, first_diag + d, :, :] for d in range(q_tiles)], *state)
    o_sel = finish(acc_s)

    n_back = WIN_LEN // KEY_TILE
    n_win = n_back + q_tiles
    first_tile = first_diag - n_back
    kw = jnp.concatenate([kwin_ref[0, 0, jnp.maximum(first_tile + d, 0)] for d in range(n_win)], axis=0)
    s_w = _dot(kw, qTm)
    w_blocks = [s_w[d * KEY_TILE:(d + 1) * KEY_TILE] for d in range(n_win)]
    for d in range(q_tiles):
        w_blocks[d] = w_blocks[d] + tri_gt(d)
        w_blocks[n_back + d] = w_blocks[n_back + d] + tri_le(d)
    w_bias = [jnp.where(first_tile + d >= 0, 0.0, NEG_INF) for d in range(n_win)]
    _, acc_w = update(w_blocks, w_bias,
                      [vTwin_ref[0, jnp.maximum(first_tile + d, 0), :, :] for d in range(n_win)],
                      *empty_state(vTwin_ref))
    o_win = finish(acc_w)

    def gate(branch):
        g = gT_ref[0]
        return jnp.concatenate([g[branch * GQA_REP + r:branch * GQA_REP + r + 1, :] for r in range(GQA_REP)], axis=1)

    oT = o_cmp * gate(0) + o_sel * gate(1) + o_win * gate(2)
    o_rows = jnp.concatenate([oT[:, r * tq:(r + 1) * tq] for r in range(GQA_REP)], axis=0)
    o_ref[0] = o_rows.T.astype(BF16)


def _overlap_window(n_sel):
    per_tile = KEY_TILE * CMP_STRIDE // SEL_LEN
    return per_tile, min(per_tile + 16, n_sel)


def _overlap_tiles(n_sel, n_cmp_pad):
    cs = np.arange(n_cmp_pad)[None, :] * CMP_STRIDE
    ss = np.arange(n_sel)[:, None] * SEL_LEN
    ov = ((cs < ss + SEL_LEN) & (cs + CMP_LEN > ss)).astype(np.float32)
    per_tile, rows = _overlap_window(n_sel)
    tiles = []
    for k in range(n_cmp_pad // KEY_TILE):
        start = min(k * per_tile, n_sel - rows)
        cols = ov[:, k * KEY_TILE:(k + 1) * KEY_TILE]
        assert not cols[:start].any() and not cols[start + rows:].any()
        tiles.append(cols[start:start + rows])
    return jnp.asarray(np.stack(tiles), BF16)


def _nsa_attention(qT, kc, vcT, ksel, vTsel, kwin, vTwin, gT):
    B, D, S = qT.shape
    G, E = N_KV_HEADS, HEAD_DIM
    n_sel = S // SEL_LEN
    n_tiles = S // KEY_TILE
    n_chunks = kc.shape[2] // KEY_TILE
    kc5 = kc.reshape(B, G, n_chunks, KEY_TILE, E)
    ksel5 = ksel.reshape(B, G, n_tiles, KEY_TILE, E)
    kwin5 = kwin.reshape(B, G, n_tiles, KEY_TILE, E)
    ov = _overlap_tiles(n_sel, n_chunks * KEY_TILE)
    cmp_tiles = 2 if n_chunks % 2 == 0 else 1
    kspec = pl.BlockSpec((1, 1, n_tiles, KEY_TILE, E), lambda b, g, i: (b, g, 0, 0, 0))
    vspec = pl.BlockSpec((1, n_tiles, V_EXT, KEY_TILE), lambda b, g, i: (b, 0, g, 0))
    m_lanes = GQA_REP * Q_TILE
    return pl.pallas_call(
        _nsa_attn_kernel,
        out_shape=jax.ShapeDtypeStruct((B, S, D), BF16),
        grid=(B, G, S // Q_TILE),
        in_specs=[
            pl.BlockSpec((1, GQA_REP * E, Q_TILE), lambda b, g, i: (b, g, i)),
            pl.BlockSpec((1, 1, n_chunks, KEY_TILE, E), lambda b, g, i: (b, g, 0, 0, 0)),
            pl.BlockSpec((1, 1, n_chunks, E, KEY_TILE), lambda b, g, i: (b, g, 0, 0, 0)),
            pl.BlockSpec(ov.shape, lambda b, g, i: (0, 0, 0)),
            kspec, vspec, kspec, vspec,
            pl.BlockSpec((1, 16, Q_TILE), lambda b, g, i: (b, g, i)),
        ],
        out_specs=pl.BlockSpec((1, Q_TILE, GQA_REP * E), lambda b, g, i: (b, i, g)),
        scratch_shapes=[
            pltpu.VMEM((n_chunks // cmp_tiles, cmp_tiles * KEY_TILE, m_lanes), F32),
            pltpu.VMEM((n_sel, m_lanes), F32),
            pltpu.VMEM((n_sel, Q_TILE), F32),
            pltpu.VMEM((SEL_TILES * KEY_TILE, m_lanes), F32),
            pltpu.VMEM((SEL_TILES * KEY_TILE, m_lanes), F32),
        ],
        compiler_params=_cparams(("parallel", "parallel", "arbitrary")),
        name="nsa_attention",
    )(qT, kc5, vcT, ov, ksel5, vTsel, kwin5, vTwin, gT)


def _layer_norm(z, g, b):
    mu = jnp.mean(z, axis=-1, keepdims=True)
    d = z - mu
    var = jnp.mean(d * d, axis=-1, keepdims=True)
    return d * lax.rsqrt(var + LN_EPS) * g + b


def _proj_ln_kernel(o_ref, x_ref, w_ref, gate_ref, g_ref, b_ref, sh_ref, sc_ref, rw_ref,
                    x1_ref, h_ref, lgT_ref):
    y = _dot(o_ref[0], w_ref[...])
    xn = _layer_norm(DN_ALPHA * x_ref[0] + gate_ref[0] * y, g_ref[...], b_ref[...])
    x1_ref[0] = xn
    h = xn * (1.0 + sc_ref[0]) + sh_ref[0]
    h_ref[0] = h
    lgT_ref[0] = lax.dot_general(rw_ref[...], h, (((1,), (1,)), ((), ())),
                                 preferred_element_type=F32, precision=HIGHEST)


def _proj_ln(o, x, w_o, gate, ln_g, ln_b, shift2, scale2, router_w):
    B, S, D = x.shape
    tm = ROW_TILE
    rw = router_w.T
    vec = pl.BlockSpec((1, 1, D), lambda b, i: (b, 0, 0))
    par = pl.BlockSpec((1, D), lambda b, i: (0, 0))
    row = pl.BlockSpec((1, tm, D), lambda b, i: (b, i, 0))
    return pl.pallas_call(
        _proj_ln_kernel,
        out_shape=[
            jax.ShapeDtypeStruct((B, S, D), F32),
            jax.ShapeDtypeStruct((B, S, D), F32),
            jax.ShapeDtypeStruct((B, N_EXPERTS, S), F32),
        ],
        grid=(B, S // tm),
        in_specs=[row, row, pl.BlockSpec((D, D), lambda b, i: (0, 0)), vec, par, par, vec, vec,
                  pl.BlockSpec((N_EXPERTS, D), lambda b, i: (0, 0))],
        out_specs=[row, row, pl.BlockSpec((1, N_EXPERTS, tm), lambda b, i: (b, 0, i))],
        compiler_params=_cparams(("parallel", "parallel")),
        name="proj_ln",
    )(o, x, w_o.astype(BF16), gate, ln_g.reshape(1, D), ln_b.reshape(1, D), shift2, scale2, rw)


def _first_max(v, idx, big):
    top = jnp.max(v, axis=0, keepdims=True)
    first = jnp.min(jnp.where(v == top, idx, big), axis=0, keepdims=True)
    return top, first


def _route_kernel(lg_ref, rb_ref, e_ref, w_ref):
    scores = jax.nn.sigmoid(lg_ref[0])
    biased = scores + rb_ref[...]
    eidx = lax.broadcasted_iota(jnp.int32, scores.shape, 0)
    npg = EXPERTS_PER_GROUP
    best_v, best_g = None, None
    for g in range(N_GROUPS):
        v = biased[g * npg:(g + 1) * npg]
        ii = g * npg + lax.broadcasted_iota(jnp.int32, v.shape, 0)
        top1, i1 = _first_max(v, ii, N_EXPERTS)
        top2 = jnp.max(jnp.where(ii == i1, NEG_INF, v), axis=0, keepdims=True)
        gs = top1 + top2
        if g == 0:
            best_v, best_g = gs, jnp.zeros_like(i1)
        else:
            better = gs > best_v
            best_g = jnp.where(better, g, best_g)
            best_v = jnp.where(better, gs, best_v)
    masked = jnp.where(eidx // npg == best_g, biased, NEG_INF)
    _, e1 = _first_max(masked, eidx, N_EXPERTS)
    _, e2 = _first_max(jnp.where(eidx == e1, NEG_INF, masked), eidx, N_EXPERTS)
    sc1 = jnp.sum(jnp.where(eidx == e1, scores, 0.0), axis=0, keepdims=True)
    sc2 = jnp.sum(jnp.where(eidx == e2, scores, 0.0), axis=0, keepdims=True)
    tot = sc1 + sc2
    e_ref[0] = jnp.concatenate([e1, e2], axis=0)
    w_ref[0] = jnp.concatenate([sc1 / tot, sc2 / tot], axis=0)


def _route(lgT, router_b):
    B, E, S = lgT.shape
    tn = min(S, 2048)
    return pl.pallas_call(
        _route_kernel,
        out_shape=[jax.ShapeDtypeStruct((B, TOP_K, S), jnp.int32), jax.ShapeDtypeStruct((B, TOP_K, S), F32)],
        grid=(B, S // tn),
        in_specs=[pl.BlockSpec((1, E, tn), lambda b, i: (b, 0, i)), pl.BlockSpec((E, 1), lambda b, i: (0, 0))],
        out_specs=[pl.BlockSpec((1, TOP_K, tn), lambda b, i: (b, 0, i))] * 2,
        compiler_params=_cparams(("parallel", "parallel")),
        name="moe_route",
    )(lgT, router_b.reshape(E, 1))


def _dispatch_plan(eidx):
    B, K, S = eidx.shape
    n_asg = B * S * K
    e_flat = eidx.transpose(0, 2, 1).reshape(n_asg)
    chunk = LANES
    onehot = (e_flat[:, None] == jnp.arange(N_EXPERTS, dtype=jnp.int32)[None, :]).astype(F32)
    oh = onehot.reshape(n_asg // chunk, chunk, N_EXPERTS)
    tri = jnp.tril(jnp.ones((chunk, chunk), F32))
    within = jnp.einsum("ij,cjk->cik", tri, oh)
    chunk_tot = within[:, -1, :]
    chunk_end = jnp.cumsum(chunk_tot, axis=0)
    incl = within + (chunk_end - chunk_tot)[:, None, :]
    rank = (jnp.sum(incl * oh, axis=-1) - 1.0).reshape(n_asg).astype(jnp.int32)
    counts = chunk_end[-1].astype(jnp.int32)
    padded = (counts + MOE_BLOCK - 1) // MOE_BLOCK * MOE_BLOCK
    pad_ends = jnp.cumsum(padded)
    dest = (pad_ends - padded)[e_flat] + rank
    n_blk = n_asg // MOE_BLOCK + N_EXPERTS
    blk_start = jnp.arange(n_blk, dtype=jnp.int32) * MOE_BLOCK
    blk_exp = jnp.minimum(jnp.sum((pad_ends[None, :] <= blk_start[:, None]).astype(jnp.int32), axis=1),
                          N_EXPERTS - 1)
    return blk_exp, dest


def _moe_dispatch_kernel(n_steps, dest_ref, x_ref, xs_init_ref, xs_hbm, stage, sems):
    del xs_init_ref
    i = pl.program_id(0)
    tm = x_ref.shape[0]
    slot = i & 1

    def drain(s):
        for _ in range(TOP_K):
            pltpu.make_async_copy(stage.at[s], xs_hbm.at[pl.ds(0, tm), :], sems.at[s]).wait()

    @pl.when(i >= 2)
    def _():
        drain(slot)

    stage[slot] = x_ref[...]

    def issue(r, c):
        for k in range(TOP_K):
            pltpu.make_async_copy(stage.at[slot, pl.ds(r, 1), :],
                                  xs_hbm.at[pl.ds(dest_ref[0, 0, TOP_K * r + k], 1), :],
                                  sems.at[slot]).start(priority=k % 2)
        return c

    lax.fori_loop(0, tm, issue, 0, unroll=8)

    @pl.when(i == n_steps - 1)
    def _():
        drain(slot)
        if n_steps >= 2:
            drain(1 - slot)


def _moe_dispatch(h2d, dest, n_rows):
    n_tok, D = h2d.shape
    tm = MOE_IO_TILE
    n_steps = n_tok // tm
    return pl.pallas_call(
        functools.partial(_moe_dispatch_kernel, n_steps),
        out_shape=jax.ShapeDtypeStruct((n_rows, D), F32),
        grid=(n_steps,),
        in_specs=[
            pl.BlockSpec((1, 1, TOP_K * tm), lambda i: (i, 0, 0), memory_space=pltpu.SMEM),
            pl.BlockSpec((tm, D), lambda i: (i, 0)),
            pl.BlockSpec(memory_space=pl.ANY),
        ],
        out_specs=pl.BlockSpec(memory_space=pl.ANY),
        scratch_shapes=[pltpu.VMEM((2, tm, D), F32), pltpu.SemaphoreType.DMA((2,))],
        input_output_aliases={2: 0},
        compiler_params=_cparams(("arbitrary",)),
        name="moe_dispatch",
    )(dest.reshape(n_steps, 1, TOP_K * tm), h2d, jnp.zeros((n_rows, D), F32))


def _moe_expert_kernel(blk_exp_ref, x_ref, wg_ref, wu_ref, wd_ref, y_ref, wg_b, wu_b, wd_b):
    i = pl.program_id(0)

    @pl.when((i == 0) | (blk_exp_ref[i] != blk_exp_ref[jnp.maximum(i - 1, 0)]))
    def _():
        wg_b[...] = wg_ref[0].astype(BF16)
        wu_b[...] = wu_ref[0].astype(BF16)
        wd_b[...] = wd_ref[0].astype(BF16)

    x = x_ref[...].astype(BF16)
    gate = _dot(x, wg_b[...])
    up = _dot(x, wu_b[...])
    hid = (gate * jax.nn.sigmoid(gate) * up).astype(BF16)
    y_ref[...] = _dot(hid, wd_b[...])


def _moe_experts(xs, blk_exp, w_gate, w_up, w_down):
    n_rows, D = xs.shape
    F = w_gate.shape[2]
    rows = pl.BlockSpec((MOE_BLOCK, D), lambda i, be: (i, 0))
    grid_spec = pltpu.PrefetchScalarGridSpec(
        num_scalar_prefetch=1,
        grid=(n_rows // MOE_BLOCK,),
        in_specs=[
            rows,
            pl.BlockSpec((1, D, F), lambda i, be: (be[i], 0, 0)),
            pl.BlockSpec((1, D, F), lambda i, be: (be[i], 0, 0)),
            pl.BlockSpec((1, F, D), lambda i, be: (be[i], 0, 0)),
        ],
        out_specs=rows,
        scratch_shapes=[pltpu.VMEM((D, F), BF16), pltpu.VMEM((D, F), BF16), pltpu.VMEM((F, D), BF16)],
    )
    return pl.pallas_call(
        _moe_expert_kernel,
        out_shape=jax.ShapeDtypeStruct((n_rows, D), F32),
        grid_spec=grid_spec,
        compiler_params=_cparams(("arbitrary",)),
        name="moe_experts",
    )(blk_exp, xs, w_gate, w_up, w_down)


def _moe_combine_kernel(n_steps, dcur_ref, dnext_ref, w_ref, x_ref, gate_ref, g_ref, b_ref, ys_hbm,
                        o_ref, gbuf, sems):
    i = pl.program_id(0)
    tm = x_ref.shape[0]
    slot = i & 1

    def issue(d_ref, s):
        def body(r, c):
            for k in range(TOP_K):
                pltpu.make_async_copy(ys_hbm.at[pl.ds(d_ref[0, 0, TOP_K * r + k], 1), :],
                                      gbuf.at[s, k, pl.ds(r, 1), :], sems.at[s]).start(priority=k % 2)
            return c
        lax.fori_loop(0, tm, body, 0, unroll=8)

    @pl.when(i == 0)
    def _():
        issue(dcur_ref, slot)

    @pl.when(i + 1 < n_steps)
    def _():
        issue(dnext_ref, 1 - slot)

    for k in range(TOP_K):
        pltpu.make_async_copy(ys_hbm.at[pl.ds(0, tm), :], gbuf.at[slot, k], sems.at[slot]).wait()
    w = w_ref[...]
    y = gbuf[slot, 0] * w[:, 0:1] + gbuf[slot, 1] * w[:, 1:2]
    o_ref[...] = _layer_norm(DN_ALPHA * x_ref[...] + gate_ref[0] * y, g_ref[...], b_ref[...])


def _moe_combine_ln(ys, dest, wts, x, gate, ln_g, ln_b):
    B, S, D = x.shape
    n_tok = B * S
    tm = MOE_IO_TILE
    n_steps = n_tok // tm
    per_b = S // tm
    d3 = dest.reshape(n_steps, 1, TOP_K * tm)
    w2 = wts.transpose(0, 2, 1).reshape(n_tok, TOP_K)
    idx = lambda f: pl.BlockSpec((1, 1, TOP_K * tm), f, memory_space=pltpu.SMEM)
    par = pl.BlockSpec((1, D), lambda i: (0, 0))
    row = pl.BlockSpec((tm, D), lambda i: (i, 0))
    out = pl.pallas_call(
        functools.partial(_moe_combine_kernel, n_steps),
        out_shape=jax.ShapeDtypeStruct((n_tok, D), F32),
        grid=(n_steps,),
        in_specs=[
            idx(lambda i: (i, 0, 0)),
            idx(lambda i: (jnp.minimum(i + 1, n_steps - 1), 0, 0)),
            pl.BlockSpec((tm, TOP_K), lambda i: (i, 0)),
            row,
            pl.BlockSpec((1, 1, D), lambda i: (i // per_b, 0, 0)),
            par, par,
            pl.BlockSpec(memory_space=pl.ANY),
        ],
        out_specs=row,
        scratch_shapes=[pltpu.VMEM((2, TOP_K, tm, D), F32), pltpu.SemaphoreType.DMA((2,))],
        compiler_params=_cparams(("arbitrary",)),
        name="moe_combine_ln",
    )(d3, d3, w2, x.reshape(n_tok, D), gate, ln_g.reshape(1, D), ln_b.reshape(1, D), ys)
    return out.reshape(B, S, D)


def _moe_sublayer(x1, h2, lgT, router_b, w_gate, w_up, w_down, gate, ln_g, ln_b):
    B, S, D = x1.shape
    eidx, wts = _route(lgT, router_b)
    blk_exp, dest = _dispatch_plan(eidx)
    n_rows = blk_exp.shape[0] * MOE_BLOCK
    xs = _moe_dispatch(h2.reshape(B * S, D), dest, n_rows)
    ys = _moe_experts(xs, blk_exp, w_gate, w_up, w_down)
    return _moe_combine_ln(ys, dest, wts, x1, gate, ln_g, ln_b)


def _dil_weight_cols():
    d = D_MODEL
    cols = list(range(d))
    for which in range(2):
        for p in range(len(DIL_PATTERNS)):
            base = d + (p * 2 + which) * KV_COLS
            cols += list(range(base, base + KV_COLS))
    return np.asarray(cols)


def _dil_inproj_kernel(x_ref, sh_ref, sc_ref, w_ref, c_ref, s1_ref, s2_ref, q_ref, *kv_refs):
    h = (x_ref[0] * (1.0 + sc_ref[0]) + sh_ref[0]).astype(BF16)
    c, s1, s2 = c_ref[0], s1_ref[0], s2_ref[0]
    d = q_ref.shape[2]
    w = KV_COLS
    n_pat = len(DIL_PATTERNS)
    for j in range(d // w):
        a = _rope_cols(_dot(h, w_ref[:, j * w:(j + 1) * w]), c, s1, s2) * Q_SCALE
        q_ref[0, :, j * w:(j + 1) * w] = a.astype(BF16)
    for p in range(n_pat):
        a = _rope_cols(_dot(h, w_ref[:, d + p * w:d + (p + 1) * w]), c, s1, s2)
        kv_refs[p][0] = a.astype(BF16)
    for p in range(n_pat):
        a = _dot(h, w_ref[:, d + (n_pat + p) * w:d + (n_pat + p + 1) * w])
        kv_refs[n_pat + p][0] = a.astype(BF16)


def _dil_inproj(x, shift, scale, w_in, tabs):
    B, S, D = x.shape
    tm = ROW_TILE
    wp = _permute_cols(w_in, _dil_weight_cols())
    ncol = wp.shape[1]
    n_pat = len(DIL_PATTERNS)
    vec = pl.BlockSpec((1, 1, D), lambda b, i: (b, 0, 0))
    tab = pl.BlockSpec((1, tm, LANES), lambda b, i: (b, i, 0))
    kvspec = pl.BlockSpec((1, tm, KV_COLS), lambda b, i: (b, i, 0))
    outs = pl.pallas_call(
        _dil_inproj_kernel,
        out_shape=[jax.ShapeDtypeStruct((B, S, D), BF16)]
        + [jax.ShapeDtypeStruct((B, S, KV_COLS), BF16)] * (2 * n_pat),
        grid=(B, S // tm),
        in_specs=[pl.BlockSpec((1, tm, D), lambda b, i: (b, i, 0)), vec, vec,
                  pl.BlockSpec((D, ncol), lambda b, i: (0, 0)), tab, tab, tab],
        out_specs=[pl.BlockSpec((1, tm, D), lambda b, i: (b, i, 0))] + [kvspec] * (2 * n_pat),
        compiler_params=_cparams(("parallel", "parallel")),
        name="dil_inproj",
    )(x, shift, scale, wp, *tabs)
    return outs[0], outs[1:1 + n_pat], outs[1 + n_pat:]


def _dil_attn_kernel(steps, first, last, *refs):
    q_ref = refs[0]
    blk = q_ref.shape[1]
    kt = DIL_BLOCK
    n_kt = blk // kt + 1
    k_refs, v_refs = refs[1:1 + n_kt], refs[1 + n_kt:1 + 2 * n_kt]
    if first:
        acc_in = ml_in = None
        outs = refs[1 + 2 * n_kt:]
    else:
        acc_in, ml_in = refs[1 + 2 * n_kt:3 + 2 * n_kt]
        outs = refs[3 + 2 * n_kt:]
    nb = pl.program_id(2)
    e = HEAD_DIM
    gw = GQA_REP * e
    kj = lax.broadcasted_iota(jnp.int32, (n_kt * kt, blk), 0)
    qi = lax.broadcasted_iota(jnp.int32, (n_kt * kt, blk), 1)
    dist = kt + qi - kj
    valid = (dist >= 0) & (dist <= steps) & ((nb * (blk // kt) - 1) * kt + kj >= 0)
    bias = jnp.where(valid, 0.0, NEG_INF)
    bias = jnp.concatenate([bias] * GQA_REP, axis=1)
    zeros_half = jnp.zeros((e, GQA_REP * blk), F32)
    mlT_old = None if first else ml_in[0].T
    m_rows, l_rows = [], []

    def heads_to_lanes(t):
        return jnp.concatenate([t[r * e:(r + 1) * e] for r in range(GQA_REP)], axis=1)

    for g in range(N_KV_HEADS):
        seg = slice((g // 2) * LANES, (g // 2 + 1) * LANES)
        qs = slice(g * gw, (g + 1) * gw)
        qTm = heads_to_lanes(q_ref[0][:, qs].astype(F32).T)
        qT2 = jnp.concatenate([qTm, zeros_half] if g % 2 == 0 else [zeros_half, qTm], axis=0).astype(BF16)
        kcat = jnp.concatenate([r[0][:, seg] for r in k_refs], axis=0)
        s = _dot(kcat, qT2) + bias
        m_new = jnp.max(s, axis=0, keepdims=True)
        if not first:
            m_old = jnp.concatenate([mlT_old[g * GQA_REP + r:g * GQA_REP + r + 1] for r in range(GQA_REP)], axis=1)
            l_old = jnp.concatenate([mlT_old[N_Q_HEADS + g * GQA_REP + r:N_Q_HEADS + g * GQA_REP + r + 1]
                                     for r in range(GQA_REP)], axis=1)
            m_new = jnp.maximum(m_old, m_new)
            alpha = jnp.exp2(m_old - m_new)
        p = jnp.exp2(s - m_new)
        l_new = jnp.sum(p, axis=0, keepdims=True)
        vcat = jnp.concatenate([r[0][:, seg] for r in v_refs], axis=0).astype(F32)
        vT = vcat.T[(g % 2) * e:(g % 2 + 1) * e].astype(BF16)
        accT = _dot(vT, p.astype(BF16))
        if not first:
            l_new = l_new + alpha * l_old
            accT = accT + alpha * heads_to_lanes(acc_in[0][:, qs].T)
        if last:
            accT = accT / l_new
        o_rows = jnp.concatenate([accT[:, r * blk:(r + 1) * blk] for r in range(GQA_REP)], axis=0)
        outs[0][0, :, qs] = o_rows.T.astype(outs[0].dtype)
        m_rows += [m_new[:, r * blk:(r + 1) * blk] for r in range(GQA_REP)]
        l_rows += [l_new[:, r * blk:(r + 1) * blk] for r in range(GQA_REP)]
    if not last:
        pad = jnp.zeros((LANES - 2 * N_Q_HEADS, blk), F32)
        outs[1][0] = jnp.concatenate(m_rows + l_rows + [pad], axis=0).T


def _dil_attention(q, kds, vds):
    B, S, D = q.shape
    n_pat = len(DIL_PATTERNS)
    acc = ml = None
    for p, (window, dil) in enumerate(DIL_PATTERNS):
        first, last = p == 0, p == n_pat - 1
        L = S // dil
        qb = DIL_Q_TILE
        n_kt = qb // DIL_BLOCK + 1
        nblk = L // qb
        kw = KV_COLS

        def view(a):
            return a.reshape(B, L, dil * a.shape[2])

        cur = lambda b, c, n: (b, n, c)
        def key_tile(j):
            return lambda b, c, n: (b, jnp.maximum(n * (n_kt - 1) - 1 + j, 0), c)

        qspec = pl.BlockSpec((1, qb, D), cur)
        kv_specs = [pl.BlockSpec((1, DIL_BLOCK, kw), key_tile(j)) for j in range(n_kt)]
        in_specs = [qspec] + kv_specs + kv_specs
        args = [view(q)] + [view(kds[p])] * n_kt + [view(vds[p])] * n_kt
        if not first:
            in_specs += [qspec, pl.BlockSpec((1, qb, LANES), cur)]
            args += [view(acc), view(ml)]
        if last:
            out_shape = [jax.ShapeDtypeStruct((B, L, dil * D), BF16)]
            out_specs = [qspec]
        else:
            out_shape = [jax.ShapeDtypeStruct((B, L, dil * D), F32),
                         jax.ShapeDtypeStruct((B, L, dil * LANES), F32)]
            out_specs = [qspec, pl.BlockSpec((1, qb, LANES), cur)]
        res = pl.pallas_call(
            functools.partial(_dil_attn_kernel, window // dil, first, last),
            out_shape=out_shape,
            grid=(B, dil, nblk),
            in_specs=in_specs,
            out_specs=out_specs,
            compiler_params=_cparams(("parallel", "parallel", "arbitrary")),
            name=f"dil_attention_{p}",
        )(*args)
        if last:
            return res[0].reshape(B, S, D)
        acc, ml = res[0].reshape(B, S, D), res[1].reshape(B, S, LANES)


def kernel(x, c, positions, ada_w, ada_b, ln_g, ln_b, nsa_w_in, nsa_cmp_pos_k, nsa_cmp_w1_k, nsa_cmp_w2_k, nsa_cmp_pos_v, nsa_cmp_w1_v, nsa_cmp_w2_v, nsa_w_o, dil_w_in, dil_w_o, router_w, router_b, moe_w_gate, moe_w_up, moe_w_down):
    B, S, D = x.shape
    mods = _ada_mods(c, ada_w, ada_b)
    def mod(i, sub):
        m = mods[i * 2 + sub]
        return [m[:, k * D:(k + 1) * D].reshape(B, 1, D) for k in range(3)]
    tabs = _rope_tables(positions)

    for i in range(DEPTH):
        shift, scale, gate = mod(i, 0)
        shift2, scale2, gate2 = mod(i, 1)
        j = i // 2
        if i % 2 == 0:
            qT, ksel, kwin, vTsel, vTwin, kcmp, vcmp, gT = _nsa_inproj(x, shift, scale, nsa_w_in[j], tabs)
            kc, vcT = _compress(kcmp, vcmp, nsa_cmp_pos_k[j], nsa_cmp_w1_k[j], nsa_cmp_w2_k[j],
                                nsa_cmp_pos_v[j], nsa_cmp_w1_v[j], nsa_cmp_w2_v[j], tabs)
            o = _nsa_attention(qT, kc, vcT, ksel, vTsel, kwin, vTwin, gT)
            w_o = nsa_w_o[j]
        else:
            q, kds, vds = _dil_inproj(x, shift, scale, dil_w_in[j], tabs)
            o = _dil_attention(q, kds, vds)
            w_o = dil_w_o[j]
        x1, h2, lgT = _proj_ln(o, x, w_o, gate, ln_g[i, 0], ln_b[i, 0], shift2, scale2, router_w)
        x = _moe_sublayer(x1, h2, lgT, router_b, moe_w_gate[i], moe_w_up[i], moe_w_down[i],
                          gate2, ln_g[i, 1], ln_b[i, 1])
    return x
```

```python
import functools

import numpy as np
import jax
import jax.numpy as jnp
from jax import lax
from jax.experimental import pallas as pl
from jax.experimental.pallas import tpu as pltpu

F32 = jnp.float32
BF16 = jnp.bfloat16
HIGHEST = lax.Precision.HIGHEST
NEG_INF = float("-inf")

D_MODEL = 1024
DEPTH = 2
HEAD_DIM = 64
N_Q_HEADS = D_MODEL // HEAD_DIM
N_KV_HEADS = 4
GQA_REP = N_Q_HEADS // N_KV_HEADS
ROPE_DIM = HEAD_DIM // 4
ROPE_THETA = 500000.0
ATTN_SCALE = HEAD_DIM ** -0.5
LOG2_E = 1.4426950408889634
Q_SCALE = ATTN_SCALE * LOG2_E
KV_COLS = N_KV_HEADS * HEAD_DIM
N_BRANCH = 3
CMP_LEN = 32
CMP_STRIDE = 16
CMP_HIDDEN = 256
SEL_LEN = 64
N_SELECT = 16
WIN_LEN = 512
FORCE_SCORE = 1.0e4
DIL_PATTERNS = ((128, 1), (512, 4), (2048, 16))
DIL_BLOCK = 128
N_EXPERTS = 32
N_GROUPS = 4
EXPERTS_PER_GROUP = N_EXPERTS // N_GROUPS
TOP_K = 2
D_EXPERT = 512
MOE_BLOCK = 128
DN_ALPHA = (2.0 * DEPTH) ** 0.25
LN_EPS = 1e-5

LANES = 128
VMEM_LIMIT_BYTES = 48 * 1024 * 1024

Q_TILE = 256
KEY_TILE = 128
SEL_TILES = 4
V_EXT = HEAD_DIM + 16
DIL_Q_TILE = 256
ROW_TILE = 512
MOE_IO_TILE = 256


def _cparams(semantics):
    return pltpu.CompilerParams(dimension_semantics=semantics, vmem_limit_bytes=VMEM_LIMIT_BYTES)


def _dot(a, b):
    return jnp.dot(a, b, preferred_element_type=F32)


def _dot_nt(a, b):
    return lax.dot_general(a, b, (((1,), (1,)), ((), ())), preferred_element_type=F32)


def _ada_kernel(c_ref, w_ref, b_ref, o_ref):
    c = c_ref[...]
    cond = c * jax.nn.sigmoid(c)
    o_ref[0] = jnp.dot(cond, w_ref[0], preferred_element_type=F32, precision=HIGHEST) + b_ref[0]


def _ada_mods(c, ada_w, ada_b):
    B, D = c.shape
    n_sub = ada_w.shape[0] * ada_w.shape[1]
    w = ada_w.reshape(n_sub, D, 3 * D)
    b = ada_b.reshape(n_sub, 1, 3 * D)
    c8 = jnp.zeros((8, D), F32).at[:B].set(c)
    tn = 768
    out = pl.pallas_call(
        _ada_kernel,
        out_shape=jax.ShapeDtypeStruct((n_sub, 8, 3 * D), F32),
        grid=(n_sub, 3 * D // tn),
        in_specs=[
            pl.BlockSpec((8, D), lambda s, j: (0, 0)),
            pl.BlockSpec((1, D, tn), lambda s, j: (s, 0, j)),
            pl.BlockSpec((1, 1, tn), lambda s, j: (s, 0, j)),
        ],
        out_specs=pl.BlockSpec((1, 8, tn), lambda s, j: (s, 0, j)),
        compiler_params=_cparams(("parallel", "parallel")),
        name="ada_mods",
    )(c8, w, b)
    return out[:, :B]


def _rope_tab_kernel(pos_ref, inv_ref, sg1_ref, sg2_ref, c_ref, s1_ref, s2_ref):
    ang = pos_ref[0] * inv_ref[...]
    sin = jnp.sin(ang)
    c_ref[0] = jnp.cos(ang)
    s1_ref[0] = sin * sg1_ref[...]
    s2_ref[0] = sin * sg2_ref[...]


def _rope_tables(positions):
    B, S = positions.shape
    half = ROPE_DIM // 2
    inv = ROPE_THETA ** (-jnp.arange(half, dtype=F32) * (2.0 / ROPE_DIM))
    li = np.arange(LANES) % HEAD_DIM
    in_rope = li < ROPE_DIM
    inv_row = jnp.where(jnp.asarray(in_rope), inv[li % half], 0.0).reshape(1, LANES)
    sg1 = jnp.asarray(np.where(li < half, -1.0, 0.0), F32).reshape(1, LANES)
    sg2 = jnp.asarray(np.where((li >= half) & in_rope, 1.0, 0.0), F32).reshape(1, LANES)
    pos = positions.astype(F32).reshape(B, S, 1)
    tm = min(S, 2048)
    row = pl.BlockSpec((1, LANES), lambda b, i: (0, 0))
    tab = pl.BlockSpec((1, tm, LANES), lambda b, i: (b, i, 0))
    return pl.pallas_call(
        _rope_tab_kernel,
        out_shape=[jax.ShapeDtypeStruct((B, S, LANES), F32)] * 3,
        grid=(B, S // tm),
        in_specs=[pl.BlockSpec((1, tm, 1), lambda b, i: (b, i, 0)), row, row, row],
        out_specs=[tab, tab, tab],
        compiler_params=_cparams(("parallel", "parallel")),
        name="rope_tables",
    )(pos, inv_row, sg1, sg2)


def _rope128(t, c, s1, s2):
    return t * c + pltpu.roll(t, LANES - ROPE_DIM // 2, 1) * s1 + pltpu.roll(t, ROPE_DIM // 2, 1) * s2


def _rope_cols(a, c, s1, s2):
    n = a.shape[1] // LANES
    return jnp.concatenate(
        [_rope128(a[:, k * LANES:(k + 1) * LANES], c, s1, s2) for k in range(n)], axis=1)


def _nsa_weight_cols():
    d = D_MODEL
    def kv(branch, which):
        base = d + (branch * 2 + which) * KV_COLS
        return list(range(base, base + KV_COLS))
    cols = list(range(d))
    cols += kv(1, 0) + kv(2, 0) + kv(1, 1) + kv(2, 1) + kv(0, 0) + kv(0, 1)
    gate0 = d + N_BRANCH * 2 * KV_COLS
    gcols = [-1] * LANES
    for g in range(N_KV_HEADS):
        for br in range(N_BRANCH):
            for r in range(GQA_REP):
                gcols[g * 16 + br * GQA_REP + r] = gate0 + (g * GQA_REP + r) * N_BRANCH + br
    return np.asarray(cols + gcols)


def _permute_cols(w, cols):
    picked = w[:, np.maximum(cols, 0)]
    return jnp.where(jnp.asarray(cols >= 0)[None, :], picked, 0.0).astype(BF16)


def _nsa_inproj_kernel(x_ref, sh_ref, sc_ref, w_ref, c_ref, s1_ref, s2_ref,
                       qT_ref, ksel_ref, kwin_ref, vTsel_ref, vTwin_ref, kcmp_ref, vcmp_ref, gT_ref):
    tm = x_ref.shape[1]
    h = (x_ref[0] * (1.0 + sc_ref[0]) + sh_ref[0]).astype(BF16)
    c, s1, s2 = c_ref[0], s1_ref[0], s2_ref[0]
    w = KV_COLS

    def proj(j, n=w):
        return _dot(h, w_ref[:, j * w:j * w + n])

    for j in range(4):
        a = _rope_cols(proj(j), c, s1, s2) * Q_SCALE
        qT_ref[0, j * w:(j + 1) * w, :] = a.T.astype(BF16)
    for j, ref in ((4, ksel_ref), (5, kwin_ref)):
        a = _rope_cols(proj(j), c, s1, s2)
        for g in range(N_KV_HEADS):
            ref[0, g] = a[:, g * HEAD_DIM:(g + 1) * HEAD_DIM].astype(BF16)
    ones_rows = jnp.ones((V_EXT - HEAD_DIM, KEY_TILE), BF16)
    for j, ref in ((6, vTsel_ref), (7, vTwin_ref)):
        aT = proj(j).T.astype(BF16)
        for k in range(tm // KEY_TILE):
            for g in range(N_KV_HEADS):
                ref[0, k, g * V_EXT:g * V_EXT + HEAD_DIM] = aT[g * HEAD_DIM:(g + 1) * HEAD_DIM,
                                                               k * KEY_TILE:(k + 1) * KEY_TILE]
                ref[0, k, g * V_EXT + HEAD_DIM:(g + 1) * V_EXT] = ones_rows
    for j, ref in ((8, kcmp_ref), (9, vcmp_ref)):
        a = proj(j)
        for g in range(N_KV_HEADS):
            ref[0, g] = a[:, g * HEAD_DIM:(g + 1) * HEAD_DIM].astype(BF16)
    gates = jax.nn.sigmoid(proj(10, LANES))
    gT_ref[0] = gates.T[:4 * 16]


def _nsa_inproj(x, shift, scale, w_in, tabs):
    B, S, D = x.shape
    tm = ROW_TILE
    wp = _permute_cols(w_in, _nsa_weight_cols())
    ncol = wp.shape[1]
    vec = pl.BlockSpec((1, 1, D), lambda b, i: (b, 0, 0))
    tab = pl.BlockSpec((1, tm, LANES), lambda b, i: (b, i, 0))
    nat = pl.BlockSpec((1, N_KV_HEADS, tm, HEAD_DIM), lambda b, i: (b, 0, i, 0))
    vt = pl.BlockSpec((1, tm // KEY_TILE, N_KV_HEADS * V_EXT, KEY_TILE), lambda b, i: (b, i, 0, 0))
    nat_shape = jax.ShapeDtypeStruct((B, N_KV_HEADS, S, HEAD_DIM), BF16)
    vt_shape = jax.ShapeDtypeStruct((B, S // KEY_TILE, N_KV_HEADS * V_EXT, KEY_TILE), BF16)
    return pl.pallas_call(
        _nsa_inproj_kernel,
        out_shape=[
            jax.ShapeDtypeStruct((B, D, S), BF16),
            nat_shape, nat_shape,
            vt_shape, vt_shape,
            nat_shape, nat_shape,
            jax.ShapeDtypeStruct((B, 4 * 16, S), F32),
        ],
        grid=(B, S // tm),
        in_specs=[
            pl.BlockSpec((1, tm, D), lambda b, i: (b, i, 0)), vec, vec,
            pl.BlockSpec((D, ncol), lambda b, i: (0, 0)), tab, tab, tab,
        ],
        out_specs=[
            pl.BlockSpec((1, D, tm), lambda b, i: (b, 0, i)),
            nat, nat, vt, vt, nat, nat,
            pl.BlockSpec((1, 4 * 16, tm), lambda b, i: (b, 0, i)),
        ],
        compiler_params=_cparams(("parallel", "parallel")),
        name="nsa_inproj",
    )(x, shift, scale, wp, *tabs)


def _compress_kernel(xk_ref, xv_ref, w1k_ref, w1v_ref, pk_ref, pv_ref, w2k_ref, w2vT_ref,
                     c_ref, s1_ref, s2_ref, kc_ref, vcT_ref):
    n = xk_ref.shape[2]
    half = w1k_ref.shape[0] // 2

    def hidden(x_ref, w1_ref, p_ref):
        x = x_ref[0, 0]
        first = _dot(x, w1_ref[:half])
        second = _dot(x, w1_ref[half:])
        bias = _dot(p_ref[...], w1_ref[...])[0:1]
        hid = first + pltpu.roll(second, n - 1, 0) + bias
        return jax.nn.gelu(hid).astype(BF16)

    kc = _dot(hidden(xk_ref, w1k_ref, pk_ref), w2k_ref[...])
    kc = _rope128(kc, c_ref[0], s1_ref[0], s2_ref[0])
    row = lax.broadcasted_iota(jnp.int32, kc.shape, 0)
    kc = jnp.where(row < n - 1, kc, 0.0)
    kc_ref[0, 0] = kc[:, :HEAD_DIM].astype(BF16)

    vcT = _dot_nt(w2vT_ref[...], hidden(xv_ref, w1v_ref, pv_ref))
    col = lax.broadcasted_iota(jnp.int32, vcT.shape, 1)
    vcT = jnp.where(col < n - 1, vcT, 0.0).astype(BF16)
    for k in range(n // KEY_TILE):
        vcT_ref[0, 0, k] = vcT[:, k * KEY_TILE:(k + 1) * KEY_TILE]


def _compress(kcmp, vcmp, pos_k, w1_k, w2_k, pos_v, w1_v, w2_v, tabs):
    B, G, S, E = kcmp.shape
    n = S // CMP_STRIDE
    wide = CMP_STRIDE * E
    xk = kcmp.reshape(B, G, n, wide)
    xv = vcmp.reshape(B, G, n, wide)
    def flat8(p):
        return jnp.zeros((8, CMP_LEN * E), BF16).at[0].set(p.reshape(-1).astype(BF16))
    w2k = jnp.zeros((CMP_HIDDEN, LANES), BF16).at[:, :E].set(w2_k.astype(BF16))
    w2vT = w2_v.T.astype(BF16)
    last = CMP_LEN - 1
    ctabs = [jnp.zeros((B, n, LANES), F32).at[:, :n - 1].set(t[:, last::CMP_STRIDE][:, :n - 1]) for t in tabs]
    xspec = pl.BlockSpec((1, 1, n, wide), lambda b, g: (b, g, 0, 0))
    w1spec = pl.BlockSpec((CMP_LEN * E, CMP_HIDDEN), lambda b, g: (0, 0))
    pspec = pl.BlockSpec((8, CMP_LEN * E), lambda b, g: (0, 0))
    tspec = pl.BlockSpec((1, n, LANES), lambda b, g: (b, 0, 0))
    return pl.pallas_call(
        _compress_kernel,
        out_shape=[
            jax.ShapeDtypeStruct((B, G, n, E), BF16),
            jax.ShapeDtypeStruct((B, G, n // KEY_TILE, E, KEY_TILE), BF16),
        ],
        grid=(B, G),
        in_specs=[xspec, xspec, w1spec, w1spec, pspec, pspec,
                  pl.BlockSpec((CMP_HIDDEN, LANES), lambda b, g: (0, 0)),
                  pl.BlockSpec((E, CMP_HIDDEN), lambda b, g: (0, 0)),
                  tspec, tspec, tspec],
        out_specs=[
            pl.BlockSpec((1, 1, n, E), lambda b, g: (b, g, 0, 0)),
            pl.BlockSpec((1, 1, n // KEY_TILE, E, KEY_TILE), lambda b, g: (b, g, 0, 0, 0)),
        ],
        compiler_params=_cparams(("parallel", "parallel")),
        name="nsa_compress",
    )(xk, xv, w1_k.astype(BF16), w1_v.astype(BF16), flat8(pos_k), flat8(pos_v), w2k, w2vT, *ctabs)


def _nsa_attn_kernel(qT_ref, kc_ref, vcT_ref, ov_ref, ksel_ref, vTsel_ref, kwin_ref, vTwin_ref, gT_ref,
                     o_ref, s_buf, imp_buf, sel_buf, sq0_buf, sq1_buf):
    i = pl.program_id(2)
    tq = Q_TILE
    m_lanes = GQA_REP * tq
    e = HEAD_DIM
    t0 = i * tq
    n_chunks = kc_ref.shape[2]
    n_sel = sel_buf.shape[0]
    ov_per_tile, ov_rows = _overlap_window(n_sel)

    qT = qT_ref[0]
    qTm = jnp.concatenate([qT[r * e:(r + 1) * e] for r in range(GQA_REP)], axis=1)
    lane = lax.broadcasted_iota(jnp.int32, (1, m_lanes), 1)
    tok = t0 + (lane & (tq - 1))
    row_k = lax.broadcasted_iota(jnp.int32, (KEY_TILE, 1), 0)

    cmp_tiles = s_buf.shape[1] // KEY_TILE
    cmp_rows = cmp_tiles * KEY_TILE
    last_cmp_tile = ((t0 + tq - CMP_LEN) // CMP_STRIDE) // KEY_TILE
    n_vis = jnp.minimum(last_cmp_tile // cmp_tiles + 1, n_chunks // cmp_tiles)
    row_c = lax.broadcasted_iota(jnp.int32, (cmp_rows, 1), 0)

    def cmp_scores(c, m):
        kc = jnp.concatenate([kc_ref[0, 0, c * cmp_tiles + k] for k in range(cmp_tiles)], axis=0)
        last_tok = (c * cmp_rows + row_c) * CMP_STRIDE + (CMP_LEN - 1)
        s = jnp.where(last_tok <= tok, _dot(kc, qTm), NEG_INF)
        s_buf[c] = s
        return jnp.maximum(m, jnp.max(s, axis=0, keepdims=True))

    m_c = lax.fori_loop(0, n_vis, cmp_scores, jnp.full((1, m_lanes), NEG_INF, F32))
    m_c = jnp.where(m_c == NEG_INF, 0.0, m_c)

    imp_buf[...] = jnp.zeros(imp_buf.shape, F32)

    def cmp_accum(c, carry):
        l, acc = carry
        p = jnp.exp2(s_buf[c] - m_c)
        pb = p.astype(BF16)
        for k in range(cmp_tiles):
            tile = c * cmp_tiles + k
            pk = pb[k * KEY_TILE:(k + 1) * KEY_TILE]
            rows = pl.ds(pl.multiple_of(jnp.minimum(tile * ov_per_tile, n_sel - ov_rows), 16), ov_rows)
            imp_buf[rows, :] += _dot(ov_ref[tile], pk)
            acc = acc + _dot(vcT_ref[0, 0, tile], pk)
        return l + jnp.sum(p, axis=0, keepdims=True), acc

    l_c, acc_c = lax.fori_loop(
        0, n_vis, cmp_accum, (jnp.zeros((1, m_lanes), F32), jnp.zeros((e, m_lanes), F32)))
    inv_l = 1.0 / jnp.maximum(l_c, 1e-30)
    o_cmp = acc_c * inv_l
    imp_n = imp_buf[...] * inv_l
    imp = imp_n[:, 0:tq]
    for r in range(1, GQA_REP):
        imp = imp + imp_n[:, r * tq:(r + 1) * tq]

    sidx = lax.broadcasted_iota(jnp.int32, (n_sel, tq), 0)
    cur = (t0 + lax.broadcasted_iota(jnp.int32, (1, tq), 1)) // SEL_LEN
    n_forced = 3
    forced = (sidx == 0) | (sidx == cur) | (sidx == cur - 1)
    vals = jnp.where((sidx <= cur) & jnp.logical_not(forced), imp, NEG_INF)

    def pick(_, rest):
        idx = lax.broadcasted_iota(jnp.int32, rest.shape, 0)
        top = jnp.max(rest, axis=0, keepdims=True)
        first = jnp.min(jnp.where(rest == top, idx, n_sel), axis=0, keepdims=True)
        return jnp.where(idx == first, NEG_INF, rest)

    n_rounds = min(N_SELECT, n_sel) - n_forced
    half = n_sel // 2
    if half % 8 == 0:
        def pick_all(v):
            return lax.fori_loop(0, n_rounds, pick, v)

        def pick_first_half(v):
            return jnp.concatenate([lax.fori_loop(0, n_rounds, pick, v[:half]), v[half:]], axis=0)

        rest = lax.cond((t0 + tq - 1) // SEL_LEN < half, pick_first_half, pick_all, vals)
    else:
        rest = lax.fori_loop(0, n_rounds, pick, vals)
    sel_buf[...] = jnp.where(forced | (rest < vals), 0.0, NEG_INF)

    def update(blocks, bias, vT_tiles, m, acc):
        part = None
        for blk, b in zip(blocks, bias):
            t = jnp.max(blk.reshape(blk.shape[0] // 8, 8, m_lanes), axis=0) + b
            part = t if part is None else jnp.maximum(part, t)
        m_new = jnp.maximum(m, jnp.max(part, axis=0, keepdims=True))
        m_safe = jnp.where(m_new == NEG_INF, 0.0, m_new)
        pb = jnp.concatenate([jnp.exp2(blk + (b - m_safe)) for blk, b in zip(blocks, bias)], axis=0).astype(BF16)
        pv = None
        for k, vT in enumerate(vT_tiles):
            t = _dot(vT, pb[k * KEY_TILE:(k + 1) * KEY_TILE])
            pv = t if pv is None else pv + t
        return m_new, jnp.exp2(m - m_safe) * acc + pv

    def empty_state(v_ref):
        return jnp.full((1, m_lanes), NEG_INF, F32), jnp.zeros((v_ref.shape[2], m_lanes), F32)

    def finish(acc):
        return acc[:e] / jnp.maximum(acc[e:e + 1], 1e-30)

    q_tiles = tq // KEY_TILE
    first_diag = i * q_tiles
    q_col = lane & (tq - 1)

    def tri_le(d):
        return jnp.where(row_k + d * KEY_TILE <= q_col, 0.0, NEG_INF)

    def tri_gt(d):
        return jnp.where(row_k + d * KEY_TILE > q_col, 0.0, NEG_INF)

    blocks_per_tile = KEY_TILE // SEL_LEN
    chunk_blocks = SEL_TILES * blocks_per_tile
    n_sel_chunks = ksel_ref.shape[2] // SEL_TILES

    def sel_scores(c):
        base = jnp.minimum(c, n_sel_chunks - 1) * SEL_TILES
        ks = jnp.concatenate([ksel_ref[0, 0, base + k] for k in range(SEL_TILES)], axis=0)
        return _dot(ks, qTm)

    def block_rows(first_block, n):
        return [jnp.concatenate([sel_buf[pl.ds(first_block + h, 1), :]] * GQA_REP, axis=1) for h in range(n)]

    def sel_chunk(sq_ref, c, state):
        s = sq_ref[...]
        blocks = [s[h * SEL_LEN:(h + 1) * SEL_LEN] for h in range(chunk_blocks)]
        rows = block_rows(c * chunk_blocks, chunk_blocks)
        bias = [rows[h] + jnp.where(c * SEL_TILES + h // blocks_per_tile < first_diag, 0.0, NEG_INF)
                for h in range(chunk_blocks)]
        return update(blocks, bias, [vTsel_ref[0, c * SEL_TILES + k, :, :] for k in range(SEL_TILES)], *state)

    def sel_pair(cp, state):
        c0 = 2 * cp
        sq1_buf[...] = sel_scores(c0 + 1)
        state = sel_chunk(sq0_buf, c0, state)
        sq0_buf[...] = sel_scores(c0 + 2)
        return sel_chunk(sq1_buf, c0 + 1, state)

    def sel_quad(cq, state):
        return sel_pair(2 * cq + 1, sel_pair(2 * cq, state))

    n_before = (first_diag + SEL_TILES - 1) // SEL_TILES
    n_pairs = n_before // 2
    n_quads = n_pairs // 2
    sq0_buf[...] = sel_scores(0)
    state = lax.fori_loop(0, n_quads, sel_quad, empty_state(vTsel_ref))
    state = lax.fori_loop(2 * n_quads, n_pairs, sel_pair, state)
    state = lax.cond(n_before % 2 == 1, lambda st: sel_chunk(sq0_buf, 2 * n_pairs, st), lambda st: st, state)
    kd = jnp.concatenate([ksel_ref[0, 0, first_diag + d] for d in range(q_tiles)], axis=0)
    s_d = _dot(kd, qTm) + jnp.concatenate([tri_le(d) for d in range(q_tiles)], axis=0)
    _, acc_s = update([s_d[h * SEL_LEN:(h + 1) * SEL_LEN] for h in range(q_tiles * blocks_per_tile)],
                      block_rows(first_diag * blocks_per_tile, q_tiles * blocks_per_tile),
                      [vTsel_ref[0, first_diag + d, :, :] for d in range(q_tiles)], *state)
    o_sel = finish(acc_s)

    n_back = WIN_LEN // KEY_TILE
    n_win = n_back + q_tiles
    first_tile = first_diag - n_back
    kw = jnp.concatenate([kwin_ref[0, 0, jnp.maximum(first_tile + d, 0)] for d in range(n_win)], axis=0)
    s_w = _dot(kw, qTm)
    w_blocks = [s_w[d * KEY_TILE:(d + 1) * KEY_TILE] for d in range(n_win)]
    for d in range(q_tiles):
        w_blocks[d] = w_blocks[d] + tri_gt(d)
        w_blocks[n_back + d] = w_blocks[n_back + d] + tri_le(d)
    w_bias = [jnp.where(first_tile + d >= 0, 0.0, NEG_INF) for d in range(n_win)]
    _, acc_w = update(w_blocks, w_bias,
                      [vTwin_ref[0, jnp.maximum(first_tile + d, 0), :, :] for d in range(n_win)],
                      *empty_state(vTwin_ref))
    o_win = finish(acc_w)

    def gate(branch):
        g = gT_ref[0]
        return jnp.concatenate([g[branch * GQA_REP + r:branch * GQA_REP + r + 1, :] for r in range(GQA_REP)], axis=1)

    oT = o_cmp * gate(0) + o_sel * gate(1) + o_win * gate(2)
    o_rows = jnp.concatenate([oT[:, r * tq:(r + 1) * tq] for r in range(GQA_REP)], axis=0)
    o_ref[0] = o_rows.T.astype(BF16)


def _overlap_window(n_sel):
    per_tile = KEY_TILE * CMP_STRIDE // SEL_LEN
    return per_tile, min(per_tile + 16, n_sel)


def _overlap_tiles(n_sel, n_cmp_pad):
    cs = np.arange(n_cmp_pad)[None, :] * CMP_STRIDE
    ss = np.arange(n_sel)[:, None] * SEL_LEN
    ov = ((cs < ss + SEL_LEN) & (cs + CMP_LEN > ss)).astype(np.float32)
    per_tile, rows = _overlap_window(n_sel)
    tiles = []
    for k in range(n_cmp_pad // KEY_TILE):
        start = min(k * per_tile, n_sel - rows)
        cols = ov[:, k * KEY_TILE:(k + 1) * KEY_TILE]
        assert not cols[:start].any() and not cols[start + rows:].any()
        tiles.append(cols[start:start + rows])
    return jnp.asarray(np.stack(tiles), BF16)


def _nsa_attention(qT, kc, vcT, ksel, vTsel, kwin, vTwin, gT):
    B, D, S = qT.shape
    G, E = N_KV_HEADS, HEAD_DIM
    n_sel = S // SEL_LEN
    n_tiles = S // KEY_TILE
    n_chunks = kc.shape[2] // KEY_TILE
    kc5 = kc.reshape(B, G, n_chunks, KEY_TILE, E)
    ksel5 = ksel.reshape(B, G, n_tiles, KEY_TILE, E)
    kwin5 = kwin.reshape(B, G, n_tiles, KEY_TILE, E)
    ov = _overlap_tiles(n_sel, n_chunks * KEY_TILE)
    cmp_tiles = 2 if n_chunks % 2 == 0 else 1
    kspec = pl.BlockSpec((1, 1, n_tiles, KEY_TILE, E), lambda b, g, i: (b, g, 0, 0, 0))
    vspec = pl.BlockSpec((1, n_tiles, V_EXT, KEY_TILE), lambda b, g, i: (b, 0, g, 0))
    m_lanes = GQA_REP * Q_TILE
    return pl.pallas_call(
        _nsa_attn_kernel,
        out_shape=jax.ShapeDtypeStruct((B, S, D), BF16),
        grid=(B, G, S // Q_TILE),
        in_specs=[
            pl.BlockSpec((1, GQA_REP * E, Q_TILE), lambda b, g, i: (b, g, i)),
            pl.BlockSpec((1, 1, n_chunks, KEY_TILE, E), lambda b, g, i: (b, g, 0, 0, 0)),
            pl.BlockSpec((1, 1, n_chunks, E, KEY_TILE), lambda b, g, i: (b, g, 0, 0, 0)),
            pl.BlockSpec(ov.shape, lambda b, g, i: (0, 0, 0)),
            kspec, vspec, kspec, vspec,
            pl.BlockSpec((1, 16, Q_TILE), lambda b, g, i: (b, g, i)),
        ],
        out_specs=pl.BlockSpec((1, Q_TILE, GQA_REP * E), lambda b, g, i: (b, i, g)),
        scratch_shapes=[
            pltpu.VMEM((n_chunks // cmp_tiles, cmp_tiles * KEY_TILE, m_lanes), F32),
            pltpu.VMEM((n_sel, m_lanes), F32),
            pltpu.VMEM((n_sel, Q_TILE), F32),
            pltpu.VMEM((SEL_TILES * KEY_TILE, m_lanes), F32),
            pltpu.VMEM((SEL_TILES * KEY_TILE, m_lanes), F32),
        ],
        compiler_params=_cparams(("parallel", "parallel", "arbitrary")),
        name="nsa_attention",
    )(qT, kc5, vcT, ov, ksel5, vTsel, kwin5, vTwin, gT)


def _layer_norm(z, g, b):
    mu = jnp.mean(z, axis=-1, keepdims=True)
    d = z - mu
    var = jnp.mean(d * d, axis=-1, keepdims=True)
    return d * lax.rsqrt(var + LN_EPS) * g + b


def _proj_ln_kernel(o_ref, x_ref, w_ref, gate_ref, g_ref, b_ref, sh_ref, sc_ref, rw_ref,
                    x1_ref, h_ref, lgT_ref):
    y = _dot(o_ref[0], w_ref[...])
    xn = _layer_norm(DN_ALPHA * x_ref[0] + gate_ref[0] * y, g_ref[...], b_ref[...])
    x1_ref[0] = xn
    h = xn * (1.0 + sc_ref[0]) + sh_ref[0]
    h_ref[0] = h
    lgT_ref[0] = lax.dot_general(rw_ref[...], h, (((1,), (1,)), ((), ())),
                                 preferred_element_type=F32, precision=HIGHEST)


def _proj_ln(o, x, w_o, gate, ln_g, ln_b, shift2, scale2, router_w):
    B, S, D = x.shape
    tm = ROW_TILE
    rw = router_w.T
    vec = pl.BlockSpec((1, 1, D), lambda b, i: (b, 0, 0))
    par = pl.BlockSpec((1, D), lambda b, i: (0, 0))
    row = pl.BlockSpec((1, tm, D), lambda b, i: (b, i, 0))
    return pl.pallas_call(
        _proj_ln_kernel,
        out_shape=[
            jax.ShapeDtypeStruct((B, S, D), F32),
            jax.ShapeDtypeStruct((B, S, D), F32),
            jax.ShapeDtypeStruct((B, N_EXPERTS, S), F32),
        ],
        grid=(B, S // tm),
        in_specs=[row, row, pl.BlockSpec((D, D), lambda b, i: (0, 0)), vec, par, par, vec, vec,
                  pl.BlockSpec((N_EXPERTS, D), lambda b, i: (0, 0))],
        out_specs=[row, row, pl.BlockSpec((1, N_EXPERTS, tm), lambda b, i: (b, 0, i))],
        compiler_params=_cparams(("parallel", "parallel")),
        name="proj_ln",
    )(o, x, w_o.astype(BF16), gate, ln_g.reshape(1, D), ln_b.reshape(1, D), shift2, scale2, rw)


def _first_max(v, idx, big):
    top = jnp.max(v, axis=0, keepdims=True)
    first = jnp.min(jnp.where(v == top, idx, big), axis=0, keepdims=True)
    return top, first


def _route_kernel(lg_ref, rb_ref, e_ref, w_ref):
    scores = jax.nn.sigmoid(lg_ref[0])
    biased = scores + rb_ref[...]
    eidx = lax.broadcasted_iota(jnp.int32, scores.shape, 0)
    npg = EXPERTS_PER_GROUP
    best_v, best_g = None, None
    for g in range(N_GROUPS):
        v = biased[g * npg:(g + 1) * npg]
        ii = g * npg + lax.broadcasted_iota(jnp.int32, v.shape, 0)
        top1, i1 = _first_max(v, ii, N_EXPERTS)
        top2 = jnp.max(jnp.where(ii == i1, NEG_INF, v), axis=0, keepdims=True)
        gs = top1 + top2
        if g == 0:
            best_v, best_g = gs, jnp.zeros_like(i1)
        else:
            better = gs > best_v
            best_g = jnp.where(better, g, best_g)
            best_v = jnp.where(better, gs, best_v)
    masked = jnp.where(eidx // npg == best_g, biased, NEG_INF)
    _, e1 = _first_max(masked, eidx, N_EXPERTS)
    _, e2 = _first_max(jnp.where(eidx == e1, NEG_INF, masked), eidx, N_EXPERTS)
    sc1 = jnp.sum(jnp.where(eidx == e1, scores, 0.0), axis=0, keepdims=True)
    sc2 = jnp.sum(jnp.where(eidx == e2, scores, 0.0), axis=0, keepdims=True)
    tot = sc1 + sc2
    e_ref[0] = jnp.concatenate([e1, e2], axis=0)
    w_ref[0] = jnp.concatenate([sc1 / tot, sc2 / tot], axis=0)


def _route(lgT, router_b):
    B, E, S = lgT.shape
    tn = min(S, 2048)
    return pl.pallas_call(
        _route_kernel,
        out_shape=[jax.ShapeDtypeStruct((B, TOP_K, S), jnp.int32), jax.ShapeDtypeStruct((B, TOP_K, S), F32)],
        grid=(B, S // tn),
        in_specs=[pl.BlockSpec((1, E, tn), lambda b, i: (b, 0, i)), pl.BlockSpec((E, 1), lambda b, i: (0, 0))],
        out_specs=[pl.BlockSpec((1, TOP_K, tn), lambda b, i: (b, 0, i))] * 2,
        compiler_params=_cparams(("parallel", "parallel")),
        name="moe_route",
    )(lgT, router_b.reshape(E, 1))


def _dispatch_plan(eidx):
    B, K, S = eidx.shape
    n_asg = B * S * K
    e_flat = eidx.transpose(0, 2, 1).reshape(n_asg)
    chunk = LANES
    onehot = (e_flat[:, None] == jnp.arange(N_EXPERTS, dtype=jnp.int32)[None, :]).astype(F32)
    oh = onehot.reshape(n_asg // chunk, chunk, N_EXPERTS)
    tri = jnp.tril(jnp.ones((chunk, chunk), F32))
    within = jnp.einsum("ij,cjk->cik", tri, oh)
    chunk_tot = within[:, -1, :]
    chunk_end = jnp.cumsum(chunk_tot, axis=0)
    incl = within + (chunk_end - chunk_tot)[:, None, :]
    rank = (jnp.sum(incl * oh, axis=-1) - 1.0).reshape(n_asg).astype(jnp.int32)
    counts = chunk_end[-1].astype(jnp.int32)
    padded = (counts + MOE_BLOCK - 1) // MOE_BLOCK * MOE_BLOCK
    pad_ends = jnp.cumsum(padded)
    dest = (pad_ends - padded)[e_flat] + rank
    n_blk = n_asg // MOE_BLOCK + N_EXPERTS
    blk_start = jnp.arange(n_blk, dtype=jnp.int32) * MOE_BLOCK
    blk_exp = jnp.minimum(jnp.sum((pad_ends[None, :] <= blk_start[:, None]).astype(jnp.int32), axis=1),
                          N_EXPERTS - 1)
    return blk_exp, dest


def _moe_dispatch_kernel(n_steps, dest_ref, x_ref, xs_init_ref, xs_hbm, stage, sems):
    del xs_init_ref
    i = pl.program_id(0)
    tm = x_ref.shape[0]
    slot = i & 1

    def drain(s):
        for _ in range(TOP_K):
            pltpu.make_async_copy(stage.at[s], xs_hbm.at[pl.ds(0, tm), :], sems.at[s]).wait()

    @pl.when(i >= 2)
    def _():
        drain(slot)

    stage[slot] = x_ref[...]

    def issue(r, c):
        for k in range(TOP_K):
            pltpu.make_async_copy(stage.at[slot, pl.ds(r, 1), :],
                                  xs_hbm.at[pl.ds(dest_ref[0, 0, TOP_K * r + k], 1), :],
                                  sems.at[slot]).start(priority=k % 2)
        return c

    lax.fori_loop(0, tm, issue, 0, unroll=8)

    @pl.when(i == n_steps - 1)
    def _():
        drain(slot)
        if n_steps >= 2:
            drain(1 - slot)


def _moe_dispatch(h2d, dest, n_rows):
    n_tok, D = h2d.shape
    tm = MOE_IO_TILE
    n_steps = n_tok // tm
    return pl.pallas_call(
        functools.partial(_moe_dispatch_kernel, n_steps),
        out_shape=jax.ShapeDtypeStruct((n_rows, D), F32),
        grid=(n_steps,),
        in_specs=[
            pl.BlockSpec((1, 1, TOP_K * tm), lambda i: (i, 0, 0), memory_space=pltpu.SMEM),
            pl.BlockSpec((tm, D), lambda i: (i, 0)),
            pl.BlockSpec(memory_space=pl.ANY),
        ],
        out_specs=pl.BlockSpec(memory_space=pl.ANY),
        scratch_shapes=[pltpu.VMEM((2, tm, D), F32), pltpu.SemaphoreType.DMA((2,))],
        input_output_aliases={2: 0},
        compiler_params=_cparams(("arbitrary",)),
        name="moe_dispatch",
    )(dest.reshape(n_steps, 1, TOP_K * tm), h2d, jnp.zeros((n_rows, D), F32))


def _moe_expert_kernel(blk_exp_ref, x_ref, wg_ref, wu_ref, wd_ref, y_ref, wg_b, wu_b, wd_b):
    i = pl.program_id(0)

    @pl.when((i == 0) | (blk_exp_ref[i] != blk_exp_ref[jnp.maximum(i - 1, 0)]))
    def _():
        wg_b[...] = wg_ref[0, 0].astype(BF16)
        wu_b[...] = wu_ref[0, 0].astype(BF16)
        wd_b[...] = wd_ref[0, 0].astype(BF16)

    x = x_ref[...].astype(BF16)
    gate = _dot(x, wg_b[...])
    up = _dot(x, wu_b[...])
    hid = (gate * jax.nn.sigmoid(gate) * up).astype(BF16)
    y_ref[...] = _dot(hid, wd_b[...])


def _moe_experts(xs, blk_exp, layer, w_gate, w_up, w_down):
    n_rows, D = xs.shape
    F = w_gate.shape[3]
    rows = pl.BlockSpec((MOE_BLOCK, D), lambda i, be: (i, 0))
    grid_spec = pltpu.PrefetchScalarGridSpec(
        num_scalar_prefetch=1,
        grid=(n_rows // MOE_BLOCK,),
        in_specs=[
            rows,
            pl.BlockSpec((1, 1, D, F), lambda i, be: (layer, be[i], 0, 0)),
            pl.BlockSpec((1, 1, D, F), lambda i, be: (layer, be[i], 0, 0)),
            pl.BlockSpec((1, 1, F, D), lambda i, be: (layer, be[i], 0, 0)),
        ],
        out_specs=rows,
        scratch_shapes=[pltpu.VMEM((D, F), BF16), pltpu.VMEM((D, F), BF16), pltpu.VMEM((F, D), BF16)],
    )
    return pl.pallas_call(
        _moe_expert_kernel,
        out_shape=jax.ShapeDtypeStruct((n_rows, D), F32),
        grid_spec=grid_spec,
        compiler_params=_cparams(("arbitrary",)),
        name="moe_experts",
    )(blk_exp, xs, w_gate, w_up, w_down)


def _moe_combine_kernel(n_steps, dcur_ref, dnext_ref, w_ref, x_ref, gate_ref, g_ref, b_ref, ys_hbm,
                        o_ref, gbuf, sems):
    i = pl.program_id(0)
    tm = x_ref.shape[0]
    slot = i & 1

    def issue(d_ref, s):
        def body(r, c):
            for k in range(TOP_K):
                pltpu.make_async_copy(ys_hbm.at[pl.ds(d_ref[0, 0, TOP_K * r + k], 1), :],
                                      gbuf.at[s, k, pl.ds(r, 1), :], sems.at[s]).start(priority=k % 2)
            return c
        lax.fori_loop(0, tm, body, 0, unroll=8)

    @pl.when(i == 0)
    def _():
        issue(dcur_ref, slot)

    @pl.when(i + 1 < n_steps)
    def _():
        issue(dnext_ref, 1 - slot)

    for k in range(TOP_K):
        pltpu.make_async_copy(ys_hbm.at[pl.ds(0, tm), :], gbuf.at[slot, k], sems.at[slot]).wait()
    w = w_ref[...]
    y = gbuf[slot, 0] * w[:, 0:1] + gbuf[slot, 1] * w[:, 1:2]
    o_ref[...] = _layer_norm(DN_ALPHA * x_ref[...] + gate_ref[0] * y, g_ref[...], b_ref[...])


def _moe_combine_ln(ys, dest, wts, x, gate, ln_g, ln_b):
    B, S, D = x.shape
    n_tok = B * S
    tm = MOE_IO_TILE
    n_steps = n_tok // tm
    per_b = S // tm
    d3 = dest.reshape(n_steps, 1, TOP_K * tm)
    w2 = wts.transpose(0, 2, 1).reshape(n_tok, TOP_K)
    idx = lambda f: pl.BlockSpec((1, 1, TOP_K * tm), f, memory_space=pltpu.SMEM)
    par = pl.BlockSpec((1, D), lambda i: (0, 0))
    row = pl.BlockSpec((tm, D), lambda i: (i, 0))
    out = pl.pallas_call(
        functools.partial(_moe_combine_kernel, n_steps),
        out_shape=jax.ShapeDtypeStruct((n_tok, D), F32),
        grid=(n_steps,),
        in_specs=[
            idx(lambda i: (i, 0, 0)),
            idx(lambda i: (jnp.minimum(i + 1, n_steps - 1), 0, 0)),
            pl.BlockSpec((tm, TOP_K), lambda i: (i, 0)),
            row,
            pl.BlockSpec((1, 1, D), lambda i: (i // per_b, 0, 0)),
            par, par,
            pl.BlockSpec(memory_space=pl.ANY),
        ],
        out_specs=row,
        scratch_shapes=[pltpu.VMEM((2, TOP_K, tm, D), F32), pltpu.SemaphoreType.DMA((2,))],
        compiler_params=_cparams(("arbitrary",)),
        name="moe_combine_ln",
    )(d3, d3, w2, x.reshape(n_tok, D), gate, ln_g.reshape(1, D), ln_b.reshape(1, D), ys)
    return out.reshape(B, S, D)


def _moe_sublayer(x1, h2, lgT, router_b, layer, w_gate, w_up, w_down, gate, ln_g, ln_b):
    B, S, D = x1.shape
    eidx, wts = _route(lgT, router_b)
    blk_exp, dest = _dispatch_plan(eidx)
    n_rows = blk_exp.shape[0] * MOE_BLOCK
    xs = _moe_dispatch(h2.reshape(B * S, D), dest, n_rows)
    ys = _moe_experts(xs, blk_exp, layer, w_gate, w_up, w_down)
    return _moe_combine_ln(ys, dest, wts, x1, gate, ln_g, ln_b)


def _dil_weight_cols():
    d = D_MODEL
    cols = list(range(d))
    for which in range(2):
        for p in range(len(DIL_PATTERNS)):
            base = d + (p * 2 + which) * KV_COLS
            cols += list(range(base, base + KV_COLS))
    return np.asarray(cols)


def _dil_inproj_kernel(x_ref, sh_ref, sc_ref, w_ref, c_ref, s1_ref, s2_ref, q_ref, *kv_refs):
    h = (x_ref[0] * (1.0 + sc_ref[0]) + sh_ref[0]).astype(BF16)
    c, s1, s2 = c_ref[0], s1_ref[0], s2_ref[0]
    d = q_ref.shape[2]
    w = KV_COLS
    n_pat = len(DIL_PATTERNS)
    for j in range(d // w):
        a = _rope_cols(_dot(h, w_ref[:, j * w:(j + 1) * w]), c, s1, s2) * Q_SCALE
        q_ref[0, :, j * w:(j + 1) * w] = a.astype(BF16)
    for p in range(n_pat):
        a = _rope_cols(_dot(h, w_ref[:, d + p * w:d + (p + 1) * w]), c, s1, s2)
        kv_refs[p][0] = a.astype(BF16)
    for p in range(n_pat):
        a = _dot(h, w_ref[:, d + (n_pat + p) * w:d + (n_pat + p + 1) * w])
        kv_refs[n_pat + p][0] = a.astype(BF16)


def _dil_inproj(x, shift, scale, w_in, tabs):
    B, S, D = x.shape
    tm = ROW_TILE
    wp = _permute_cols(w_in, _dil_weight_cols())
    ncol = wp.shape[1]
    n_pat = len(DIL_PATTERNS)
    vec = pl.BlockSpec((1, 1, D), lambda b, i: (b, 0, 0))
    tab = pl.BlockSpec((1, tm, LANES), lambda b, i: (b, i, 0))
    kvspec = pl.BlockSpec((1, tm, KV_COLS), lambda b, i: (b, i, 0))
    outs = pl.pallas_call(
        _dil_inproj_kernel,
        out_shape=[jax.ShapeDtypeStruct((B, S, D), BF16)]
        + [jax.ShapeDtypeStruct((B, S, KV_COLS), BF16)] * (2 * n_pat),
        grid=(B, S // tm),
        in_specs=[pl.BlockSpec((1, tm, D), lambda b, i: (b, i, 0)), vec, vec,
                  pl.BlockSpec((D, ncol), lambda b, i: (0, 0)), tab, tab, tab],
        out_specs=[pl.BlockSpec((1, tm, D), lambda b, i: (b, i, 0))] + [kvspec] * (2 * n_pat),
        compiler_params=_cparams(("parallel", "parallel")),
        name="dil_inproj",
    )(x, shift, scale, wp, *tabs)
    return outs[0], outs[1:1 + n_pat], outs[1 + n_pat:]


def _dil_attn_kernel(steps, first, last, *refs):
    q_ref = refs[0]
    blk = q_ref.shape[1]
    kt = DIL_BLOCK
    n_kt = blk // kt + 1
    k_refs, v_refs = refs[1:1 + n_kt], refs[1 + n_kt:1 + 2 * n_kt]
    if first:
        acc_in = ml_in = None
        outs = refs[1 + 2 * n_kt:]
    else:
        acc_in, ml_in = refs[1 + 2 * n_kt:3 + 2 * n_kt]
        outs = refs[3 + 2 * n_kt:]
    nb = pl.program_id(2)
    e = HEAD_DIM
    gw = GQA_REP * e
    kj = lax.broadcasted_iota(jnp.int32, (n_kt * kt, blk), 0)
    qi = lax.broadcasted_iota(jnp.int32, (n_kt * kt, blk), 1)
    dist = kt + qi - kj
    valid = (dist >= 0) & (dist <= steps) & ((nb * (blk // kt) - 1) * kt + kj >= 0)
    bias = jnp.where(valid, 0.0, NEG_INF)
    bias = jnp.concatenate([bias] * GQA_REP, axis=1)
    zeros_half = jnp.zeros((e, GQA_REP * blk), F32)
    mlT_old = None if first else ml_in[0].T
    m_rows, l_rows = [], []

    def heads_to_lanes(t):
        return jnp.concatenate([t[r * e:(r + 1) * e] for r in range(GQA_REP)], axis=1)

    for g in range(N_KV_HEADS):
        seg = slice((g // 2) * LANES, (g // 2 + 1) * LANES)
        qs = slice(g * gw, (g + 1) * gw)
        qTm = heads_to_lanes(q_ref[0][:, qs].astype(F32).T)
        qT2 = jnp.concatenate([qTm, zeros_half] if g % 2 == 0 else [zeros_half, qTm], axis=0).astype(BF16)
        kcat = jnp.concatenate([r[0][:, seg] for r in k_refs], axis=0)
        s = _dot(kcat, qT2) + bias
        m_new = jnp.max(s, axis=0, keepdims=True)
        if not first:
            m_old = jnp.concatenate([mlT_old[g * GQA_REP + r:g * GQA_REP + r + 1] for r in range(GQA_REP)], axis=1)
            l_old = jnp.concatenate([mlT_old[N_Q_HEADS + g * GQA_REP + r:N_Q_HEADS + g * GQA_REP + r + 1]
                                     for r in range(GQA_REP)], axis=1)
            m_new = jnp.maximum(m_old, m_new)
            alpha = jnp.exp2(m_old - m_new)
        p = jnp.exp2(s - m_new)
        l_new = jnp.sum(p, axis=0, keepdims=True)
        vcat = jnp.concatenate([r[0][:, seg] for r in v_refs], axis=0).astype(F32)
        vT = vcat.T[(g % 2) * e:(g % 2 + 1) * e].astype(BF16)
        accT = _dot(vT, p.astype(BF16))
        if not first:
            l_new = l_new + alpha * l_old
            accT = accT + alpha * heads_to_lanes(acc_in[0][:, qs].T)
        if last:
            accT = accT / l_new
        o_rows = jnp.concatenate([accT[:, r * blk:(r + 1) * blk] for r in range(GQA_REP)], axis=0)
        outs[0][0, :, qs] = o_rows.T.astype(outs[0].dtype)
        m_rows += [m_new[:, r * blk:(r + 1) * blk] for r in range(GQA_REP)]
        l_rows += [l_new[:, r * blk:(r + 1) * blk] for r in range(GQA_REP)]
    if not last:
        pad = jnp.zeros((LANES - 2 * N_Q_HEADS, blk), F32)
        outs[1][0] = jnp.concatenate(m_rows + l_rows + [pad], axis=0).T


def _dil_attention(q, kds, vds):
    B, S, D = q.shape
    n_pat = len(DIL_PATTERNS)
    acc = ml = None
    for p, (window, dil) in enumerate(DIL_PATTERNS):
        first, last = p == 0, p == n_pat - 1
        L = S // dil
        qb = DIL_Q_TILE
        n_kt = qb // DIL_BLOCK + 1
        nblk = L // qb
        kw = KV_COLS

        def view(a):
            return a.reshape(B, L, dil * a.shape[2])

        cur = lambda b, c, n: (b, n, c)
        def key_tile(j):
            return lambda b, c, n: (b, jnp.maximum(n * (n_kt - 1) - 1 + j, 0), c)

        qspec = pl.BlockSpec((1, qb, D), cur)
        kv_specs = [pl.BlockSpec((1, DIL_BLOCK, kw), key_tile(j)) for j in range(n_kt)]
        in_specs = [qspec] + kv_specs + kv_specs
        args = [view(q)] + [view(kds[p])] * n_kt + [view(vds[p])] * n_kt
        if not first:
            in_specs += [qspec, pl.BlockSpec((1, qb, LANES), cur)]
            args += [view(acc), view(ml)]
        if last:
            out_shape = [jax.ShapeDtypeStruct((B, L, dil * D), BF16)]
            out_specs = [qspec]
        else:
            out_shape = [jax.ShapeDtypeStruct((B, L, dil * D), F32),
                         jax.ShapeDtypeStruct((B, L, dil * LANES), F32)]
            out_specs = [qspec, pl.BlockSpec((1, qb, LANES), cur)]
        res = pl.pallas_call(
            functools.partial(_dil_attn_kernel, window // dil, first, last),
            out_shape=out_shape,
            grid=(B, dil, nblk),
            in_specs=in_specs,
            out_specs=out_specs,
            compiler_params=_cparams(("parallel", "parallel", "arbitrary")),
            name=f"dil_attention_{p}",
        )(*args)
        if last:
            return res[0].reshape(B, S, D)
        acc, ml = res[0].reshape(B, S, D), res[1].reshape(B, S, LANES)


def kernel(x, c, positions, ada_w, ada_b, ln_g, ln_b, nsa_w_in, nsa_cmp_pos_k, nsa_cmp_w1_k, nsa_cmp_w2_k, nsa_cmp_pos_v, nsa_cmp_w1_v, nsa_cmp_w2_v, nsa_w_o, dil_w_in, dil_w_o, router_w, router_b, moe_w_gate, moe_w_up, moe_w_down):
    B, S, D = x.shape
    mods = _ada_mods(c, ada_w, ada_b)
    def mod(i, sub):
        m = mods[i * 2 + sub]
        return [m[:, k * D:(k + 1) * D].reshape(B, 1, D) for k in range(3)]
    tabs = _rope_tables(positions)

    for i in range(DEPTH):
        shift, scale, gate = mod(i, 0)
        shift2, scale2, gate2 = mod(i, 1)
        j = i // 2
        if i % 2 == 0:
            qT, ksel, kwin, vTsel, vTwin, kcmp, vcmp, gT = _nsa_inproj(x, shift, scale, nsa_w_in[j], tabs)
            kc, vcT = _compress(kcmp, vcmp, nsa_cmp_pos_k[j], nsa_cmp_w1_k[j], nsa_cmp_w2_k[j],
                                nsa_cmp_pos_v[j], nsa_cmp_w1_v[j], nsa_cmp_w2_v[j], tabs)
            o = _nsa_attention(qT, kc, vcT, ksel, vTsel, kwin, vTwin, gT)
            w_o = nsa_w_o[j]
        else:
            q, kds, vds = _dil_inproj(x, shift, scale, dil_w_in[j], tabs)
            o = _dil_attention(q, kds, vds)
            w_o = dil_w_o[j]
        x1, h2, lgT = _proj_ln(o, x, w_o, gate, ln_g[i, 0], ln_b[i, 0], shift2, scale2, router_w)
        x = _moe_sublayer(x1, h2, lgT, router_b, i, moe_w_gate, moe_w_up, moe_w_down,
                          gate2, ln_g[i, 1], ln_b[i, 1])
    return x
```

```python
import functools

import numpy as np
import jax
import jax.numpy as jnp
from jax import lax
from jax.experimental import pallas as pl
from jax.experimental.pallas import tpu as pltpu

F32 = jnp.float32
BF16 = jnp.bfloat16
HIGHEST = lax.Precision.HIGHEST
NEG_INF = float("-inf")

D_MODEL = 1024
DEPTH = 2
HEAD_DIM = 64
N_Q_HEADS = D_MODEL // HEAD_DIM
N_KV_HEADS = 4
GQA_REP = N_Q_HEADS // N_KV_HEADS
ROPE_DIM = HEAD_DIM // 4
ROPE_THETA = 500000.0
ATTN_SCALE = HEAD_DIM ** -0.5
LOG2_E = 1.4426950408889634
Q_SCALE = ATTN_SCALE * LOG2_E
KV_COLS = N_KV_HEADS * HEAD_DIM
N_BRANCH = 3
CMP_LEN = 32
CMP_STRIDE = 16
CMP_HIDDEN = 256
SEL_LEN = 64
N_SELECT = 16
WIN_LEN = 512
FORCE_SCORE = 1.0e4
DIL_PATTERNS = ((128, 1), (512, 4), (2048, 16))
DIL_BLOCK = 128
N_EXPERTS = 32
N_GROUPS = 4
EXPERTS_PER_GROUP = N_EXPERTS // N_GROUPS
TOP_K = 2
D_EXPERT = 512
MOE_BLOCK = 128
DN_ALPHA = (2.0 * DEPTH) ** 0.25
LN_EPS = 1e-5

LANES = 128
VMEM_LIMIT_BYTES = 48 * 1024 * 1024

Q_TILE = 256
KEY_TILE = 128
SEL_TILES = 4
V_EXT = HEAD_DIM + 16
DIL_Q_TILE = 256
ROW_TILE = 512
MOE_IO_TILE = 256


def _cparams(semantics):
    return pltpu.CompilerParams(dimension_semantics=semantics, vmem_limit_bytes=VMEM_LIMIT_BYTES)


def _dot(a, b):
    return jnp.dot(a, b, preferred_element_type=F32)


def _dot_nt(a, b):
    return lax.dot_general(a, b, (((1,), (1,)), ((), ())), preferred_element_type=F32)


def _ada_kernel(c_ref, w_ref, b_ref, o_ref):
    c = c_ref[...]
    cond = c * jax.nn.sigmoid(c)
    o_ref[0] = jnp.dot(cond, w_ref[0], preferred_element_type=F32, precision=HIGHEST) + b_ref[0]


def _ada_mods(c, ada_w, ada_b):
    B, D = c.shape
    n_sub = ada_w.shape[0] * ada_w.shape[1]
    w = ada_w.reshape(n_sub, D, 3 * D)
    b = ada_b.reshape(n_sub, 1, 3 * D)
    c8 = jnp.zeros((8, D), F32).at[:B].set(c)
    tn = 768
    out = pl.pallas_call(
        _ada_kernel,
        out_shape=jax.ShapeDtypeStruct((n_sub, 8, 3 * D), F32),
        grid=(n_sub, 3 * D // tn),
        in_specs=[
            pl.BlockSpec((8, D), lambda s, j: (0, 0)),
            pl.BlockSpec((1, D, tn), lambda s, j: (s, 0, j)),
            pl.BlockSpec((1, 1, tn), lambda s, j: (s, 0, j)),
        ],
        out_specs=pl.BlockSpec((1, 8, tn), lambda s, j: (s, 0, j)),
        compiler_params=_cparams(("parallel", "parallel")),
        name="ada_mods",
    )(c8, w, b)
    return out[:, :B]


def _rope_tab_kernel(pos_ref, inv_ref, sg1_ref, sg2_ref, c_ref, s1_ref, s2_ref):
    ang = pos_ref[0] * inv_ref[...]
    sin = jnp.sin(ang)
    c_ref[0] = jnp.cos(ang)
    s1_ref[0] = sin * sg1_ref[...]
    s2_ref[0] = sin * sg2_ref[...]


def _rope_tables(positions):
    B, S = positions.shape
    half = ROPE_DIM // 2
    inv = ROPE_THETA ** (-jnp.arange(half, dtype=F32) * (2.0 / ROPE_DIM))
    li = np.arange(LANES) % HEAD_DIM
    in_rope = li < ROPE_DIM
    inv_row = jnp.where(jnp.asarray(in_rope), inv[li % half], 0.0).reshape(1, LANES)
    sg1 = jnp.asarray(np.where(li < half, -1.0, 0.0), F32).reshape(1, LANES)
    sg2 = jnp.asarray(np.where((li >= half) & in_rope, 1.0, 0.0), F32).reshape(1, LANES)
    pos = positions.astype(F32).reshape(B, S, 1)
    tm = min(S, 2048)
    row = pl.BlockSpec((1, LANES), lambda b, i: (0, 0))
    tab = pl.BlockSpec((1, tm, LANES), lambda b, i: (b, i, 0))
    return pl.pallas_call(
        _rope_tab_kernel,
        out_shape=[jax.ShapeDtypeStruct((B, S, LANES), F32)] * 3,
        grid=(B, S // tm),
        in_specs=[pl.BlockSpec((1, tm, 1), lambda b, i: (b, i, 0)), row, row, row],
        out_specs=[tab, tab, tab],
        compiler_params=_cparams(("parallel", "parallel")),
        name="rope_tables",
    )(pos, inv_row, sg1, sg2)


def _rope128(t, c, s1, s2):
    return t * c + pltpu.roll(t, LANES - ROPE_DIM // 2, 1) * s1 + pltpu.roll(t, ROPE_DIM // 2, 1) * s2


def _rope_cols(a, c, s1, s2):
    n = a.shape[1] // LANES
    return jnp.concatenate(
        [_rope128(a[:, k * LANES:(k + 1) * LANES], c, s1, s2) for k in range(n)], axis=1)


def _nsa_weight_cols():
    d = D_MODEL
    def kv(branch, which):
        base = d + (branch * 2 + which) * KV_COLS
        return list(range(base, base + KV_COLS))
    cols = list(range(d))
    cols += kv(1, 0) + kv(2, 0) + kv(1, 1) + kv(2, 1) + kv(0, 0) + kv(0, 1)
    gate0 = d + N_BRANCH * 2 * KV_COLS
    gcols = [-1] * LANES
    for g in range(N_KV_HEADS):
        for br in range(N_BRANCH):
            for r in range(GQA_REP):
                gcols[g * 16 + br * GQA_REP + r] = gate0 + (g * GQA_REP + r) * N_BRANCH + br
    return np.asarray(cols + gcols)


def _permute_cols(w, cols):
    picked = w[:, np.maximum(cols, 0)]
    return jnp.where(jnp.asarray(cols >= 0)[None, :], picked, 0.0).astype(BF16)


def _nsa_inproj_kernel(x_ref, sh_ref, sc_ref, w_ref, c_ref, s1_ref, s2_ref,
                       qT_ref, ksel_ref, kwin_ref, vTsel_ref, vTwin_ref, kcmp_ref, vcmp_ref, gT_ref):
    tm = x_ref.shape[1]
    h = (x_ref[0] * (1.0 + sc_ref[0]) + sh_ref[0]).astype(BF16)
    c, s1, s2 = c_ref[0], s1_ref[0], s2_ref[0]
    w = KV_COLS

    def proj(j, n=w):
        return _dot(h, w_ref[:, j * w:j * w + n])

    for j in range(4):
        a = _rope_cols(proj(j), c, s1, s2) * Q_SCALE
        qT_ref[0, j * w:(j + 1) * w, :] = a.T.astype(BF16)
    for j, ref in ((4, ksel_ref), (5, kwin_ref)):
        a = _rope_cols(proj(j), c, s1, s2)
        for g in range(N_KV_HEADS):
            ref[0, g] = a[:, g * HEAD_DIM:(g + 1) * HEAD_DIM].astype(BF16)
    ones_rows = jnp.ones((V_EXT - HEAD_DIM, KEY_TILE), BF16)
    for j, ref in ((6, vTsel_ref), (7, vTwin_ref)):
        aT = proj(j).T.astype(BF16)
        for k in range(tm // KEY_TILE):
            for g in range(N_KV_HEADS):
                ref[0, k, g * V_EXT:g * V_EXT + HEAD_DIM] = aT[g * HEAD_DIM:(g + 1) * HEAD_DIM,
                                                               k * KEY_TILE:(k + 1) * KEY_TILE]
                ref[0, k, g * V_EXT + HEAD_DIM:(g + 1) * V_EXT] = ones_rows
    for j, ref in ((8, kcmp_ref), (9, vcmp_ref)):
        a = proj(j)
        for g in range(N_KV_HEADS):
            ref[0, g] = a[:, g * HEAD_DIM:(g + 1) * HEAD_DIM].astype(BF16)
    gates = jax.nn.sigmoid(proj(10, LANES))
    gT_ref[0] = gates.T[:4 * 16]


def _nsa_inproj(x, shift, scale, w_in, tabs):
    B, S, D = x.shape
    tm = ROW_TILE
    wp = _permute_cols(w_in, _nsa_weight_cols())
    ncol = wp.shape[1]
    vec = pl.BlockSpec((1, 1, D), lambda b, i: (b, 0, 0))
    tab = pl.BlockSpec((1, tm, LANES), lambda b, i: (b, i, 0))
    nat = pl.BlockSpec((1, N_KV_HEADS, tm, HEAD_DIM), lambda b, i: (b, 0, i, 0))
    vt = pl.BlockSpec((1, tm // KEY_TILE, N_KV_HEADS * V_EXT, KEY_TILE), lambda b, i: (b, i, 0, 0))
    nat_shape = jax.ShapeDtypeStruct((B, N_KV_HEADS, S, HEAD_DIM), BF16)
    vt_shape = jax.ShapeDtypeStruct((B, S // KEY_TILE, N_KV_HEADS * V_EXT, KEY_TILE), BF16)
    return pl.pallas_call(
        _nsa_inproj_kernel,
        out_shape=[
            jax.ShapeDtypeStruct((B, D, S), BF16),
            nat_shape, nat_shape,
            vt_shape, vt_shape,
            nat_shape, nat_shape,
            jax.ShapeDtypeStruct((B, 4 * 16, S), F32),
        ],
        grid=(B, S // tm),
        in_specs=[
            pl.BlockSpec((1, tm, D), lambda b, i: (b, i, 0)), vec, vec,
            pl.BlockSpec((D, ncol), lambda b, i: (0, 0)), tab, tab, tab,
        ],
        out_specs=[
            pl.BlockSpec((1, D, tm), lambda b, i: (b, 0, i)),
            nat, nat, vt, vt, nat, nat,
            pl.BlockSpec((1, 4 * 16, tm), lambda b, i: (b, 0, i)),
        ],
        compiler_params=_cparams(("parallel", "parallel")),
        name="nsa_inproj",
    )(x, shift, scale, wp, *tabs)


def _compress_kernel(xk_ref, xv_ref, w1k_ref, w1v_ref, pk_ref, pv_ref, w2k_ref, w2vT_ref,
                     c_ref, s1_ref, s2_ref, kc_ref, vcT_ref):
    n = xk_ref.shape[2]
    half = w1k_ref.shape[0] // 2

    def hidden(x_ref, w1_ref, p_ref):
        x = x_ref[0, 0]
        first = _dot(x, w1_ref[:half])
        second = _dot(x, w1_ref[half:])
        bias = _dot(p_ref[...], w1_ref[...])[0:1]
        hid = first + pltpu.roll(second, n - 1, 0) + bias
        return jax.nn.gelu(hid).astype(BF16)

    kc = _dot(hidden(xk_ref, w1k_ref, pk_ref), w2k_ref[...])
    kc = _rope128(kc, c_ref[0], s1_ref[0], s2_ref[0])
    row = lax.broadcasted_iota(jnp.int32, kc.shape, 0)
    kc = jnp.where(row < n - 1, kc, 0.0)
    kc_ref[0, 0] = kc[:, :HEAD_DIM].astype(BF16)

    vcT = _dot_nt(w2vT_ref[...], hidden(xv_ref, w1v_ref, pv_ref))
    col = lax.broadcasted_iota(jnp.int32, vcT.shape, 1)
    vcT = jnp.where(col < n - 1, vcT, 0.0).astype(BF16)
    for k in range(n // KEY_TILE):
        vcT_ref[0, 0, k] = vcT[:, k * KEY_TILE:(k + 1) * KEY_TILE]


def _compress(kcmp, vcmp, pos_k, w1_k, w2_k, pos_v, w1_v, w2_v, tabs):
    B, G, S, E = kcmp.shape
    n = S // CMP_STRIDE
    wide = CMP_STRIDE * E
    xk = kcmp.reshape(B, G, n, wide)
    xv = vcmp.reshape(B, G, n, wide)
    def flat8(p):
        return jnp.zeros((8, CMP_LEN * E), BF16).at[0].set(p.reshape(-1).astype(BF16))
    w2k = jnp.zeros((CMP_HIDDEN, LANES), BF16).at[:, :E].set(w2_k.astype(BF16))
    w2vT = w2_v.T.astype(BF16)
    last = CMP_LEN - 1
    ctabs = [jnp.zeros((B, n, LANES), F32).at[:, :n - 1].set(t[:, last::CMP_STRIDE][:, :n - 1]) for t in tabs]
    xspec = pl.BlockSpec((1, 1, n, wide), lambda b, g: (b, g, 0, 0))
    w1spec = pl.BlockSpec((CMP_LEN * E, CMP_HIDDEN), lambda b, g: (0, 0))
    pspec = pl.BlockSpec((8, CMP_LEN * E), lambda b, g: (0, 0))
    tspec = pl.BlockSpec((1, n, LANES), lambda b, g: (b, 0, 0))
    return pl.pallas_call(
        _compress_kernel,
        out_shape=[
            jax.ShapeDtypeStruct((B, G, n, E), BF16),
            jax.ShapeDtypeStruct((B, G, n // KEY_TILE, E, KEY_TILE), BF16),
        ],
        grid=(B, G),
        in_specs=[xspec, xspec, w1spec, w1spec, pspec, pspec,
                  pl.BlockSpec((CMP_HIDDEN, LANES), lambda b, g: (0, 0)),
                  pl.BlockSpec((E, CMP_HIDDEN), lambda b, g: (0, 0)),
                  tspec, tspec, tspec],
        out_specs=[
            pl.BlockSpec((1, 1, n, E), lambda b, g: (b, g, 0, 0)),
            pl.BlockSpec((1, 1, n // KEY_TILE, E, KEY_TILE), lambda b, g: (b, g, 0, 0, 0)),
        ],
        compiler_params=_cparams(("parallel", "parallel")),
        name="nsa_compress",
    )(xk, xv, w1_k.astype(BF16), w1_v.astype(BF16), flat8(pos_k), flat8(pos_v), w2k, w2vT, *ctabs)


def _nsa_attn_kernel(qT_ref, kc_ref, vcT_ref, ov_ref, ksel_ref, vTsel_ref, kwin_ref, vTwin_ref, gT_ref,
                     o_ref, s_buf, imp_buf, sel_buf, sq0_buf, sq1_buf):
    i = pl.program_id(2)
    tq = Q_TILE
    m_lanes = GQA_REP * tq
    e = HEAD_DIM
    t0 = i * tq
    n_chunks = kc_ref.shape[2]
    n_sel = sel_buf.shape[0]
    ov_per_tile, ov_rows = _overlap_window(n_sel)

    qT = qT_ref[0]
    qTm = jnp.concatenate([qT[r * e:(r + 1) * e] for r in range(GQA_REP)], axis=1)
    lane = lax.broadcasted_iota(jnp.int32, (1, m_lanes), 1)
    tok = t0 + (lane & (tq - 1))
    row_k = lax.broadcasted_iota(jnp.int32, (KEY_TILE, 1), 0)

    cmp_tiles = s_buf.shape[1] // KEY_TILE
    cmp_rows = cmp_tiles * KEY_TILE
    last_cmp_tile = ((t0 + tq - CMP_LEN) // CMP_STRIDE) // KEY_TILE
    n_vis = jnp.minimum(last_cmp_tile // cmp_tiles + 1, n_chunks // cmp_tiles)
    row_c = lax.broadcasted_iota(jnp.int32, (cmp_rows, 1), 0)

    def cmp_scores(c, m):
        kc = jnp.concatenate([kc_ref[0, 0, c * cmp_tiles + k] for k in range(cmp_tiles)], axis=0)
        last_tok = (c * cmp_rows + row_c) * CMP_STRIDE + (CMP_LEN - 1)
        s = jnp.where(last_tok <= tok, _dot(kc, qTm), NEG_INF)
        s_buf[c] = s
        return jnp.maximum(m, jnp.max(s, axis=0, keepdims=True))

    m_c = lax.fori_loop(0, n_vis, cmp_scores, jnp.full((1, m_lanes), NEG_INF, F32))
    m_c = jnp.where(m_c == NEG_INF, 0.0, m_c)

    imp_buf[...] = jnp.zeros(imp_buf.shape, F32)

    def cmp_accum(c, carry):
        l, acc = carry
        p = jnp.exp2(s_buf[c] - m_c)
        pb = p.astype(BF16)
        for k in range(cmp_tiles):
            tile = c * cmp_tiles + k
            pk = pb[k * KEY_TILE:(k + 1) * KEY_TILE]
            rows = pl.ds(pl.multiple_of(jnp.minimum(tile * ov_per_tile, n_sel - ov_rows), 16), ov_rows)
            imp_buf[rows, :] += _dot(ov_ref[tile], pk)
            acc = acc + _dot(vcT_ref[0, 0, tile], pk)
        return l + jnp.sum(p, axis=0, keepdims=True), acc

    l_c, acc_c = lax.fori_loop(
        0, n_vis, cmp_accum, (jnp.zeros((1, m_lanes), F32), jnp.zeros((e, m_lanes), F32)))
    inv_l = 1.0 / jnp.maximum(l_c, 1e-30)
    o_cmp = acc_c * inv_l
    imp_n = imp_buf[...] * inv_l
    imp = imp_n[:, 0:tq]
    for r in range(1, GQA_REP):
        imp = imp + imp_n[:, r * tq:(r + 1) * tq]

    sidx = lax.broadcasted_iota(jnp.int32, (n_sel, tq), 0)
    cur = (t0 + lax.broadcasted_iota(jnp.int32, (1, tq), 1)) // SEL_LEN
    n_forced = 3
    forced = (sidx == 0) | (sidx == cur) | (sidx == cur - 1)
    vals = jnp.where((sidx <= cur) & jnp.logical_not(forced), imp, NEG_INF)

    def pick(_, rest):
        idx = lax.broadcasted_iota(jnp.int32, rest.shape, 0)
        top = jnp.max(rest, axis=0, keepdims=True)
        first = jnp.min(jnp.where(rest == top, idx, n_sel), axis=0, keepdims=True)
        return jnp.where(idx == first, NEG_INF, rest)

    n_rounds = min(N_SELECT, n_sel) - n_forced
    half = n_sel // 2
    if half % 8 == 0:
        def pick_all(v):
            return lax.fori_loop(0, n_rounds, pick, v)

        def pick_first_half(v):
            return jnp.concatenate([lax.fori_loop(0, n_rounds, pick, v[:half]), v[half:]], axis=0)

        rest = lax.cond((t0 + tq - 1) // SEL_LEN < half, pick_first_half, pick_all, vals)
    else:
        rest = lax.fori_loop(0, n_rounds, pick, vals)
    sel_buf[...] = jnp.where(forced | (rest < vals), 0.0, NEG_INF)

    def update(blocks, bias, vT_tiles, m, acc):
        part = None
        for blk, b in zip(blocks, bias):
            t = jnp.max(blk.reshape(blk.shape[0] // 8, 8, m_lanes), axis=0) + b
            part = t if part is None else jnp.maximum(part, t)
        m_new = jnp.maximum(m, jnp.max(part, axis=0, keepdims=True))
        m_safe = jnp.where(m_new == NEG_INF, 0.0, m_new)
        pb = jnp.concatenate([jnp.exp2(blk + (b - m_safe)) for blk, b in zip(blocks, bias)], axis=0).astype(BF16)
        pv = None
        for k, vT in enumerate(vT_tiles):
            t = _dot(vT, pb[k * KEY_TILE:(k + 1) * KEY_TILE])
            pv = t if pv is None else pv + t
        return m_new, jnp.exp2(m - m_safe) * acc + pv

    def empty_state(v_ref):
        return jnp.full((1, m_lanes), NEG_INF, F32), jnp.zeros((v_ref.shape[2], m_lanes), F32)

    def finish(acc):
        return acc[:e] / jnp.maximum(acc[e:e + 1], 1e-30)

    q_tiles = tq // KEY_TILE
    first_diag = i * q_tiles
    q_col = lane & (tq - 1)

    def tri_le(d):
        return jnp.where(row_k + d * KEY_TILE <= q_col, 0.0, NEG_INF)

    def tri_gt(d):
        return jnp.where(row_k + d * KEY_TILE > q_col, 0.0, NEG_INF)

    blocks_per_tile = KEY_TILE // SEL_LEN
    chunk_blocks = SEL_TILES * blocks_per_tile
    n_sel_chunks = ksel_ref.shape[2] // SEL_TILES

    def sel_scores(c):
        base = jnp.minimum(c, n_sel_chunks - 1) * SEL_TILES
        ks = jnp.concatenate([ksel_ref[0, 0, base + k] for k in range(SEL_TILES)], axis=0)
        return _dot(ks, qTm)

    def block_rows(first_block, n):
        return [jnp.concatenate([sel_buf[pl.ds(first_block + h, 1), :]] * GQA_REP, axis=1) for h in range(n)]

    def sel_chunk(sq_ref, c, state):
        s = sq_ref[...]
        blocks = [s[h * SEL_LEN:(h + 1) * SEL_LEN] for h in range(chunk_blocks)]
        rows = block_rows(c * chunk_blocks, chunk_blocks)
        bias = [rows[h] + jnp.where(c * SEL_TILES + h // blocks_per_tile < first_diag, 0.0, NEG_INF)
                for h in range(chunk_blocks)]
        return update(blocks, bias, [vTsel_ref[0, c * SEL_TILES + k, :, :] for k in range(SEL_TILES)], *state)

    def sel_pair(cp, state):
        c0 = 2 * cp
        sq1_buf[...] = sel_scores(c0 + 1)
        state = sel_chunk(sq0_buf, c0, state)
        sq0_buf[...] = sel_scores(c0 + 2)
        return sel_chunk(sq1_buf, c0 + 1, state)

    pairs_per_trip = 3

    def sel_group(cg, state):
        for k in range(pairs_per_trip):
            state = sel_pair(pairs_per_trip * cg + k, state)
        return state

    n_before = (first_diag + SEL_TILES - 1) // SEL_TILES
    n_pairs = n_before // 2
    n_groups = n_pairs // pairs_per_trip
    sq0_buf[...] = sel_scores(0)
    state = lax.fori_loop(0, n_groups, sel_group, empty_state(vTsel_ref))
    state = lax.fori_loop(pairs_per_trip * n_groups, n_pairs, sel_pair, state)
    state = lax.cond(n_before % 2 == 1, lambda st: sel_chunk(sq0_buf, 2 * n_pairs, st), lambda st: st, state)
    kd = jnp.concatenate([ksel_ref[0, 0, first_diag + d] for d in range(q_tiles)], axis=0)
    s_d = _dot(kd, qTm) + jnp.concatenate([tri_le(d) for d in range(q_tiles)], axis=0)
    _, acc_s = update([s_d[h * SEL_LEN:(h + 1) * SEL_LEN] for h in range(q_tiles * blocks_per_tile)],
                      block_rows(first_diag * blocks_per_tile, q_tiles * blocks_per_tile),
                      [vTsel_ref[0, first_diag + d, :, :] for d in range(q_tiles)], *state)
    o_sel = finish(acc_s)

    n_back = WIN_LEN // KEY_TILE
    n_win = n_back + q_tiles
    first_tile = first_diag - n_back
    kw = jnp.concatenate([kwin_ref[0, 0, jnp.maximum(first_tile + d, 0)] for d in range(n_win)], axis=0)
    s_w = _dot(kw, qTm)
    w_blocks = [s_w[d * KEY_TILE:(d + 1) * KEY_TILE] for d in range(n_win)]
    for d in range(q_tiles):
        w_blocks[d] = w_blocks[d] + tri_gt(d)
        w_blocks[n_back + d] = w_blocks[n_back + d] + tri_le(d)
    w_bias = [jnp.where(first_tile + d >= 0, 0.0, NEG_INF) for d in range(n_win)]
    _, acc_w = update(w_blocks, w_bias,
                      [vTwin_ref[0, jnp.maximum(first_tile + d, 0), :, :] for d in range(n_win)],
                      *empty_state(vTwin_ref))
    o_win = finish(acc_w)

    def gate(branch):
        g = gT_ref[0]
        return jnp.concatenate([g[branch * GQA_REP + r:branch * GQA_REP + r + 1, :] for r in range(GQA_REP)], axis=1)

    oT = o_cmp * gate(0) + o_sel * gate(1) + o_win * gate(2)
    o_rows = jnp.concatenate([oT[:, r * tq:(r + 1) * tq] for r in range(GQA_REP)], axis=0)
    o_ref[0] = o_rows.T.astype(BF16)


def _overlap_window(n_sel):
    per_tile = KEY_TILE * CMP_STRIDE // SEL_LEN
    return per_tile, min(per_tile + 16, n_sel)


def _overlap_tiles(n_sel, n_cmp_pad):
    cs = np.arange(n_cmp_pad)[None, :] * CMP_STRIDE
    ss = np.arange(n_sel)[:, None] * SEL_LEN
    ov = ((cs < ss + SEL_LEN) & (cs + CMP_LEN > ss)).astype(np.float32)
    per_tile, rows = _overlap_window(n_sel)
    tiles = []
    for k in range(n_cmp_pad // KEY_TILE):
        start = min(k * per_tile, n_sel - rows)
        cols = ov[:, k * KEY_TILE:(k + 1) * KEY_TILE]
        assert not cols[:start].any() and not cols[start + rows:].any()
        tiles.append(cols[start:start + rows])
    return jnp.asarray(np.stack(tiles), BF16)


def _nsa_attention(qT, kc, vcT, ksel, vTsel, kwin, vTwin, gT):
    B, D, S = qT.shape
    G, E = N_KV_HEADS, HEAD_DIM
    n_sel = S // SEL_LEN
    n_tiles = S // KEY_TILE
    n_chunks = kc.shape[2] // KEY_TILE
    kc5 = kc.reshape(B, G, n_chunks, KEY_TILE, E)
    ksel5 = ksel.reshape(B, G, n_tiles, KEY_TILE, E)
    kwin5 = kwin.reshape(B, G, n_tiles, KEY_TILE, E)
    ov = _overlap_tiles(n_sel, n_chunks * KEY_TILE)
    cmp_tiles = 2 if n_chunks % 2 == 0 else 1
    kspec = pl.BlockSpec((1, 1, n_tiles, KEY_TILE, E), lambda b, g, i: (b, g, 0, 0, 0))
    vspec = pl.BlockSpec((1, n_tiles, V_EXT, KEY_TILE), lambda b, g, i: (b, 0, g, 0))
    m_lanes = GQA_REP * Q_TILE
    return pl.pallas_call(
        _nsa_attn_kernel,
        out_shape=jax.ShapeDtypeStruct((B, S, D), BF16),
        grid=(B, G, S // Q_TILE),
        in_specs=[
            pl.BlockSpec((1, GQA_REP * E, Q_TILE), lambda b, g, i: (b, g, i)),
            pl.BlockSpec((1, 1, n_chunks, KEY_TILE, E), lambda b, g, i: (b, g, 0, 0, 0)),
            pl.BlockSpec((1, 1, n_chunks, E, KEY_TILE), lambda b, g, i: (b, g, 0, 0, 0)),
            pl.BlockSpec(ov.shape, lambda b, g, i: (0, 0, 0)),
            kspec, vspec, kspec, vspec,
            pl.BlockSpec((1, 16, Q_TILE), lambda b, g, i: (b, g, i)),
        ],
        out_specs=pl.BlockSpec((1, Q_TILE, GQA_REP * E), lambda b, g, i: (b, i, g)),
        scratch_shapes=[
            pltpu.VMEM((n_chunks // cmp_tiles, cmp_tiles * KEY_TILE, m_lanes), F32),
            pltpu.VMEM((n_sel, m_lanes), F32),
            pltpu.VMEM((n_sel, Q_TILE), F32),
            pltpu.VMEM((SEL_TILES * KEY_TILE, m_lanes), F32),
            pltpu.VMEM((SEL_TILES * KEY_TILE, m_lanes), F32),
        ],
        compiler_params=_cparams(("parallel", "parallel", "arbitrary")),
        name="nsa_attention",
    )(qT, kc5, vcT, ov, ksel5, vTsel, kwin5, vTwin, gT)


def _layer_norm(z, g, b):
    mu = jnp.mean(z, axis=-1, keepdims=True)
    d = z - mu
    var = jnp.mean(d * d, axis=-1, keepdims=True)
    return d * lax.rsqrt(var + LN_EPS) * g + b


def _proj_ln_kernel(o_ref, x_ref, w_ref, gate_ref, g_ref, b_ref, sh_ref, sc_ref, rw_ref,
                    x1_ref, h_ref, lgT_ref):
    y = _dot(o_ref[0], w_ref[...])
    xn = _layer_norm(DN_ALPHA * x_ref[0] + gate_ref[0] * y, g_ref[...], b_ref[...])
    x1_ref[0] = xn
    h = xn * (1.0 + sc_ref[0]) + sh_ref[0]
    h_ref[0] = h
    lgT_ref[0] = lax.dot_general(rw_ref[...], h, (((1,), (1,)), ((), ())),
                                 preferred_element_type=F32, precision=HIGHEST)


def _proj_ln(o, x, w_o, gate, ln_g, ln_b, shift2, scale2, router_w):
    B, S, D = x.shape
    tm = ROW_TILE
    rw = router_w.T
    vec = pl.BlockSpec((1, 1, D), lambda b, i: (b, 0, 0))
    par = pl.BlockSpec((1, D), lambda b, i: (0, 0))
    row = pl.BlockSpec((1, tm, D), lambda b, i: (b, i, 0))
    return pl.pallas_call(
        _proj_ln_kernel,
        out_shape=[
            jax.ShapeDtypeStruct((B, S, D), F32),
            jax.ShapeDtypeStruct((B, S, D), F32),
            jax.ShapeDtypeStruct((B, N_EXPERTS, S), F32),
        ],
        grid=(B, S // tm),
        in_specs=[row, row, pl.BlockSpec((D, D), lambda b, i: (0, 0)), vec, par, par, vec, vec,
                  pl.BlockSpec((N_EXPERTS, D), lambda b, i: (0, 0))],
        out_specs=[row, row, pl.BlockSpec((1, N_EXPERTS, tm), lambda b, i: (b, 0, i))],
        compiler_params=_cparams(("parallel", "parallel")),
        name="proj_ln",
    )(o, x, w_o.astype(BF16), gate, ln_g.reshape(1, D), ln_b.reshape(1, D), shift2, scale2, rw)


def _first_max(v, idx, big):
    top = jnp.max(v, axis=0, keepdims=True)
    first = jnp.min(jnp.where(v == top, idx, big), axis=0, keepdims=True)
    return top, first


def _route_kernel(lg_ref, rb_ref, e_ref, w_ref):
    scores = jax.nn.sigmoid(lg_ref[0])
    biased = scores + rb_ref[...]
    eidx = lax.broadcasted_iota(jnp.int32, scores.shape, 0)
    npg = EXPERTS_PER_GROUP
    best_v, best_g = None, None
    for g in range(N_GROUPS):
        v = biased[g * npg:(g + 1) * npg]
        ii = g * npg + lax.broadcasted_iota(jnp.int32, v.shape, 0)
        top1, i1 = _first_max(v, ii, N_EXPERTS)
        top2 = jnp.max(jnp.where(ii == i1, NEG_INF, v), axis=0, keepdims=True)
        gs = top1 + top2
        if g == 0:
            best_v, best_g = gs, jnp.zeros_like(i1)
        else:
            better = gs > best_v
            best_g = jnp.where(better, g, best_g)
            best_v = jnp.where(better, gs, best_v)
    masked = jnp.where(eidx // npg == best_g, biased, NEG_INF)
    _, e1 = _first_max(masked, eidx, N_EXPERTS)
    _, e2 = _first_max(jnp.where(eidx == e1, NEG_INF, masked), eidx, N_EXPERTS)
    sc1 = jnp.sum(jnp.where(eidx == e1, scores, 0.0), axis=0, keepdims=True)
    sc2 = jnp.sum(jnp.where(eidx == e2, scores, 0.0), axis=0, keepdims=True)
    tot = sc1 + sc2
    e_ref[0] = jnp.concatenate([e1, e2], axis=0)
    w_ref[0] = jnp.concatenate([sc1 / tot, sc2 / tot], axis=0)


def _route(lgT, router_b):
    B, E, S = lgT.shape
    tn = min(S, 2048)
    return pl.pallas_call(
        _route_kernel,
        out_shape=[jax.ShapeDtypeStruct((B, TOP_K, S), jnp.int32), jax.ShapeDtypeStruct((B, TOP_K, S), F32)],
        grid=(B, S // tn),
        in_specs=[pl.BlockSpec((1, E, tn), lambda b, i: (b, 0, i)), pl.BlockSpec((E, 1), lambda b, i: (0, 0))],
        out_specs=[pl.BlockSpec((1, TOP_K, tn), lambda b, i: (b, 0, i))] * 2,
        compiler_params=_cparams(("parallel", "parallel")),
        name="moe_route",
    )(lgT, router_b.reshape(E, 1))


def _dispatch_plan(eidx):
    B, K, S = eidx.shape
    n_asg = B * S * K
    e_flat = eidx.transpose(0, 2, 1).reshape(n_asg)
    chunk = LANES
    onehot = (e_flat[:, None] == jnp.arange(N_EXPERTS, dtype=jnp.int32)[None, :]).astype(F32)
    oh = onehot.reshape(n_asg // chunk, chunk, N_EXPERTS)
    tri = jnp.tril(jnp.ones((chunk, chunk), F32))
    within = jnp.einsum("ij,cjk->cik", tri, oh)
    chunk_tot = within[:, -1, :]
    chunk_end = jnp.cumsum(chunk_tot, axis=0)
    incl = within + (chunk_end - chunk_tot)[:, None, :]
    rank = (jnp.sum(incl * oh, axis=-1) - 1.0).reshape(n_asg).astype(jnp.int32)
    counts = chunk_end[-1].astype(jnp.int32)
    padded = (counts + MOE_BLOCK - 1) // MOE_BLOCK * MOE_BLOCK
    pad_ends = jnp.cumsum(padded)
    dest = (pad_ends - padded)[e_flat] + rank
    n_blk = n_asg // MOE_BLOCK + N_EXPERTS
    blk_start = jnp.arange(n_blk, dtype=jnp.int32) * MOE_BLOCK
    blk_exp = jnp.minimum(jnp.sum((pad_ends[None, :] <= blk_start[:, None]).astype(jnp.int32), axis=1),
                          N_EXPERTS - 1)
    return blk_exp, dest


def _moe_dispatch_kernel(n_steps, dest_ref, x_ref, xs_init_ref, xs_hbm, stage, sems):
    del xs_init_ref
    i = pl.program_id(0)
    tm = x_ref.shape[0]
    slot = i & 1

    def drain(s):
        for _ in range(TOP_K):
            pltpu.make_async_copy(stage.at[s], xs_hbm.at[pl.ds(0, tm), :], sems.at[s]).wait()

    @pl.when(i >= 2)
    def _():
        drain(slot)

    stage[slot] = x_ref[...]

    def issue(r, c):
        for k in range(TOP_K):
            pltpu.make_async_copy(stage.at[slot, pl.ds(r, 1), :],
                                  xs_hbm.at[pl.ds(dest_ref[0, 0, TOP_K * r + k], 1), :],
                                  sems.at[slot]).start(priority=k % 2)
        return c

    lax.fori_loop(0, tm, issue, 0, unroll=8)

    @pl.when(i == n_steps - 1)
    def _():
        drain(slot)
        if n_steps >= 2:
            drain(1 - slot)


def _moe_dispatch(h2d, dest, n_rows):
    n_tok, D = h2d.shape
    tm = MOE_IO_TILE
    n_steps = n_tok // tm
    return pl.pallas_call(
        functools.partial(_moe_dispatch_kernel, n_steps),
        out_shape=jax.ShapeDtypeStruct((n_rows, D), F32),
        grid=(n_steps,),
        in_specs=[
            pl.BlockSpec((1, 1, TOP_K * tm), lambda i: (i, 0, 0), memory_space=pltpu.SMEM),
            pl.BlockSpec((tm, D), lambda i: (i, 0)),
            pl.BlockSpec(memory_space=pl.ANY),
        ],
        out_specs=pl.BlockSpec(memory_space=pl.ANY),
        scratch_shapes=[pltpu.VMEM((2, tm, D), F32), pltpu.SemaphoreType.DMA((2,))],
        input_output_aliases={2: 0},
        compiler_params=_cparams(("arbitrary",)),
        name="moe_dispatch",
    )(dest.reshape(n_steps, 1, TOP_K * tm), h2d, jnp.zeros((n_rows, D), F32))


def _moe_expert_kernel(blk_exp_ref, x_ref, wg_ref, wu_ref, wd_ref, y_ref, wg_b, wu_b, wd_b):
    i = pl.program_id(0)

    @pl.when((i == 0) | (blk_exp_ref[i] != blk_exp_ref[jnp.maximum(i - 1, 0)]))
    def _():
        wg_b[...] = wg_ref[0, 0].astype(BF16)
        wu_b[...] = wu_ref[0, 0].astype(BF16)
        wd_b[...] = wd_ref[0, 0].astype(BF16)

    x = x_ref[...].astype(BF16)
    gate = _dot(x, wg_b[...])
    up = _dot(x, wu_b[...])
    hid = (gate * jax.nn.sigmoid(gate) * up).astype(BF16)
    y_ref[...] = _dot(hid, wd_b[...])


def _moe_experts(xs, blk_exp, layer, w_gate, w_up, w_down):
    n_rows, D = xs.shape
    F = w_gate.shape[3]
    rows = pl.BlockSpec((MOE_BLOCK, D), lambda i, be: (i, 0))
    grid_spec = pltpu.PrefetchScalarGridSpec(
        num_scalar_prefetch=1,
        grid=(n_rows // MOE_BLOCK,),
        in_specs=[
            rows,
            pl.BlockSpec((1, 1, D, F), lambda i, be: (layer, be[i], 0, 0)),
            pl.BlockSpec((1, 1, D, F), lambda i, be: (layer, be[i], 0, 0)),
            pl.BlockSpec((1, 1, F, D), lambda i, be: (layer, be[i], 0, 0)),
        ],
        out_specs=rows,
        scratch_shapes=[pltpu.VMEM((D, F), BF16), pltpu.VMEM((D, F), BF16), pltpu.VMEM((F, D), BF16)],
    )
    return pl.pallas_call(
        _moe_expert_kernel,
        out_shape=jax.ShapeDtypeStruct((n_rows, D), F32),
        grid_spec=grid_spec,
        compiler_params=_cparams(("arbitrary",)),
        name="moe_experts",
    )(blk_exp, xs, w_gate, w_up, w_down)


def _moe_combine_kernel(n_steps, dcur_ref, dnext_ref, w_ref, x_ref, gate_ref, g_ref, b_ref, ys_hbm,
                        o_ref, gbuf, sems):
    i = pl.program_id(0)
    tm = x_ref.shape[0]
    slot = i & 1

    def issue(d_ref, s):
        def body(r, c):
            for k in range(TOP_K):
                pltpu.make_async_copy(ys_hbm.at[pl.ds(d_ref[0, 0, TOP_K * r + k], 1), :],
                                      gbuf.at[s, k, pl.ds(r, 1), :], sems.at[s]).start(priority=k % 2)
            return c
        lax.fori_loop(0, tm, body, 0, unroll=8)

    @pl.when(i == 0)
    def _():
        issue(dcur_ref, slot)

    @pl.when(i + 1 < n_steps)
    def _():
        issue(dnext_ref, 1 - slot)

    for k in range(TOP_K):
        pltpu.make_async_copy(ys_hbm.at[pl.ds(0, tm), :], gbuf.at[slot, k], sems.at[slot]).wait()
    w = w_ref[...]
    y = gbuf[slot, 0] * w[:, 0:1] + gbuf[slot, 1] * w[:, 1:2]
    o_ref[...] = _layer_norm(DN_ALPHA * x_ref[...] + gate_ref[0] * y, g_ref[...], b_ref[...])


def _moe_combine_ln(ys, dest, wts, x, gate, ln_g, ln_b):
    B, S, D = x.shape
    n_tok = B * S
    tm = MOE_IO_TILE
    n_steps = n_tok // tm
    per_b = S // tm
    d3 = dest.reshape(n_steps, 1, TOP_K * tm)
    w2 = wts.transpose(0, 2, 1).reshape(n_tok, TOP_K)
    idx = lambda f: pl.BlockSpec((1, 1, TOP_K * tm), f, memory_space=pltpu.SMEM)
    par = pl.BlockSpec((1, D), lambda i: (0, 0))
    row = pl.BlockSpec((tm, D), lambda i: (i, 0))
    out = pl.pallas_call(
        functools.partial(_moe_combine_kernel, n_steps),
        out_shape=jax.ShapeDtypeStruct((n_tok, D), F32),
        grid=(n_steps,),
        in_specs=[
            idx(lambda i: (i, 0, 0)),
            idx(lambda i: (jnp.minimum(i + 1, n_steps - 1), 0, 0)),
            pl.BlockSpec((tm, TOP_K), lambda i: (i, 0)),
            row,
            pl.BlockSpec((1, 1, D), lambda i: (i // per_b, 0, 0)),
            par, par,
            pl.BlockSpec(memory_space=pl.ANY),
        ],
        out_specs=row,
        scratch_shapes=[pltpu.VMEM((2, TOP_K, tm, D), F32), pltpu.SemaphoreType.DMA((2,))],
        compiler_params=_cparams(("arbitrary",)),
        name="moe_combine_ln",
    )(d3, d3, w2, x.reshape(n_tok, D), gate, ln_g.reshape(1, D), ln_b.reshape(1, D), ys)
    return out.reshape(B, S, D)


def _moe_sublayer(x1, h2, lgT, router_b, layer, w_gate, w_up, w_down, gate, ln_g, ln_b):
    B, S, D = x1.shape
    eidx, wts = _route(lgT, router_b)
    blk_exp, dest = _dispatch_plan(eidx)
    n_rows = blk_exp.shape[0] * MOE_BLOCK
    xs = _moe_dispatch(h2.reshape(B * S, D), dest, n_rows)
    ys = _moe_experts(xs, blk_exp, layer, w_gate, w_up, w_down)
    return _moe_combine_ln(ys, dest, wts, x1, gate, ln_g, ln_b)


def _dil_weight_cols():
    d = D_MODEL
    cols = list(range(d))
    for which in range(2):
        for p in range(len(DIL_PATTERNS)):
            base = d + (p * 2 + which) * KV_COLS
            cols += list(range(base, base + KV_COLS))
    return np.asarray(cols)


def _dil_inproj_kernel(x_ref, sh_ref, sc_ref, w_ref, c_ref, s1_ref, s2_ref, q_ref, *kv_refs):
    h = (x_ref[0] * (1.0 + sc_ref[0]) + sh_ref[0]).astype(BF16)
    c, s1, s2 = c_ref[0], s1_ref[0], s2_ref[0]
    d = q_ref.shape[2]
    w = KV_COLS
    n_pat = len(DIL_PATTERNS)
    for j in range(d // w):
        a = _rope_cols(_dot(h, w_ref[:, j * w:(j + 1) * w]), c, s1, s2) * Q_SCALE
        q_ref[0, :, j * w:(j + 1) * w] = a.astype(BF16)
    for p in range(n_pat):
        a = _rope_cols(_dot(h, w_ref[:, d + p * w:d + (p + 1) * w]), c, s1, s2)
        kv_refs[p][0] = a.astype(BF16)
    for p in range(n_pat):
        a = _dot(h, w_ref[:, d + (n_pat + p) * w:d + (n_pat + p + 1) * w])
        kv_refs[n_pat + p][0] = a.astype(BF16)


def _dil_inproj(x, shift, scale, w_in, tabs):
    B, S, D = x.shape
    tm = ROW_TILE
    wp = _permute_cols(w_in, _dil_weight_cols())
    ncol = wp.shape[1]
    n_pat = len(DIL_PATTERNS)
    vec = pl.BlockSpec((1, 1, D), lambda b, i: (b, 0, 0))
    tab = pl.BlockSpec((1, tm, LANES), lambda b, i: (b, i, 0))
    kvspec = pl.BlockSpec((1, tm, KV_COLS), lambda b, i: (b, i, 0))
    outs = pl.pallas_call(
        _dil_inproj_kernel,
        out_shape=[jax.ShapeDtypeStruct((B, S, D), BF16)]
        + [jax.ShapeDtypeStruct((B, S, KV_COLS), BF16)] * (2 * n_pat),
        grid=(B, S // tm),
        in_specs=[pl.BlockSpec((1, tm, D), lambda b, i: (b, i, 0)), vec, vec,
                  pl.BlockSpec((D, ncol), lambda b, i: (0, 0)), tab, tab, tab],
        out_specs=[pl.BlockSpec((1, tm, D), lambda b, i: (b, i, 0))] + [kvspec] * (2 * n_pat),
        compiler_params=_cparams(("parallel", "parallel")),
        name="dil_inproj",
    )(x, shift, scale, wp, *tabs)
    return outs[0], outs[1:1 + n_pat], outs[1 + n_pat:]


def _dil_attn_kernel(steps, first, last, *refs):
    q_ref = refs[0]
    blk = q_ref.shape[1]
    kt = DIL_BLOCK
    n_kt = blk // kt + 1
    k_refs, v_refs = refs[1:1 + n_kt], refs[1 + n_kt:1 + 2 * n_kt]
    if first:
        acc_in = ml_in = None
        outs = refs[1 + 2 * n_kt:]
    else:
        acc_in, ml_in = refs[1 + 2 * n_kt:3 + 2 * n_kt]
        outs = refs[3 + 2 * n_kt:]
    nb = pl.program_id(2)
    e = HEAD_DIM
    gw = GQA_REP * e
    kj = lax.broadcasted_iota(jnp.int32, (n_kt * kt, blk), 0)
    qi = lax.broadcasted_iota(jnp.int32, (n_kt * kt, blk), 1)
    dist = kt + qi - kj
    valid = (dist >= 0) & (dist <= steps) & ((nb * (blk // kt) - 1) * kt + kj >= 0)
    bias = jnp.where(valid, 0.0, NEG_INF)
    bias = jnp.concatenate([bias] * GQA_REP, axis=1)
    zeros_half = jnp.zeros((e, GQA_REP * blk), F32)
    mlT_old = None if first else ml_in[0].T
    m_rows, l_rows = [], []

    def heads_to_lanes(t):
        return jnp.concatenate([t[r * e:(r + 1) * e] for r in range(GQA_REP)], axis=1)

    for g in range(N_KV_HEADS):
        seg = slice((g // 2) * LANES, (g // 2 + 1) * LANES)
        qs = slice(g * gw, (g + 1) * gw)
        qTm = heads_to_lanes(q_ref[0][:, qs].astype(F32).T)
        qT2 = jnp.concatenate([qTm, zeros_half] if g % 2 == 0 else [zeros_half, qTm], axis=0).astype(BF16)
        kcat = jnp.concatenate([r[0][:, seg] for r in k_refs], axis=0)
        s = _dot(kcat, qT2) + bias
        m_new = jnp.max(s, axis=0, keepdims=True)
        if not first:
            m_old = jnp.concatenate([mlT_old[g * GQA_REP + r:g * GQA_REP + r + 1] for r in range(GQA_REP)], axis=1)
            l_old = jnp.concatenate([mlT_old[N_Q_HEADS + g * GQA_REP + r:N_Q_HEADS + g * GQA_REP + r + 1]
                                     for r in range(GQA_REP)], axis=1)
            m_new = jnp.maximum(m_old, m_new)
            alpha = jnp.exp2(m_old - m_new)
        p = jnp.exp2(s - m_new)
        l_new = jnp.sum(p, axis=0, keepdims=True)
        vcat = jnp.concatenate([r[0][:, seg] for r in v_refs], axis=0).astype(F32)
        vT = vcat.T[(g % 2) * e:(g % 2 + 1) * e].astype(BF16)
        accT = _dot(vT, p.astype(BF16))
        if not first:
            l_new = l_new + alpha * l_old
            accT = accT + alpha * heads_to_lanes(acc_in[0][:, qs].T)
        if last:
            accT = accT / l_new
        o_rows = jnp.concatenate([accT[:, r * blk:(r + 1) * blk] for r in range(GQA_REP)], axis=0)
        outs[0][0, :, qs] = o_rows.T.astype(outs[0].dtype)
        m_rows += [m_new[:, r * blk:(r + 1) * blk] for r in range(GQA_REP)]
        l_rows += [l_new[:, r * blk:(r + 1) * blk] for r in range(GQA_REP)]
    if not last:
        pad = jnp.zeros((LANES - 2 * N_Q_HEADS, blk), F32)
        outs[1][0] = jnp.concatenate(m_rows + l_rows + [pad], axis=0).T


def _dil_attention(q, kds, vds):
    B, S, D = q.shape
    n_pat = len(DIL_PATTERNS)
    acc = ml = None
    for p, (window, dil) in enumerate(DIL_PATTERNS):
        first, last = p == 0, p == n_pat - 1
        L = S // dil
        qb = DIL_Q_TILE
        n_kt = qb // DIL_BLOCK + 1
        nblk = L // qb
        kw = KV_COLS

        def view(a):
            return a.reshape(B, L, dil * a.shape[2])

        cur = lambda b, c, n: (b, n, c)
        def key_tile(j):
            return lambda b, c, n: (b, jnp.maximum(n * (n_kt - 1) - 1 + j, 0), c)

        qspec = pl.BlockSpec((1, qb, D), cur)
        kv_specs = [pl.BlockSpec((1, DIL_BLOCK, kw), key_tile(j)) for j in range(n_kt)]
        in_specs = [qspec] + kv_specs + kv_specs
        args = [view(q)] + [view(kds[p])] * n_kt + [view(vds[p])] * n_kt
        if not first:
            in_specs += [qspec, pl.BlockSpec((1, qb, LANES), cur)]
            args += [view(acc), view(ml)]
        if last:
            out_shape = [jax.ShapeDtypeStruct((B, L, dil * D), BF16)]
            out_specs = [qspec]
        else:
            out_shape = [jax.ShapeDtypeStruct((B, L, dil * D), F32),
                         jax.ShapeDtypeStruct((B, L, dil * LANES), F32)]
            out_specs = [qspec, pl.BlockSpec((1, qb, LANES), cur)]
        res = pl.pallas_call(
            functools.partial(_dil_attn_kernel, window // dil, first, last),
            out_shape=out_shape,
            grid=(B, dil, nblk),
            in_specs=in_specs,
            out_specs=out_specs,
            compiler_params=_cparams(("parallel", "parallel", "arbitrary")),
            name=f"dil_attention_{p}",
        )(*args)
        if last:
            return res[0].reshape(B, S, D)
        acc, ml = res[0].reshape(B, S, D), res[1].reshape(B, S, LANES)


def kernel(x, c, positions, ada_w, ada_b, ln_g, ln_b, nsa_w_in, nsa_cmp_pos_k, nsa_cmp_w1_k, nsa_cmp_w2_k, nsa_cmp_pos_v, nsa_cmp_w1_v, nsa_cmp_w2_v, nsa_w_o, dil_w_in, dil_w_o, router_w, router_b, moe_w_gate, moe_w_up, moe_w_down):
    B, S, D = x.shape
    mods = _ada_mods(c, ada_w, ada_b)
    def mod(i, sub):
        m = mods[i * 2 + sub]
        return [m[:, k * D:(k + 1) * D].reshape(B, 1, D) for k in range(3)]
    tabs = _rope_tables(positions)

    for i in range(DEPTH):
        shift, scale, gate = mod(i, 0)
        shift2, scale2, gate2 = mod(i, 1)
        j = i // 2
        if i % 2 == 0:
            qT, ksel, kwin, vTsel, vTwin, kcmp, vcmp, gT = _nsa_inproj(x, shift, scale, nsa_w_in[j], tabs)
            kc, vcT = _compress(kcmp, vcmp, nsa_cmp_pos_k[j], nsa_cmp_w1_k[j], nsa_cmp_w2_k[j],
                                nsa_cmp_pos_v[j], nsa_cmp_w1_v[j], nsa_cmp_w2_v[j], tabs)
            o = _nsa_attention(qT, kc, vcT, ksel, vTsel, kwin, vTwin, gT)
            w_o = nsa_w_o[j]
        else:
            q, kds, vds = _dil_inproj(x, shift, scale, dil_w_in[j], tabs)
            o = _dil_attention(q, kds, vds)
            w_o = dil_w_o[j]
        x1, h2, lgT = _proj_ln(o, x, w_o, gate, ln_g[i, 0], ln_b[i, 0], shift2, scale2, router_w)
        x = _moe_sublayer(x1, h2, lgT, router_b, i, moe_w_gate, moe_w_up, moe_w_down,
                          gate2, ln_g[i, 1], ln_b[i, 1])
    return x
```
